```python
import math
import jax, jax.numpy as jnp
from jax import lax
import numpy as np

D_MODEL = 1024
BATCH = 2
SEQ = 8192
DEPTH = 1
DEC_BATCH = 4
DEC_SEQ = 4096
PAST_LEN = 128

GRID_W = 64
HEAD_DIM = 64
M_HEADS = 8
A_HEADS = 8
M_WIDTH = M_HEADS * HEAD_DIM
A_WIDTH = A_HEADS * HEAD_DIM
MIX_WIDTH = M_WIDTH + A_WIDTH
N_GATES = 4 * M_HEADS
IN_COLS = 4 * M_WIDTH + N_GATES + 3 * A_WIDTH
SPLITS = (M_WIDTH, 2 * M_WIDTH, 3 * M_WIDTH, 4 * M_WIDTH, 4 * M_WIDTH + N_GATES,
          4 * M_WIDTH + N_GATES + A_WIDTH, 4 * M_WIDTH + N_GATES + 2 * A_WIDTH)
CONV_W = 3
CHUNK = 64
WIN_H_MAX = 8
WIN_W = 16
N_EXPERTS = 32
TOP_K = 4
D_FF = 1024
SWIGLU_LIMIT = 7.0
SWIGLU_ALPHA = 1.702
PLE_DIM = 256
MOE_BLOCK = 128
EPS = 1e-6

kernel_name = "hybrid_mlstm_natten_moe_encoder"

f32 = jnp.float32


def rmsnorm(x, g):
    xf = x.astype(f32)
    y = xf * lax.rsqrt(jnp.mean(xf * xf, axis=-1, keepdims=True) + EPS)
    return (y * g.astype(f32)).astype(x.dtype)


def centred_dwconv(u, w):
    pad = CONV_W // 2
    T = u.shape[1]
    up = jnp.pad(u, ((0, 0), (pad, pad), (0, 0)))
    out = up[:, 0:T] * w[0]
    for j in range(1, CONV_W):
        out = out + up[:, j:j + T] * w[j]
    return out


def mlstm_chunkwise(q, k, v, log_i, log_f):
    B, H, T, d = q.shape
    nc = T // CHUNK

    def to_chunks(a):
        return jnp.moveaxis(a.reshape(a.shape[:2] + (nc, CHUNK) + a.shape[3:]), 2, 0)

    qc, kc, vc = to_chunks(q), to_chunks(k), to_chunks(v)
    ic, fc = to_chunks(log_i), to_chunks(log_f)
    lower = jnp.tril(jnp.ones((CHUNK, CHUNK), dtype=bool))

    def step(carry, inp):
        C, n, m = carry
        q_, k_, v_, li, lf = inp
        b = jnp.cumsum(lf, axis=-1)
        D = b[..., :, None] - b[..., None, :] + li[..., None, :]
        D = jnp.where(lower, D, -jnp.inf)
        inter = b + m[..., None]
        m_t = jnp.maximum(inter, jnp.max(D, axis=-1))
        w_inter = jnp.exp(inter - m_t)
        S = jnp.einsum('bhld,bhsd->bhls', q_, k_) * jnp.exp(D - m_t[..., None])
        num = w_inter[..., None] * jnp.einsum('bhld,bhde->bhle', q_, C) + jnp.einsum('bhls,bhse->bhle', S, v_)
        den = w_inter * jnp.einsum('bhld,bhd->bhl', q_, n) + jnp.sum(S, axis=-1)
        h = num / jnp.maximum(jnp.abs(den), jnp.exp(-m_t))[..., None]
        bL = b[..., -1]
        g = bL[..., None] - b + li
        m_new = jnp.maximum(bL + m, jnp.max(g, axis=-1))
        decay = jnp.exp(bL + m - m_new)
        wk = jnp.exp(g - m_new[..., None])
        kw = k_ * wk[..., None]
        C_new = decay[..., None, None] * C + jnp.einsum('bhsd,bhse->bhde', kw, v_)
        n_new = decay[..., None] * n + jnp.sum(kw, axis=2)
        return (C_new, n_new, m_new), h

    init = (jnp.zeros((B, H, d, d), f32), jnp.zeros((B, H, d), f32), jnp.zeros((B, H), f32))
    _, hs = lax.scan(step, init, (qc, kc, vc, ic, fc))
    return jnp.moveaxis(hs, 0, 2).reshape(B, H, T, d)


def mlstm_mixer(q, k, v, gates):
    B, T, _ = q.shape

    def heads(a):
        return a.reshape(B, T, M_HEADS, HEAD_DIM).transpose(0, 2, 1, 3).astype(f32)

    qh, kh, vh = heads(q), heads(k) * (HEAD_DIM ** -0.5), heads(v)
    g = gates.astype(f32).reshape(B, T, 4, M_HEADS).transpose(2, 0, 3, 1)
    li_f, lf_f = g[0], jax.nn.log_sigmoid(g[1])
    li_b, lf_b = g[2], jax.nn.log_sigmoid(g[3])
    h_f = mlstm_chunkwise(qh, kh, vh, li_f, lf_f)
    flip = lambda a: jnp.flip(a, axis=2)
    h_b = flip(mlstm_chunkwise(flip(qh), flip(kh), flip(vh), flip(li_b), flip(lf_b)))
    return (h_f + h_b).transpose(0, 2, 1, 3)


def neighbourhood_attention(q, k, v, g_q, g_k, rel_bias):
    B, T, _ = q.shape
    rows = T // GRID_W
    win_h = min(WIN_H_MAX, rows)

    def grid(a):
        return a.reshape(B, rows, GRID_W, A_HEADS, HEAD_DIM).transpose(0, 3, 1, 2, 4)

    qg = rmsnorm(grid(q), g_q) * (HEAD_DIM ** -0.5)
    kg = rmsnorm(grid(k), g_k)
    vg = grid(v)
    r = jnp.arange(rows)
    r0 = jnp.clip(r - win_h // 2, 0, rows - win_h)
    key_rows = r0[:, None] + jnp.arange(win_h)[None, :]
    k_blk = kg[:, :, key_rows]
    v_blk = vg[:, :, key_rows]
    c = jnp.arange(GRID_W)
    c0 = jnp.clip(c - WIN_W // 2, 0, GRID_W - WIN_W)
    col_in = (c[None, :] >= c0[:, None]) & (c[None, :] < c0[:, None] + WIN_W)
    dr = key_rows - r[:, None]
    dc = jnp.clip(c[None, :] - c[:, None], -(WIN_W - 1), WIN_W - 1)
    idx_r = (dr + (WIN_H_MAX - 1))[:, None, :, None]
    idx_c = (dc + (WIN_W - 1))[None, :, None, :]
    bias = rel_bias.astype(f32)[:, idx_r, idx_c]
    bias = jnp.where(col_in[None, None, :, None, :], bias, -jnp.inf)
    s = jnp.einsum('bhrqd,bhrjkd->bhrqjk', qg, k_blk).astype(f32) + bias[None]
    sh = s.shape
    p = jax.nn.softmax(s.reshape(sh[:4] + (win_h * GRID_W,)), axis=-1).reshape(sh)
    o = jnp.einsum('bhrqjk,bhrjkd->bhrqd', p.astype(v_blk.dtype), v_blk)
    return o.transpose(0, 2, 3, 1, 4).reshape(B, T, A_WIDTH)


def clamped_swiglu_expert(xblk, w_gu, b_gu, w_d, b_d):
    h = xblk @ w_gu + b_gu
    gt = jnp.minimum(h[:, 0::2], SWIGLU_LIMIT)
    up = jnp.clip(h[:, 1::2], -SWIGLU_LIMIT, SWIGLU_LIMIT)
    act = (up + 1.0) * (gt * jax.nn.sigmoid(SWIGLU_ALPHA * gt))
    return act @ w_d + b_d


def moe(x, w_router, b_router, w_gate_up, b_gate_up, w_down, b_down):
    B, T, D = x.shape
    xf = x.reshape(-1, D)
    N = xf.shape[0]
    logits = (xf @ w_router + b_router).astype(f32)
    top_v, top_i = lax.top_k(logits, TOP_K)
    gate = jax.nn.softmax(top_v, axis=-1)
    NK = N * TOP_K
    flat_e = top_i.reshape(-1)
    flat_tok = jnp.arange(NK, dtype=jnp.int32) // TOP_K
    flat_g = gate.reshape(-1)
    order = jnp.argsort(flat_e)
    se, stok, sg = flat_e[order], flat_tok[order], flat_g[order]
    counts = jnp.bincount(flat_e, length=N_EXPERTS)
    padded = (counts + MOE_BLOCK - 1) // MOE_BLOCK * MOE_BLOCK
    start = jnp.cumsum(counts) - counts
    pend = jnp.cumsum(padded)
    pstart = pend - padded
    dest = pstart[se] + jnp.arange(NK, dtype=jnp.int32) - start[se]
    cap = -(-(NK + N_EXPERTS * (MOE_BLOCK - 1)) // MOE_BLOCK) * MOE_BLOCK
    n_blocks = cap // MOE_BLOCK
    buf_tok = jnp.zeros((cap,), jnp.int32).at[dest].set(stok)
    buf_g = jnp.zeros((cap,), f32).at[dest].set(sg)
    block_e = jnp.searchsorted(pend, jnp.arange(n_blocks) * MOE_BLOCK, side='right')
    block_e = jnp.minimum(block_e, N_EXPERTS - 1)
    xb = xf[buf_tok].reshape(n_blocks, MOE_BLOCK, D)

    def run_block(args):
        xblk, e = args
        return clamped_swiglu_expert(xblk, w_gate_up[e], b_gate_up[e], w_down[e], b_down[e])

    yb = lax.map(run_block, (xb, block_e)).reshape(cap, D).astype(f32)
    y = jnp.zeros((N, D), f32).at[buf_tok].add(yb * buf_g[:, None])
    return y.astype(x.dtype).reshape(B, T, D)


def encoder_layer(x, p, g_mix, w_in, b_gates, conv_w, g_m_head, g_q, g_k, rel_bias, g_a_out, w_out,
                  g_ffn, w_router, b_router, w_gate_up, b_gate_up, w_down, b_down,
                  g_ple, w_ple_gate, w_ple_proj, g_ple_post):
    B, T, _ = x.shape
    h = rmsnorm(x, g_mix)
    z = h @ w_in
    q_m, k_m, v_m, o_m, gates, q_a, k_a, v_a = jnp.split(z, list(SPLITS), axis=-1)
    qk = jax.nn.silu(centred_dwconv(jnp.concatenate([q_m, k_m], axis=-1), conv_w))
    q_m, k_m = qk[..., :M_WIDTH], qk[..., M_WIDTH:]
    hm = mlstm_mixer(q_m, k_m, v_m, gates + b_gates)
    hm = rmsnorm(hm, g_m_head).reshape(B, T, M_WIDTH).astype(x.dtype) * jax.nn.sigmoid(o_m)
    ha = rmsnorm(neighbourhood_attention(q_a, k_a, v_a, g_q, g_k, rel_bias), g_a_out)
    x = x + jnp.concatenate([hm, ha.astype(x.dtype)], axis=-1) @ w_out
    x = x + moe(rmsnorm(x, g_ffn), w_router, b_router, w_gate_up, b_gate_up, w_down, b_down)
    pe = rmsnorm(p @ w_ple_proj, g_ple_post)
    x = x + jax.nn.sigmoid(rmsnorm(x, g_ple) @ w_ple_gate) * pe
    return x


def setup_inputs(seed: int = 0) -> dict:
    key = jax.random.key(seed)
    ks = jax.random.split(key, 32)
    nrm = lambda k, shape, s: jax.random.normal(k, shape, f32) * s
    gain = lambda k, shape: 1.0 + 0.05 * jax.random.normal(k, shape, f32)
    L = DEPTH
    f_bias = jnp.linspace(3.0, 6.0, M_HEADS, dtype=f32)
    b_gates = jnp.concatenate([
        nrm(ks[4], (L, M_HEADS), 0.1),
        f_bias[None] + nrm(ks[5], (L, M_HEADS), 0.1),
        nrm(ks[6], (L, M_HEADS), 0.1),
        f_bias[None] + nrm(ks[7], (L, M_HEADS), 0.1)], axis=-1)
    return {
        "x_prompt": nrm(ks[0], (BATCH, SEQ, D_MODEL), 1.0),
        "x_sample": nrm(ks[1], (DEC_BATCH, DEC_SEQ, D_MODEL), 1.0),
        "p_prompt": nrm(ks[2], (DEPTH, BATCH, SEQ, PLE_DIM), 1.0),
        "p_sample": nrm(ks[3], (DEPTH, DEC_BATCH, DEC_SEQ, PLE_DIM), 1.0),
        "g_mix": gain(ks[8], (L, D_MODEL)),
        "w_in": nrm(ks[9], (L, D_MODEL, IN_COLS), D_MODEL ** -0.5),
        "b_gates": b_gates,
        "conv_w": nrm(ks[10], (L, CONV_W, 2 * M_WIDTH), 0.5),
        "g_m_head": gain(ks[11], (L, M_HEADS, HEAD_DIM)),
        "g_q": gain(ks[12], (L, HEAD_DIM)),
        "g_k": gain(ks[13], (L, HEAD_DIM)),
        "rel_bias": nrm(ks[14], (L, A_HEADS, 2 * WIN_H_MAX - 1, 2 * WIN_W - 1), 0.1),
        "g_a_out": gain(ks[15], (L, A_WIDTH)),
        "w_out": nrm(ks[16], (L, MIX_WIDTH, D_MODEL), MIX_WIDTH ** -0.5),
        "g_ffn": gain(ks[17], (L, D_MODEL)),
        "w_router": nrm(ks[18], (L, D_MODEL, N_EXPERTS), D_MODEL ** -0.5),
        "b_router": nrm(ks[19], (L, N_EXPERTS), 0.01),
        "w_gate_up": nrm(ks[20], (L, N_EXPERTS, D_MODEL, 2 * D_FF), D_MODEL ** -0.5),
        "b_gate_up": nrm(ks[21], (L, N_EXPERTS, 2 * D_FF), 0.01),
        "w_down": nrm(ks[22], (L, N_EXPERTS, D_FF, D_MODEL), D_FF ** -0.5),
        "b_down": nrm(ks[23], (L, N_EXPERTS, D_MODEL), 0.01),
        "g_ple": gain(ks[24], (L, D_MODEL)),
        "w_ple_gate": nrm(ks[25], (L, D_MODEL, D_MODEL), D_MODEL ** -0.5),
        "w_ple_proj": nrm(ks[26], (L, PLE_DIM, D_MODEL), PLE_DIM ** -0.5),
        "g_ple_post": gain(ks[27], (L, D_MODEL)),
    }


def run_trunk(x, p, g_mix, w_in, b_gates, conv_w, g_m_head, g_q, g_k, rel_bias, g_a_out, w_out,
              g_ffn, w_router, b_router, w_gate_up, b_gate_up, w_down, b_down,
              g_ple, w_ple_gate, w_ple_proj, g_ple_post):
    for i in range(DEPTH):
        x = encoder_layer(x, p[i], g_mix[i], w_in[i], b_gates[i], conv_w[i], g_m_head[i], g_q[i], g_k[i],
                          rel_bias[i], g_a_out[i], w_out[i], g_ffn[i], w_router[i], b_router[i],
                          w_gate_up[i], b_gate_up[i], w_down[i], b_down[i],
                          g_ple[i], w_ple_gate[i], w_ple_proj[i], g_ple_post[i])
    return x


def reference(x_prompt, x_sample, p_prompt, p_sample, g_mix, w_in, b_gates, conv_w, g_m_head, g_q, g_k,
              rel_bias, g_a_out, w_out, g_ffn, w_router, b_router, w_gate_up, b_gate_up, w_down, b_down,
              g_ple, w_ple_gate, w_ple_proj, g_ple_post):
    y_prompt = run_trunk(x_prompt, p_prompt, g_mix, w_in, b_gates, conv_w, g_m_head, g_q, g_k, rel_bias,
                         g_a_out, w_out, g_ffn, w_router, b_router, w_gate_up, b_gate_up, w_down, b_down,
                         g_ple, w_ple_gate, w_ple_proj, g_ple_post)
    y_sample = run_trunk(x_sample, p_sample, g_mix, w_in, b_gates, conv_w, g_m_head, g_q, g_k, rel_bias,
                         g_a_out, w_out, g_ffn, w_router, b_router, w_gate_up, b_gate_up, w_down, b_down,
                         g_ple, w_ple_gate, w_ple_proj, g_ple_post)
    return (y_prompt, y_sample)
```

```python
import functools

import jax
import jax.numpy as jnp
from jax import lax
from jax.experimental import pallas as pl
from jax.experimental.pallas import tpu as pltpu

f32 = jnp.float32
bf16 = jnp.bfloat16
i32 = jnp.int32

D_MODEL = 1024
HEAD_DIM = 64
M_HEADS = 8
A_HEADS = 8
M_WIDTH = M_HEADS * HEAD_DIM
A_WIDTH = A_HEADS * HEAD_DIM
N_GATES = 4 * M_HEADS
GRID_W = 64
WIN_H = 8
WIN_W = 16
N_EXPERTS = 32
TOP_K = 4
D_FF = 1024
SWIGLU_LIMIT = 7.0
SWIGLU_ALPHA = 1.702
PLE_DIM = 256
EPS = 1e-6

LANES = 128
SUBLANES = 8
VMEM_LIMIT = 56 * 1024 * 1024

TM_IN = 512
MLSTM_CHUNK = 256
NAT_ROWS = 8
TM_OUT = 512
TM_EXP = 256
DISPATCH_CHUNK = 512
TM_CMB = 256

NEG_INF = float("-inf")


def _cparams(sem):
    return pltpu.CompilerParams(dimension_semantics=sem, vmem_limit_bytes=VMEM_LIMIT)


def _rms(xv, g):
    return xv * lax.rsqrt(jnp.mean(xv * xv, axis=-1, keepdims=True) + EPS) * g


def _split2(a):
    hi = a.astype(bf16)
    lo = (a - hi.astype(f32)).astype(bf16)
    return hi, lo


def _split3(a):
    hi = a.astype(bf16)
    r = a - hi.astype(f32)
    mid = r.astype(bf16)
    lo = (r - mid.astype(f32)).astype(bf16)
    return hi, mid, lo


def _head_mean_sq(xv, width):
    a = lax.broadcasted_iota(i32, (width, width), 0) // HEAD_DIM
    b = lax.broadcasted_iota(i32, (width, width), 1) // HEAD_DIM
    bd = jnp.where(a == b, 1.0 / HEAD_DIM, 0.0).astype(bf16)
    hi, lo = _split2(xv * xv)
    return (jnp.dot(hi, bd, preferred_element_type=f32) + jnp.dot(lo, bd, preferred_element_type=f32))


def _log_sigmoid(x):
    return jnp.minimum(x, 0.0) - jnp.log1p(jnp.exp(-jnp.abs(x)))


def _inproj_kernel(x_ref, xp_ref, xn_ref, g_ref, wm_ref, wg_ref, bg_ref, cw_ref,
                   qk_ref, v_ref, o_ref, gates_ref, qa_ref, ka_ref, va_ref, *, tm, seq_len):
    i = pl.program_id(0)
    g = g_ref[...]
    h = _rms(x_ref[...], g).astype(bf16)
    z = jnp.dot(h, wm_ref[...], preferred_element_type=f32)
    gates_ref[...] = jnp.dot(h, wg_ref[...], preferred_element_type=f32) + bg_ref[...]
    hh = _rms(jnp.concatenate([xp_ref[...], xn_ref[...]], axis=0), g).astype(bf16)
    zh = jnp.dot(hh, wm_ref[:, :2 * M_WIDTH], preferred_element_type=f32)
    start = lax.rem(i * tm, seq_len)
    prev_row = jnp.where(start == 0, 0.0, zh[SUBLANES - 1:SUBLANES, :])
    next_row = jnp.where(start + tm == seq_len, 0.0, zh[SUBLANES:SUBLANES + 1, :])
    u = z[:, :2 * M_WIDTH]
    rid = lax.broadcasted_iota(i32, (tm, 1), 0)
    u_prev = jnp.where(rid == 0, prev_row, pltpu.roll(u, 1, 0))
    u_next = jnp.where(rid == tm - 1, next_row, pltpu.roll(u, tm - 1, 0))
    cw = cw_ref[...]
    c = u_prev * cw[0:1, :] + u * cw[1:2, :] + u_next * cw[2:3, :]
    qk_ref[...] = (c * jax.nn.sigmoid(c)).astype(bf16)
    v_ref[...] = z[:, 2 * M_WIDTH:3 * M_WIDTH].astype(bf16)
    o_ref[...] = z[:, 3 * M_WIDTH:4 * M_WIDTH]
    base = 4 * M_WIDTH
    qa_ref[...] = z[:, base:base + A_WIDTH]
    ka_ref[...] = z[:, base + A_WIDTH:base + 2 * A_WIDTH]
    va_ref[...] = z[:, base + 2 * A_WIDTH:base + 3 * A_WIDTH].astype(bf16)


def _inproj(x2, g_mix, w_main, w_gate, b_gate, conv_w, seq_len):
    n = x2.shape[0]
    tm = TM_IN
    nb8 = n // SUBLANES
    r8 = tm // SUBLANES
    wcols = w_main.shape[1]
    row = lambda i: (i, 0)
    const = lambda i: (0, 0)
    out_shapes = (
        jax.ShapeDtypeStruct((n, 2 * M_WIDTH), bf16),
        jax.ShapeDtypeStruct((n, M_WIDTH), bf16),
        jax.ShapeDtypeStruct((n, M_WIDTH), f32),
        jax.ShapeDtypeStruct((n, LANES), f32),
        jax.ShapeDtypeStruct((n, A_WIDTH), f32),
        jax.ShapeDtypeStruct((n, A_WIDTH), f32),
        jax.ShapeDtypeStruct((n, A_WIDTH), bf16),
    )
    return pl.pallas_call(
        functools.partial(_inproj_kernel, tm=tm, seq_len=seq_len),
        grid=(n // tm,),
        in_specs=[
            pl.BlockSpec((tm, D_MODEL), row),
            pl.BlockSpec((SUBLANES, D_MODEL), lambda i: (jnp.maximum(i * r8 - 1, 0), 0)),
            pl.BlockSpec((SUBLANES, D_MODEL), lambda i: (jnp.minimum((i + 1) * r8, nb8 - 1), 0)),
            pl.BlockSpec((1, D_MODEL), const),
            pl.BlockSpec((D_MODEL, wcols), const),
            pl.BlockSpec((D_MODEL, LANES), const),
            pl.BlockSpec((1, LANES), const),
            pl.BlockSpec((3, 2 * M_WIDTH), const),
        ],
        out_specs=[
            pl.BlockSpec((tm, 2 * M_WIDTH), row),
            pl.BlockSpec((tm, M_WIDTH), row),
            pl.BlockSpec((tm, M_WIDTH), row),
            pl.BlockSpec((tm, LANES), row),
            pl.BlockSpec((tm, A_WIDTH), row),
            pl.BlockSpec((tm, A_WIDTH), row),
            pl.BlockSpec((tm, A_WIDTH), row),
        ],
        out_shape=out_shapes,
        compiler_params=_cparams(("parallel",)),
        name="inproj",
    )(x2, x2, x2, g_mix, w_main, w_gate, b_gate, conv_w)


def _mlstm_kernel(q_ref, k_ref, v_ref, gc_ref, gr_ref, out_ref, c_ref, m_ref, *, L):
    d = pl.program_id(1)
    c = pl.program_id(2)

    @pl.when(c == 0)
    def _():
        c_ref[...] = jnp.zeros_like(c_ref)
        m_ref[...] = jnp.zeros_like(m_ref)

    sgn = jnp.where(d == 0, 1, -1)
    row = lax.broadcasted_iota(i32, (L, L), 0)
    col = lax.broadcasted_iota(i32, (L, L), 1)
    mask = (row - col) * sgn >= 0
    t_col = mask.astype(bf16)
    t_row = ((col - row) * sgn >= 0).astype(bf16)

    gc = gc_ref[...]
    gr = gr_ref[...]
    li_c = gc[:, 0:M_HEADS]
    lf_c = _log_sigmoid(gc[:, M_HEADS:2 * M_HEADS])
    li_r = gr[0:M_HEADS, :]
    lf_r = _log_sigmoid(gr[M_HEADS:2 * M_HEADS, :])
    b_c = sum(jnp.dot(t_col, part, preferred_element_type=f32) for part in _split3(lf_c))
    b_r = sum(jnp.dot(part, t_row, preferred_element_type=f32) for part in _split3(lf_r))
    bl_all = jnp.sum(lf_c, axis=0, keepdims=True)

    lane = lax.broadcasted_iota(i32, (1, LANES), 1)
    halves = (lane < HEAD_DIM, lane >= HEAD_DIM)
    nt = (((1,), (1,)), ((), ()))
    tn = (((0,), (0,)), ((), ()))

    for p in range(M_HEADS // 2):
        sl = slice(p * LANES, (p + 1) * LANES)
        q2 = q_ref[:, sl]
        k2 = k_ref[:, sl]
        v2 = v_ref[:, sl]
        outs = []
        for half in range(2):
            h = 2 * p + half
            inh = halves[half]
            qz = jnp.where(inh, q2 * (HEAD_DIM ** -0.5), jnp.zeros_like(q2))
            kz = jnp.where(inh, k2, jnp.zeros_like(k2))
            va = jnp.where(inh, v2, jnp.ones_like(v2))
            bc = b_c[:, h:h + 1]
            br = b_r[h:h + 1, :]
            lic = li_c[:, h:h + 1]
            lir = li_r[h:h + 1, :]
            bl = bl_all[:, h:h + 1]
            m_prev = m_ref[h:h + 1, 0:1]
            dmat = jnp.where(mask, bc - br + lir, NEG_INF)
            inter = bc + m_prev
            m_t = jnp.maximum(inter, jnp.max(dmat, axis=1, keepdims=True))
            w_int = jnp.exp(inter - m_t)
            s = lax.dot_general(qz, k2, nt, preferred_element_type=f32) * jnp.exp(dmat - m_t)
            cst = c_ref[h]
            num = (w_int * jnp.dot(qz, cst.astype(bf16), preferred_element_type=f32)
                   + jnp.dot(s.astype(bf16), va, preferred_element_type=f32))
            den = num[:, HEAD_DIM:HEAD_DIM + 1] if half == 0 else num[:, 0:1]
            outs.append(num / jnp.maximum(jnp.abs(den), jnp.exp(-m_t)))
            g_c = bl - bc + lic
            g_r = bl - br + lir
            m_new = jnp.maximum(bl + m_prev, jnp.max(g_r, axis=1, keepdims=True))
            decay = jnp.exp(bl + m_prev - m_new)
            kw = (kz.astype(f32) * jnp.exp(g_c - m_new)).astype(bf16)
            c_ref[h] = decay * cst + lax.dot_general(kw, va, tn, preferred_element_type=f32)
            m_ref[h:h + 1, :] = jnp.broadcast_to(m_new, (1, LANES))
        out_ref[:, sl] = jnp.where(halves[0], outs[0], outs[1])


def _mlstm(qk, v, gcol, grow, batch, seq_len):
    L = MLSTM_CHUNK
    nc = seq_len // L
    qk3 = qk.reshape(batch, seq_len, 2 * M_WIDTH)
    v3 = v.reshape(batch, seq_len, M_WIDTH)
    cidx = lambda d, c: jnp.where(d == 0, c, nc - 1 - c)
    return pl.pallas_call(
        functools.partial(_mlstm_kernel, L=L),
        grid=(batch, 2, nc),
        in_specs=[
            pl.BlockSpec((None, L, M_WIDTH), lambda b, d, c: (b, cidx(d, c), 0)),
            pl.BlockSpec((None, L, M_WIDTH), lambda b, d, c: (b, cidx(d, c), 1)),
            pl.BlockSpec((None, L, M_WIDTH), lambda b, d, c: (b, cidx(d, c), 0)),
            pl.BlockSpec((None, None, L, 2 * M_HEADS), lambda b, d, c: (b, d, cidx(d, c), 0)),
            pl.BlockSpec((None, None, 2 * M_HEADS, L), lambda b, d, c: (b, d, 0, cidx(d, c))),
        ],
        out_specs=pl.BlockSpec((None, None, L, M_WIDTH), lambda b, d, c: (b, d, cidx(d, c), 0)),
        out_shape=jax.ShapeDtypeStruct((batch, 2, seq_len, M_WIDTH), f32),
        scratch_shapes=[pltpu.VMEM((M_HEADS, LANES, LANES), f32), pltpu.VMEM((M_HEADS, LANES), f32)],
        compiler_params=_cparams(("parallel", "parallel", "arbitrary")),
        name="mlstm",
    )(qk3, qk3, v3, gcol, grow)


def _natten_kernel(q_ref, k_ref, v_ref, bias_ref, gq_ref, gk_ref, out_ref, kn_ref, *, seq_len, rb_rows):
    rb = pl.program_id(2)
    rows = seq_len // GRID_W
    norm_rows = 512

    @pl.when(rb == 0)
    def _():
        def body(i, carry):
            sl = pl.ds(pl.multiple_of(i * norm_rows, norm_rows), norm_rows)
            kv = k_ref[sl, :]
            kn_ref[sl, :] = (kv * lax.rsqrt(_head_mean_sq(kv, LANES) + EPS) * gk_ref[...]).astype(bf16)
            return carry
        lax.fori_loop(0, seq_len // norm_rows, body, 0)

    qv = q_ref[...]
    qn = (qv * lax.rsqrt(_head_mean_sq(qv, LANES) + EPS) * gq_ref[...] * (HEAD_DIM ** -0.5)).astype(bf16)
    lane = lax.broadcasted_iota(i32, (1, LANES), 1)
    halves = (lane < HEAD_DIM, lane >= HEAD_DIM)
    nt = (((1,), (1,)), ((), ()))
    for j in range(rb_rows):
        r = rb * rb_rows + j
        r0 = jnp.clip(r - WIN_H // 2, 0, rows - WIN_H)
        delta = r - r0
        ks = pl.ds(pl.multiple_of(r0 * GRID_W, GRID_W), WIN_H * GRID_W)
        kwin = kn_ref[ks, :]
        vwin = v_ref[ks, :]
        qj = qn[j * GRID_W:(j + 1) * GRID_W, :]
        outs = []
        for hh in range(2):
            qz = jnp.where(halves[hh], qj, jnp.zeros_like(qj))
            s = lax.dot_general(qz, kwin, nt, preferred_element_type=f32) + bias_ref[hh, delta]
            m = jnp.max(s, axis=-1, keepdims=True)
            p = jnp.exp(s - m)
            l = jnp.sum(p, axis=-1, keepdims=True)
            outs.append(jnp.dot(p.astype(bf16), vwin, preferred_element_type=f32) / l)
        out_ref[j * GRID_W:(j + 1) * GRID_W, :] = jnp.where(halves[0], outs[0], outs[1])


def _natten_bias_table(rel_bias):
    delta = jnp.arange(WIN_H)[:, None]
    j = jnp.arange(WIN_H)[None, :]
    idx_r = j - delta + (WIN_H - 1)
    cq = jnp.arange(GRID_W)[:, None]
    ck = jnp.arange(GRID_W)[None, :]
    c0 = jnp.clip(cq - WIN_W // 2, 0, GRID_W - WIN_W)
    col_in = (ck >= c0) & (ck < c0 + WIN_W)
    idx_c = jnp.clip(ck - cq, -(WIN_W - 1), WIN_W - 1) + (WIN_W - 1)
    tab = rel_bias.astype(f32)[:, idx_r[:, None, :, None], idx_c[None, :, None, :]]
    tab = jnp.where(col_in[None, None, :, None, :], tab, NEG_INF)
    return tab.reshape(A_HEADS // 2, 2, WIN_H, GRID_W, WIN_H * GRID_W)


def _natten(qa, ka, va, bias_tab, g_q2, g_k2, batch, seq_len):
    rows = seq_len // GRID_W
    rbr = NAT_ROWS
    tq = rbr * GRID_W
    q3 = qa.reshape(batch, seq_len, A_WIDTH)
    k3 = ka.reshape(batch, seq_len, A_WIDTH)
    v3 = va.reshape(batch, seq_len, A_WIDTH)
    return pl.pallas_call(
        functools.partial(_natten_kernel, seq_len=seq_len, rb_rows=rbr),
        grid=(batch, A_HEADS // 2, rows // rbr),
        in_specs=[
            pl.BlockSpec((None, tq, LANES), lambda b, hp, rb: (b, rb, hp)),
            pl.BlockSpec((None, seq_len, LANES), lambda b, hp, rb: (b, 0, hp)),
            pl.BlockSpec((None, seq_len, LANES), lambda b, hp, rb: (b, 0, hp)),
            pl.BlockSpec((None, 2, WIN_H, GRID_W, WIN_H * GRID_W), lambda b, hp, rb: (hp, 0, 0, 0, 0)),
            pl.BlockSpec((1, LANES), lambda b, hp, rb: (0, 0)),
            pl.BlockSpec((1, LANES), lambda b, hp, rb: (0, 0)),
        ],
        out_specs=pl.BlockSpec((None, tq, LANES), lambda b, hp, rb: (b, rb, hp)),
        out_shape=jax.ShapeDtypeStruct((batch, seq_len, A_WIDTH), f32),
        scratch_shapes=[pltpu.VMEM((seq_len, LANES), bf16)],
        compiler_params=_cparams(("parallel", "parallel", "arbitrary")),
        name="natten",
    )(q3, k3, v3, bias_tab, g_q2, g_k2)


def _outproj_kernel(x_ref, hf_ref, hb_ref, o_ref, ha_ref, gm_ref, ga_ref, wo_ref, gf_ref,
                    wr1_ref, wr2_ref, br_ref,
                    xmid_ref, hn_ref, ids_ref, pos_ref, gate_ref, cnt_ref, carry_ref, *, tm):
    i = pl.program_id(0)

    @pl.when(i == 0)
    def _():
        carry_ref[...] = jnp.zeros_like(carry_ref)

    hm = hf_ref[...] + hb_ref[...]
    hm = hm * lax.rsqrt(_head_mean_sq(hm, M_WIDTH) + EPS) * gm_ref[...] * jax.nn.sigmoid(o_ref[...])
    ha = _rms(ha_ref[...], ga_ref[...])
    mix = (jnp.dot(hm.astype(bf16), wo_ref[0:M_WIDTH, :], preferred_element_type=f32)
           + jnp.dot(ha.astype(bf16), wo_ref[M_WIDTH:M_WIDTH + A_WIDTH, :], preferred_element_type=f32))
    xm = x_ref[...] + mix
    xmid_ref[...] = xm
    hn = _rms(xm, gf_ref[...])
    hn_ref[...] = hn

    h1, h2 = _split2(hn)
    logits = (jnp.dot(h1, wr1_ref[...], preferred_element_type=f32)
              + (jnp.dot(h1, wr2_ref[...], preferred_element_type=f32)
                 + jnp.dot(h2, wr1_ref[...], preferred_element_type=f32))) + br_ref[...]
    lane = lax.broadcasted_iota(i32, (tm, LANES), 1)
    work = logits
    vals, idxs, sels = [], [], []
    for _ in range(TOP_K):
        mx = jnp.max(work, axis=-1, keepdims=True)
        idx = jnp.min(jnp.where(work == mx, lane, LANES), axis=-1, keepdims=True)
        sel = lane == idx
        vals.append(mx)
        idxs.append(idx)
        sels.append(sel)
        work = jnp.where(sel, NEG_INF, work)
    es = [jnp.exp(v - vals[0]) for v in vals]
    tot = es[0] + es[1] + es[2] + es[3]

    onehot = jnp.where(sels[0] | sels[1] | sels[2] | sels[3], 1.0, 0.0)
    tri = (lax.broadcasted_iota(i32, (tm, tm), 0) > lax.broadcasted_iota(i32, (tm, tm), 1)).astype(bf16)
    base = jnp.dot(tri, onehot.astype(bf16), preferred_element_type=f32) + carry_ref[...]
    ids_out = jnp.zeros((tm, LANES), i32)
    pos_out = jnp.zeros((tm, LANES), i32)
    gate_out = jnp.zeros((tm, LANES), f32)
    for k in range(TOP_K):
        pk = jnp.sum(jnp.where(sels[k], base, 0.0), axis=-1, keepdims=True).astype(i32)
        ids_out = jnp.where(lane == k, idxs[k], ids_out)
        pos_out = jnp.where(lane == k, pk, pos_out)
        gate_out = jnp.where(lane == k, es[k] / tot, gate_out)
    ids_ref[...] = ids_out[:, :TOP_K]
    pos_ref[...] = pos_out[:, :TOP_K]
    gate_ref[...] = gate_out[:, :TOP_K]
    carry_ref[...] += jnp.sum(onehot, axis=0, keepdims=True)
    cnt_ref[...] = carry_ref[...]


def _outproj(x2, hfb, o_m, ha, g_m, g_a, w_out, g_ffn, wr1, wr2, b_r, batch, seq_len):
    n = x2.shape[0]
    tm = TM_OUT
    tpb = seq_len // tm
    row = lambda i: (i, 0)
    const = lambda i: (0, 0)
    ha2 = ha.reshape(n, A_WIDTH)
    return pl.pallas_call(
        functools.partial(_outproj_kernel, tm=tm),
        grid=(n // tm,),
        in_specs=[
            pl.BlockSpec((tm, D_MODEL), row),
            pl.BlockSpec((None, None, tm, M_WIDTH), lambda i: (i // tpb, 0, i % tpb, 0)),
            pl.BlockSpec((None, None, tm, M_WIDTH), lambda i: (i // tpb, 1, i % tpb, 0)),
            pl.BlockSpec((tm, M_WIDTH), row),
            pl.BlockSpec((tm, A_WIDTH), row),
            pl.BlockSpec((1, M_WIDTH), const),
            pl.BlockSpec((1, A_WIDTH), const),
            pl.BlockSpec((M_WIDTH + A_WIDTH, D_MODEL), const),
            pl.BlockSpec((1, D_MODEL), const),
            pl.BlockSpec((D_MODEL, LANES), const),
            pl.BlockSpec((D_MODEL, LANES), const),
            pl.BlockSpec((1, LANES), const),
        ],
        out_specs=[
            pl.BlockSpec((tm, D_MODEL), row),
            pl.BlockSpec((tm, D_MODEL), row),
            pl.BlockSpec((tm, TOP_K), row),
            pl.BlockSpec((tm, TOP_K), row),
            pl.BlockSpec((tm, TOP_K), row),
            pl.BlockSpec((1, LANES), const),
        ],
        out_shape=(
            jax.ShapeDtypeStruct((n, D_MODEL), f32),
            jax.ShapeDtypeStruct((n, D_MODEL), f32),
            jax.ShapeDtypeStruct((n, TOP_K), i32),
            jax.ShapeDtypeStruct((n, TOP_K), i32),
            jax.ShapeDtypeStruct((n, TOP_K), f32),
            jax.ShapeDtypeStruct((1, LANES), f32),
        ),
        scratch_shapes=[pltpu.VMEM((1, LANES), f32)],
        compiler_params=_cparams(("arbitrary",)),
        name="outproj_router",
    )(x2, hfb, hfb, o_m, ha2, g_m, g_a, w_out, g_ffn, wr1, wr2, b_r)


def _row_copy(src_hbm, dst_hbm, src_row, dst_row, sem):
    return pltpu.make_async_copy(src_hbm.at[pl.ds(src_row, 1)], dst_hbm.at[pl.ds(dst_row, 1)], sem)


def _dispatch_kernel(pend_ref, padded_ref, slot_ref, hn_hbm, xb_hbm, zero_ref, sem, *, chunk, tm_e, n_blocks):
    i = pl.program_id(0)

    def zero_copy(e):
        return pltpu.make_async_copy(zero_ref, xb_hbm.at[pl.ds(pl.multiple_of(pend_ref[e] - tm_e, tm_e), tm_e)], sem)

    def tail_copy(b):
        return pltpu.make_async_copy(zero_ref, xb_hbm.at[pl.ds(pl.multiple_of(b * tm_e, tm_e), tm_e)], sem)

    @pl.when(i == 0)
    def _():
        zero_ref[...] = jnp.zeros_like(zero_ref)
        for e in range(N_EXPERTS):
            @pl.when(padded_ref[e] > 0)
            def _():
                zero_copy(e).start()
        used = pend_ref[N_EXPERTS - 1] // tm_e

        def tail_start(b, carry):
            tail_copy(b).start()
            return carry

        def tail_wait(b, carry):
            tail_copy(b).wait()
            return carry

        lax.fori_loop(used, n_blocks, tail_start, 0)
        for e in range(N_EXPERTS):
            @pl.when(padded_ref[e] > 0)
            def _():
                zero_copy(e).wait()
        lax.fori_loop(used, n_blocks, tail_wait, 0)

    def issue(j, carry):
        tok = lax.shift_right_logical(i * chunk + j, 2)
        _row_copy(hn_hbm, xb_hbm, tok, slot_ref[0, 0, j], sem).start()
        return carry

    lax.fori_loop(0, chunk, issue, 0, unroll=8)

    def drain(j, carry):
        _row_copy(hn_hbm, xb_hbm, 0, 0, sem).wait()
        return carry

    lax.fori_loop(0, chunk, drain, 0, unroll=8)


def _dispatch(hn, slot_flat, pend, padded, cap):
    nk = slot_flat.shape[0]
    chunk = DISPATCH_CHUNK
    slot3 = slot_flat.reshape(nk // chunk, 1, chunk)
    grid_spec = pltpu.PrefetchScalarGridSpec(
        num_scalar_prefetch=2,
        grid=(nk // chunk,),
        in_specs=[
            pl.BlockSpec((1, 1, chunk), lambda i, pe, pa: (i, 0, 0), memory_space=pltpu.SMEM),
            pl.BlockSpec(memory_space=pl.ANY),
        ],
        out_specs=pl.BlockSpec(memory_space=pl.ANY),
        scratch_shapes=[pltpu.VMEM((TM_EXP, D_MODEL), f32), pltpu.SemaphoreType.DMA(())],
    )
    return pl.pallas_call(
        functools.partial(_dispatch_kernel, chunk=chunk, tm_e=TM_EXP, n_blocks=cap // TM_EXP),
        grid_spec=grid_spec,
        out_shape=jax.ShapeDtypeStruct((cap, D_MODEL), f32),
        compiler_params=_cparams(("arbitrary",)),
        name="dispatch",
    )(pend, padded, slot3, hn)


def _experts_kernel(be_ref, nv_ref, xb_ref, wgu_ref, bgu_ref, wd_ref, bd_ref, yb_ref):
    j = pl.program_id(0)

    @pl.when(j < nv_ref[0])
    def _():
        xv = xb_ref[...].astype(bf16)
        h = jnp.dot(xv, wgu_ref[...], preferred_element_type=f32) + bgu_ref[...]
        gt = jnp.minimum(h[:, :D_FF], SWIGLU_LIMIT)
        up = jnp.clip(h[:, D_FF:], -SWIGLU_LIMIT, SWIGLU_LIMIT)
        act = (up + 1.0) * (gt * jax.nn.sigmoid(SWIGLU_ALPHA * gt))
        yb_ref[...] = jnp.dot(act.astype(bf16), wd_ref[...], preferred_element_type=f32) + bd_ref[...]

    @pl.when(j >= nv_ref[0])
    def _():
        yb_ref[...] = jnp.zeros_like(yb_ref)


def _experts(xb, block_e, nvalid, w_gu, b_gu, w_d, b_d):
    cap = xb.shape[0]
    tm = TM_EXP
    grid_spec = pltpu.PrefetchScalarGridSpec(
        num_scalar_prefetch=2,
        grid=(cap // tm,),
        in_specs=[
            pl.BlockSpec((tm, D_MODEL), lambda j, be, nv: (jnp.minimum(j, nv[0] - 1), 0)),
            pl.BlockSpec((None, D_MODEL, 2 * D_FF), lambda j, be, nv: (be[j], 0, 0)),
            pl.BlockSpec((None, 1, 2 * D_FF), lambda j, be, nv: (be[j], 0, 0)),
            pl.BlockSpec((None, D_FF, D_MODEL), lambda j, be, nv: (be[j], 0, 0)),
            pl.BlockSpec((None, 1, D_MODEL), lambda j, be, nv: (be[j], 0, 0)),
        ],
        out_specs=pl.BlockSpec((tm, D_MODEL), lambda j, be, nv: (j, 0)),
    )
    return pl.pallas_call(
        _experts_kernel,
        grid_spec=grid_spec,
        out_shape=jax.ShapeDtypeStruct((cap, D_MODEL), f32),
        compiler_params=_cparams(("arbitrary",)),
        name="experts",
    )(block_e, nvalid, xb, w_gu, b_gu, w_d, b_d)


def _combine_kernel(slot_ref, xmid_ref, gate_ref, p_ref, wproj_ref, gpost_ref, gple_ref, wgate_ref, yb_hbm,
                    out_ref, ybuf_ref, sem, *, tm):
    def row(j, k):
        return pltpu.make_async_copy(yb_hbm.at[pl.ds(slot_ref[0, 0, j * TOP_K + k], 1)],
                                     ybuf_ref.at[k, pl.ds(j, 1)], sem)

    def issue(j, carry):
        for k in range(TOP_K):
            row(j, k).start()
        return carry

    lax.fori_loop(0, tm, issue, 0, unroll=4)

    pe = _rms(jnp.dot(p_ref[...].astype(bf16), wproj_ref[...], preferred_element_type=f32), gpost_ref[...])

    def drain(j, carry):
        for k in range(TOP_K):
            row(0, k).wait()
        return carry

    lax.fori_loop(0, tm, drain, 0, unroll=4)

    gate = gate_ref[...]
    y = gate[:, 0:1] * ybuf_ref[0]
    for k in range(1, TOP_K):
        y = y + gate[:, k:k + 1] * ybuf_ref[k]
    x2 = xmid_ref[...] + y
    gl = jnp.dot(_rms(x2, gple_ref[...]).astype(bf16), wgate_ref[...], preferred_element_type=f32)
    out_ref[...] = x2 + jax.nn.sigmoid(gl) * pe


def _combine(slot_flat, xmid, gate, p2, w_proj, g_post, g_ple, w_gate, yb):
    n = xmid.shape[0]
    tm = TM_CMB
    slot3 = slot_flat.reshape(n // tm, 1, tm * TOP_K)
    row = lambda i: (i, 0)
    const = lambda i: (0, 0)
    return pl.pallas_call(
        functools.partial(_combine_kernel, tm=tm),
        grid=(n // tm,),
        in_specs=[
            pl.BlockSpec((1, 1, tm * TOP_K), lambda i: (i, 0, 0), memory_space=pltpu.SMEM),
            pl.BlockSpec((tm, D_MODEL), row),
            pl.BlockSpec((tm, TOP_K), row),
            pl.BlockSpec((tm, PLE_DIM), row),
            pl.BlockSpec((PLE_DIM, D_MODEL), const),
            pl.BlockSpec((1, D_MODEL), const),
            pl.BlockSpec((1, D_MODEL), const),
            pl.BlockSpec((D_MODEL, D_MODEL), const),
            pl.BlockSpec(memory_space=pl.ANY),
        ],
        out_specs=pl.BlockSpec((tm, D_MODEL), row),
        out_shape=jax.ShapeDtypeStruct((n, D_MODEL), f32),
        scratch_shapes=[pltpu.VMEM((TOP_K, tm, D_MODEL), f32), pltpu.SemaphoreType.DMA(())],
        compiler_params=_cparams(("arbitrary",)),
        name="combine_ple",
    )(slot3, xmid, gate, p2, w_proj, g_post, g_ple, w_gate, yb)


def _prep_weights(g_mix, w_in, b_gates, conv_w, g_m_head, g_q, g_k, rel_bias, g_a_out, w_out, g_ffn,
                  w_router, b_router, w_gate_up, b_gate_up, w_down, b_down, g_ple, w_ple_gate, w_ple_proj,
                  g_ple_post):
    g0 = 4 * M_WIDTH
    w = w_in[0]
    pw = {}
    pw["g_mix"] = g_mix[0][None, :]
    pw["w_main"] = jnp.concatenate([w[:, :g0], w[:, g0 + N_GATES:]], axis=1).astype(bf16)
    pw["w_gate"] = jnp.pad(w[:, g0:g0 + N_GATES], ((0, 0), (0, LANES - N_GATES))).astype(bf16)
    pw["b_gate"] = jnp.pad(b_gates[0], (0, LANES - N_GATES))[None, :]
    pw["conv_w"] = conv_w[0]
    pw["g_m"] = g_m_head[0].reshape(1, M_WIDTH)
    pw["g_q2"] = jnp.tile(g_q[0], 2)[None, :]
    pw["g_k2"] = jnp.tile(g_k[0], 2)[None, :]
    pw["bias_tab"] = _natten_bias_table(rel_bias[0])
    pw["g_a"] = g_a_out[0][None, :]
    pw["w_out"] = w_out[0].astype(bf16)
    pw["g_ffn"] = g_ffn[0][None, :]
    wr = jnp.pad(w_router[0], ((0, 0), (0, LANES - N_EXPERTS)))
    wr1 = wr.astype(bf16)
    pw["wr1"] = wr1
    pw["wr2"] = (wr - wr1.astype(f32)).astype(bf16)
    pw["b_r"] = jnp.pad(b_router[0], (0, LANES - N_EXPERTS), constant_values=NEG_INF)[None, :]
    wgu = w_gate_up[0]
    pw["w_gu"] = jnp.concatenate([wgu[:, :, 0::2], wgu[:, :, 1::2]], axis=-1).astype(bf16)
    bgu = b_gate_up[0]
    pw["b_gu"] = jnp.concatenate([bgu[:, 0::2], bgu[:, 1::2]], axis=-1)[:, None, :]
    pw["w_d"] = w_down[0].astype(bf16)
    pw["b_d"] = b_down[0][:, None, :]
    pw["g_ple"] = g_ple[0][None, :]
    pw["w_ple_gate"] = w_ple_gate[0].astype(bf16)
    pw["w_ple_proj"] = w_ple_proj[0].astype(bf16)
    pw["g_ple_post"] = g_ple_post[0][None, :]
    return pw


def _trunk(x, p, pw):
    batch, seq_len, _ = x.shape
    n = batch * seq_len
    x2 = x.reshape(n, D_MODEL)
    qk, v_m, o_m, gates, qa, ka, va = _inproj(x2, pw["g_mix"], pw["w_main"], pw["w_gate"], pw["b_gate"],
                                              pw["conv_w"], seq_len)
    g4 = gates[:, :N_GATES].reshape(batch, seq_len, 2, 2 * M_HEADS)
    gcol = g4.transpose(0, 2, 1, 3)
    grow = g4.transpose(0, 2, 3, 1)
    hfb = _mlstm(qk, v_m, gcol, grow, batch, seq_len)
    ha = _natten(qa, ka, va, pw["bias_tab"], pw["g_q2"], pw["g_k2"], batch, seq_len)
    xmid, hn, ids, pos, gate, cnt = _outproj(x2, hfb, o_m, ha, pw["g_m"], pw["g_a"], pw["w_out"], pw["g_ffn"],
                                             pw["wr1"], pw["wr2"], pw["b_r"], batch, seq_len)

    tm_e = TM_EXP
    counts = cnt[0, :N_EXPERTS].astype(i32)
    padded = (counts + tm_e - 1) // tm_e * tm_e
    pend = jnp.cumsum(padded).astype(i32)
    pstart = pend - padded
    slot = (pstart[ids] + pos).reshape(-1)
    nk = n * TOP_K
    n_blocks = (nk + N_EXPERTS * (tm_e - 1) + tm_e - 1) // tm_e
    cap = n_blocks * tm_e
    block_e = jnp.minimum(jnp.searchsorted(pend, jnp.arange(n_blocks, dtype=i32) * tm_e, side="right"),
                          N_EXPERTS - 1).astype(i32)
    nvalid = (pend[-1:] // tm_e).astype(i32)

    xb = _dispatch(hn, slot, pend, padded, cap)
    yb = _experts(xb, block_e, nvalid, pw["w_gu"], pw["b_gu"], pw["w_d"], pw["b_d"])
    out = _combine(slot, xmid, gate, p.reshape(n, PLE_DIM), pw["w_ple_proj"], pw["g_ple_post"], pw["g_ple"],
                   pw["w_ple_gate"], yb)
    return out.reshape(batch, seq_len, D_MODEL)


def kernel(x_prompt, x_sample, p_prompt, p_sample, g_mix, w_in, b_gates, conv_w, g_m_head, g_q, g_k, rel_bias,
           g_a_out, w_out, g_ffn, w_router, b_router, w_gate_up, b_gate_up, w_down, b_down, g_ple, w_ple_gate,
           w_ple_proj, g_ple_post):
    assert w_in.shape[0] == 1, "single-layer trunk"
    pw = _prep_weights(g_mix, w_in, b_gates, conv_w, g_m_head, g_q, g_k, rel_bias, g_a_out, w_out, g_ffn,
                       w_router, b_router, w_gate_up, b_gate_up, w_down, b_down, g_ple, w_ple_gate, w_ple_proj,
                       g_ple_post)
    y_prompt = _trunk(x_prompt, p_prompt[0], pw)
    y_sample = _trunk(x_sample, p_sample[0], pw)
    return (y_prompt, y_sample)
```

```python
import functools

import jax
import jax.numpy as jnp
from jax import lax
from jax.experimental import pallas as pl
from jax.experimental.pallas import tpu as pltpu

f32 = jnp.float32
bf16 = jnp.bfloat16
i32 = jnp.int32

D_MODEL = 1024
HEAD_DIM = 64
M_HEADS = 8
A_HEADS = 8
M_WIDTH = M_HEADS * HEAD_DIM
A_WIDTH = A_HEADS * HEAD_DIM
N_GATES = 4 * M_HEADS
GRID_W = 64
WIN_H = 8
WIN_W = 16
N_EXPERTS = 32
TOP_K = 4
D_FF = 1024
SWIGLU_LIMIT = 7.0
SWIGLU_ALPHA = 1.702
PLE_DIM = 256
EPS = 1e-6

LANES = 128
SUBLANES = 8
VMEM_LIMIT = 56 * 1024 * 1024

TM_IN = 512
MLSTM_CHUNK = 256
NAT_ROWS = 8
TM_OUT = 512
TM_EXP = 256
TM_DISPATCH = 256
TR_WPREP = 512
TM_CMB = 256

NEG_INF = float("-inf")


def _cparams(sem):
    return pltpu.CompilerParams(dimension_semantics=sem, vmem_limit_bytes=VMEM_LIMIT)


def _rms(xv, g):
    return xv * lax.rsqrt(jnp.mean(xv * xv, axis=-1, keepdims=True) + EPS) * g


def _split2(a):
    hi = a.astype(bf16)
    lo = (a - hi.astype(f32)).astype(bf16)
    return hi, lo


def _split3(a):
    hi = a.astype(bf16)
    r = a - hi.astype(f32)
    mid = r.astype(bf16)
    lo = (r - mid.astype(f32)).astype(bf16)
    return hi, mid, lo


def _head_mean_sq(xv, width):
    a = lax.broadcasted_iota(i32, (width, width), 0) // HEAD_DIM
    b = lax.broadcasted_iota(i32, (width, width), 1) // HEAD_DIM
    bd = jnp.where(a == b, 1.0 / HEAD_DIM, 0.0).astype(bf16)
    hi, lo = _split2(xv * xv)
    return (jnp.dot(hi, bd, preferred_element_type=f32) + jnp.dot(lo, bd, preferred_element_type=f32))


def _log_sigmoid(x):
    return jnp.minimum(x, 0.0) - jnp.log1p(jnp.exp(-jnp.abs(x)))


def _inproj_kernel(x_ref, xp_ref, xn_ref, g_ref, wm_ref, wg_ref, bg_ref, cw_ref,
                   qk_ref, v_ref, o_ref, gates_ref, qa_ref, ka_ref, va_ref, *, tm, seq_len):
    i = pl.program_id(0)
    g = g_ref[...]
    h = _rms(x_ref[...], g).astype(bf16)
    z = jnp.dot(h, wm_ref[...], preferred_element_type=f32)
    gates_ref[...] = jnp.dot(h, wg_ref[...], preferred_element_type=f32) + bg_ref[...]
    hh = _rms(jnp.concatenate([xp_ref[...], xn_ref[...]], axis=0), g).astype(bf16)
    zh = jnp.dot(hh, wm_ref[:, :2 * M_WIDTH], preferred_element_type=f32)
    start = lax.rem(i * tm, seq_len)
    prev_row = jnp.where(start == 0, 0.0, zh[SUBLANES - 1:SUBLANES, :])
    next_row = jnp.where(start + tm == seq_len, 0.0, zh[SUBLANES:SUBLANES + 1, :])
    u = z[:, :2 * M_WIDTH]
    rid = lax.broadcasted_iota(i32, (tm, 1), 0)
    u_prev = jnp.where(rid == 0, prev_row, pltpu.roll(u, 1, 0))
    u_next = jnp.where(rid == tm - 1, next_row, pltpu.roll(u, tm - 1, 0))
    cw = cw_ref[...]
    c = u_prev * cw[0:1, :] + u * cw[1:2, :] + u_next * cw[2:3, :]
    qk_ref[...] = (c * jax.nn.sigmoid(c)).astype(bf16)
    v_ref[...] = z[:, 2 * M_WIDTH:3 * M_WIDTH].astype(bf16)
    o_ref[...] = z[:, 3 * M_WIDTH:4 * M_WIDTH]
    base = 4 * M_WIDTH
    qa_ref[...] = z[:, base:base + A_WIDTH]
    ka_ref[...] = z[:, base + A_WIDTH:base + 2 * A_WIDTH]
    va_ref[...] = z[:, base + 2 * A_WIDTH:base + 3 * A_WIDTH].astype(bf16)


def _inproj(x2, g_mix, w_main, w_gate, b_gate, conv_w, seq_len):
    n = x2.shape[0]
    tm = TM_IN
    nb8 = n // SUBLANES
    r8 = tm // SUBLANES
    wcols = w_main.shape[1]
    row = lambda i: (i, 0)
    const = lambda i: (0, 0)
    out_shapes = (
        jax.ShapeDtypeStruct((n, 2 * M_WIDTH), bf16),
        jax.ShapeDtypeStruct((n, M_WIDTH), bf16),
        jax.ShapeDtypeStruct((n, M_WIDTH), f32),
        jax.ShapeDtypeStruct((n, LANES), f32),
        jax.ShapeDtypeStruct((n, A_WIDTH), f32),
        jax.ShapeDtypeStruct((n, A_WIDTH), f32),
        jax.ShapeDtypeStruct((n, A_WIDTH), bf16),
    )
    return pl.pallas_call(
        functools.partial(_inproj_kernel, tm=tm, seq_len=seq_len),
        grid=(n // tm,),
        in_specs=[
            pl.BlockSpec((tm, D_MODEL), row),
            pl.BlockSpec((SUBLANES, D_MODEL), lambda i: (jnp.maximum(i * r8 - 1, 0), 0)),
            pl.BlockSpec((SUBLANES, D_MODEL), lambda i: (jnp.minimum((i + 1) * r8, nb8 - 1), 0)),
            pl.BlockSpec((1, D_MODEL), const),
            pl.BlockSpec((D_MODEL, wcols), const),
            pl.BlockSpec((D_MODEL, LANES), const),
            pl.BlockSpec((1, LANES), const),
            pl.BlockSpec((3, 2 * M_WIDTH), const),
        ],
        out_specs=[
            pl.BlockSpec((tm, 2 * M_WIDTH), row),
            pl.BlockSpec((tm, M_WIDTH), row),
            pl.BlockSpec((tm, M_WIDTH), row),
            pl.BlockSpec((tm, LANES), row),
            pl.BlockSpec((tm, A_WIDTH), row),
            pl.BlockSpec((tm, A_WIDTH), row),
            pl.BlockSpec((tm, A_WIDTH), row),
        ],
        out_shape=out_shapes,
        compiler_params=_cparams(("parallel",)),
        name="inproj",
    )(x2, x2, x2, g_mix, w_main, w_gate, b_gate, conv_w)


def _mlstm_kernel(q_ref, k_ref, v_ref, gc_ref, gr_ref, out_ref, c_ref, m_ref, *, L):
    d = pl.program_id(1)
    c = pl.program_id(2)

    @pl.when(c == 0)
    def _():
        c_ref[...] = jnp.zeros_like(c_ref)
        m_ref[...] = jnp.zeros_like(m_ref)

    sgn = jnp.where(d == 0, 1, -1)
    row = lax.broadcasted_iota(i32, (L, L), 0)
    col = lax.broadcasted_iota(i32, (L, L), 1)
    mask = (row - col) * sgn >= 0
    t_col = mask.astype(bf16)
    t_row = ((col - row) * sgn >= 0).astype(bf16)

    gc = gc_ref[...]
    gr = gr_ref[...]
    li_c = gc[:, 0:M_HEADS]
    lf_c = _log_sigmoid(gc[:, M_HEADS:2 * M_HEADS])
    li_r = gr[0:M_HEADS, :]
    lf_r = _log_sigmoid(gr[M_HEADS:2 * M_HEADS, :])
    b_c = sum(jnp.dot(t_col, part, preferred_element_type=f32) for part in _split3(lf_c))
    b_r = sum(jnp.dot(part, t_row, preferred_element_type=f32) for part in _split3(lf_r))
    bl_all = jnp.sum(lf_c, axis=0, keepdims=True)

    lane = lax.broadcasted_iota(i32, (1, LANES), 1)
    halves = (lane < HEAD_DIM, lane >= HEAD_DIM)
    nt = (((1,), (1,)), ((), ()))
    tn = (((0,), (0,)), ((), ()))

    for p in range(M_HEADS // 2):
        sl = slice(p * LANES, (p + 1) * LANES)
        q2 = q_ref[:, sl]
        k2 = k_ref[:, sl]
        v2 = v_ref[:, sl]
        outs = []
        for half in range(2):
            h = 2 * p + half
            inh = halves[half]
            qz = jnp.where(inh, q2 * (HEAD_DIM ** -0.5), jnp.zeros_like(q2))
            kz = jnp.where(inh, k2, jnp.zeros_like(k2))
            va = jnp.where(inh, v2, jnp.ones_like(v2))
            bc = b_c[:, h:h + 1]
            br = b_r[h:h + 1, :]
            lic = li_c[:, h:h + 1]
            lir = li_r[h:h + 1, :]
            bl = bl_all[:, h:h + 1]
            m_prev = m_ref[h:h + 1, 0:1]
            dmat = jnp.where(mask, bc - br + lir, NEG_INF)
            inter = bc + m_prev
            m_t = jnp.maximum(inter, jnp.max(dmat, axis=1, keepdims=True))
            w_int = jnp.exp(inter - m_t)
            s = lax.dot_general(qz, k2, nt, preferred_element_type=f32) * jnp.exp(dmat - m_t)
            cst = c_ref[h]
            num = (w_int * jnp.dot(qz, cst.astype(bf16), preferred_element_type=f32)
                   + jnp.dot(s.astype(bf16), va, preferred_element_type=f32))
            den = num[:, HEAD_DIM:HEAD_DIM + 1] if half == 0 else num[:, 0:1]
            outs.append(num / jnp.maximum(jnp.abs(den), jnp.exp(-m_t)))
            g_c = bl - bc + lic
            g_r = bl - br + lir
            m_new = jnp.maximum(bl + m_prev, jnp.max(g_r, axis=1, keepdims=True))
            decay = jnp.exp(bl + m_prev - m_new)
            kw = (kz.astype(f32) * jnp.exp(g_c - m_new)).astype(bf16)
            c_ref[h] = decay * cst + lax.dot_general(kw, va, tn, preferred_element_type=f32)
            m_ref[h:h + 1, :] = jnp.broadcast_to(m_new, (1, LANES))
        out_ref[:, sl] = jnp.where(halves[0], outs[0], outs[1])


def _mlstm(qk, v, gcol, grow, batch, seq_len):
    L = MLSTM_CHUNK
    nc = seq_len // L
    qk3 = qk.reshape(batch, seq_len, 2 * M_WIDTH)
    v3 = v.reshape(batch, seq_len, M_WIDTH)
    cidx = lambda d, c: jnp.where(d == 0, c, nc - 1 - c)
    return pl.pallas_call(
        functools.partial(_mlstm_kernel, L=L),
        grid=(batch, 2, nc),
        in_specs=[
            pl.BlockSpec((None, L, M_WIDTH), lambda b, d, c: (b, cidx(d, c), 0)),
            pl.BlockSpec((None, L, M_WIDTH), lambda b, d, c: (b, cidx(d, c), 1)),
            pl.BlockSpec((None, L, M_WIDTH), lambda b, d, c: (b, cidx(d, c), 0)),
            pl.BlockSpec((None, None, L, 2 * M_HEADS), lambda b, d, c: (b, d, cidx(d, c), 0)),
            pl.BlockSpec((None, None, 2 * M_HEADS, L), lambda b, d, c: (b, d, 0, cidx(d, c))),
        ],
        out_specs=pl.BlockSpec((None, None, L, M_WIDTH), lambda b, d, c: (b, d, cidx(d, c), 0)),
        out_shape=jax.ShapeDtypeStruct((batch, 2, seq_len, M_WIDTH), f32),
        scratch_shapes=[pltpu.VMEM((M_HEADS, LANES, LANES), f32), pltpu.VMEM((M_HEADS, LANES), f32)],
        compiler_params=_cparams(("parallel", "parallel", "arbitrary")),
        name="mlstm",
    )(qk3, qk3, v3, gcol, grow)


def _natten_kernel(q_ref, k_ref, v_ref, bias_ref, gq_ref, gk_ref, out_ref, kn_ref, *, seq_len, rb_rows):
    rb = pl.program_id(2)
    rows = seq_len // GRID_W
    norm_rows = 512

    @pl.when(rb == 0)
    def _():
        def body(i, carry):
            sl = pl.ds(pl.multiple_of(i * norm_rows, norm_rows), norm_rows)
            kv = k_ref[sl, :]
            kn_ref[sl, :] = (kv * lax.rsqrt(_head_mean_sq(kv, LANES) + EPS) * gk_ref[...]).astype(bf16)
            return carry
        lax.fori_loop(0, seq_len // norm_rows, body, 0)

    qv = q_ref[...]
    qn = (qv * lax.rsqrt(_head_mean_sq(qv, LANES) + EPS) * gq_ref[...] * (HEAD_DIM ** -0.5)).astype(bf16)
    lane = lax.broadcasted_iota(i32, (1, LANES), 1)
    halves = (lane < HEAD_DIM, lane >= HEAD_DIM)
    nt = (((1,), (1,)), ((), ()))
    for j in range(rb_rows):
        r = rb * rb_rows + j
        r0 = jnp.clip(r - WIN_H // 2, 0, rows - WIN_H)
        delta = r - r0
        ks = pl.ds(pl.multiple_of(r0 * GRID_W, GRID_W), WIN_H * GRID_W)
        kwin = kn_ref[ks, :]
        vwin = v_ref[ks, :]
        qj = qn[j * GRID_W:(j + 1) * GRID_W, :]
        outs = []
        for hh in range(2):
            qz = jnp.where(halves[hh], qj, jnp.zeros_like(qj))
            s = lax.dot_general(qz, kwin, nt, preferred_element_type=f32) + bias_ref[hh, delta]
            m = jnp.max(s, axis=-1, keepdims=True)
            p = jnp.exp(s - m)
            l = jnp.sum(p, axis=-1, keepdims=True)
            outs.append(jnp.dot(p.astype(bf16), vwin, preferred_element_type=f32) / l)
        out_ref[j * GRID_W:(j + 1) * GRID_W, :] = jnp.where(halves[0], outs[0], outs[1])


def _natten_bias_table(rel_bias):
    cq = jnp.arange(GRID_W)[:, None]
    ck = jnp.arange(GRID_W)[None, :]
    c0 = jnp.clip(cq - WIN_W // 2, 0, GRID_W - WIN_W)
    col_in = (ck >= c0) & (ck < c0 + WIN_W)
    idx_c = jnp.clip(ck - cq, -(WIN_W - 1), WIN_W - 1) + (WIN_W - 1)
    pick = idx_c[:, :, None] == jnp.arange(2 * WIN_W - 1)
    tz = jnp.sum(jnp.where(pick[None, None], rel_bias.astype(f32)[:, :, None, None, :], 0.0), axis=-1)
    tz = jnp.where(col_in[None, None], tz, NEG_INF)
    tab = jnp.stack([tz[:, WIN_H - 1 - dl:2 * WIN_H - 1 - dl] for dl in range(WIN_H)], axis=1)
    tab = tab.transpose(0, 1, 3, 2, 4)
    return tab.reshape(A_HEADS // 2, 2, WIN_H, GRID_W, WIN_H * GRID_W)


def _natten(qa, ka, va, bias_tab, g_q2, g_k2, batch, seq_len):
    rows = seq_len // GRID_W
    rbr = NAT_ROWS
    tq = rbr * GRID_W
    q3 = qa.reshape(batch, seq_len, A_WIDTH)
    k3 = ka.reshape(batch, seq_len, A_WIDTH)
    v3 = va.reshape(batch, seq_len, A_WIDTH)
    return pl.pallas_call(
        functools.partial(_natten_kernel, seq_len=seq_len, rb_rows=rbr),
        grid=(batch, A_HEADS // 2, rows // rbr),
        in_specs=[
            pl.BlockSpec((None, tq, LANES), lambda b, hp, rb: (b, rb, hp)),
            pl.BlockSpec((None, seq_len, LANES), lambda b, hp, rb: (b, 0, hp)),
            pl.BlockSpec((None, seq_len, LANES), lambda b, hp, rb: (b, 0, hp)),
            pl.BlockSpec((None, 2, WIN_H, GRID_W, WIN_H * GRID_W), lambda b, hp, rb: (hp, 0, 0, 0, 0)),
            pl.BlockSpec((1, LANES), lambda b, hp, rb: (0, 0)),
            pl.BlockSpec((1, LANES), lambda b, hp, rb: (0, 0)),
        ],
        out_specs=pl.BlockSpec((None, tq, LANES), lambda b, hp, rb: (b, rb, hp)),
        out_shape=jax.ShapeDtypeStruct((batch, seq_len, A_WIDTH), f32),
        scratch_shapes=[pltpu.VMEM((seq_len, LANES), bf16)],
        compiler_params=_cparams(("parallel", "parallel", "arbitrary")),
        name="natten",
    )(q3, k3, v3, bias_tab, g_q2, g_k2)


def _outproj_kernel(x_ref, hf_ref, hb_ref, o_ref, ha_ref, gm_ref, ga_ref, wo_ref, gf_ref,
                    wr1_ref, wr2_ref, br_ref,
                    xmid_ref, hn_ref, ids_ref, pos_ref, gate_ref, cnt_ref, carry_ref, *, tm):
    i = pl.program_id(0)

    @pl.when(i == 0)
    def _():
        carry_ref[...] = jnp.zeros_like(carry_ref)

    hm = hf_ref[...] + hb_ref[...]
    hm = hm * lax.rsqrt(_head_mean_sq(hm, M_WIDTH) + EPS) * gm_ref[...] * jax.nn.sigmoid(o_ref[...])
    ha = _rms(ha_ref[...], ga_ref[...])
    mix = (jnp.dot(hm.astype(bf16), wo_ref[0:M_WIDTH, :], preferred_element_type=f32)
           + jnp.dot(ha.astype(bf16), wo_ref[M_WIDTH:M_WIDTH + A_WIDTH, :], preferred_element_type=f32))
    xm = x_ref[...] + mix
    xmid_ref[...] = xm
    hn = _rms(xm, gf_ref[...])
    hn_ref[...] = hn

    h1, h2 = _split2(hn)
    logits = (jnp.dot(h1, wr1_ref[...], preferred_element_type=f32)
              + (jnp.dot(h1, wr2_ref[...], preferred_element_type=f32)
                 + jnp.dot(h2, wr1_ref[...], preferred_element_type=f32))) + br_ref[...]
    lane = lax.broadcasted_iota(i32, (tm, LANES), 1)
    work = logits
    vals, idxs, sels = [], [], []
    for _ in range(TOP_K):
        mx = jnp.max(work, axis=-1, keepdims=True)
        idx = jnp.min(jnp.where(work == mx, lane, LANES), axis=-1, keepdims=True)
        sel = lane == idx
        vals.append(mx)
        idxs.append(idx)
        sels.append(sel)
        work = jnp.where(sel, NEG_INF, work)
    es = [jnp.exp(v - vals[0]) for v in vals]
    tot = es[0] + es[1] + es[2] + es[3]

    onehot = jnp.where(sels[0] | sels[1] | sels[2] | sels[3], 1.0, 0.0)
    tri = (lax.broadcasted_iota(i32, (tm, tm), 0) > lax.broadcasted_iota(i32, (tm, tm), 1)).astype(bf16)
    base = jnp.dot(tri, onehot.astype(bf16), preferred_element_type=f32) + carry_ref[...]
    ids_out = jnp.zeros((tm, LANES), i32)
    pos_out = jnp.zeros((tm, LANES), i32)
    gate_out = jnp.zeros((tm, LANES), f32)
    for k in range(TOP_K):
        pk = jnp.sum(jnp.where(sels[k], base, 0.0), axis=-1, keepdims=True).astype(i32)
        ids_out = jnp.where(lane == k, idxs[k], ids_out)
        pos_out = jnp.where(lane == k, pk, pos_out)
        gate_out = jnp.where(lane == k, es[k] / tot, gate_out)
    ids_ref[...] = ids_out[:, :TOP_K]
    pos_ref[...] = pos_out[:, :TOP_K]
    gate_ref[...] = gate_out[:, :TOP_K]
    carry_ref[...] += jnp.sum(onehot, axis=0, keepdims=True)
    cnt_ref[...] = carry_ref[...]


def _outproj(x2, hfb, o_m, ha, g_m, g_a, w_out, g_ffn, wr1, wr2, b_r, batch, seq_len):
    n = x2.shape[0]
    tm = TM_OUT
    tpb = seq_len // tm
    row = lambda i: (i, 0)
    const = lambda i: (0, 0)
    ha2 = ha.reshape(n, A_WIDTH)
    return pl.pallas_call(
        functools.partial(_outproj_kernel, tm=tm),
        grid=(n // tm,),
        in_specs=[
            pl.BlockSpec((tm, D_MODEL), row),
            pl.BlockSpec((None, None, tm, M_WIDTH), lambda i: (i // tpb, 0, i % tpb, 0)),
            pl.BlockSpec((None, None, tm, M_WIDTH), lambda i: (i // tpb, 1, i % tpb, 0)),
            pl.BlockSpec((tm, M_WIDTH), row),
            pl.BlockSpec((tm, A_WIDTH), row),
            pl.BlockSpec((1, M_WIDTH), const),
            pl.BlockSpec((1, A_WIDTH), const),
            pl.BlockSpec((M_WIDTH + A_WIDTH, D_MODEL), const),
            pl.BlockSpec((1, D_MODEL), const),
            pl.BlockSpec((D_MODEL, LANES), const),
            pl.BlockSpec((D_MODEL, LANES), const),
            pl.BlockSpec((1, LANES), const),
        ],
        out_specs=[
            pl.BlockSpec((tm, D_MODEL), row),
            pl.BlockSpec((tm, D_MODEL), row),
            pl.BlockSpec((tm, TOP_K), row),
            pl.BlockSpec((tm, TOP_K), row),
            pl.BlockSpec((tm, TOP_K), row),
            pl.BlockSpec((1, LANES), const),
        ],
        out_shape=(
            jax.ShapeDtypeStruct((n, D_MODEL), f32),
            jax.ShapeDtypeStruct((n, D_MODEL), f32),
            jax.ShapeDtypeStruct((n, TOP_K), i32),
            jax.ShapeDtypeStruct((n, TOP_K), i32),
            jax.ShapeDtypeStruct((n, TOP_K), f32),
            jax.ShapeDtypeStruct((1, LANES), f32),
        ),
        scratch_shapes=[pltpu.VMEM((1, LANES), f32)],
        compiler_params=_cparams(("arbitrary",)),
        name="outproj_router",
    )(x2, hfb, hfb, o_m, ha2, g_m, g_a, w_out, g_ffn, wr1, wr2, b_r)


def _dispatch_kernel(pend_ref, padded_ref, slot_ref, hn_ref, xb_hbm, zero_ref, sem, *, tm, tm_e, n_blocks):
    i = pl.program_id(0)

    def zero_copy(e):
        return pltpu.make_async_copy(zero_ref, xb_hbm.at[pl.ds(pl.multiple_of(pend_ref[e] - tm_e, tm_e), tm_e)], sem)

    def tail_copy(b):
        return pltpu.make_async_copy(zero_ref, xb_hbm.at[pl.ds(pl.multiple_of(b * tm_e, tm_e), tm_e)], sem)

    @pl.when(i == 0)
    def _():
        zero_ref[...] = jnp.zeros_like(zero_ref)
        for e in range(N_EXPERTS):
            @pl.when(padded_ref[e] > 0)
            def _():
                zero_copy(e).start()
        used = pend_ref[N_EXPERTS - 1] // tm_e

        def tail_start(b, carry):
            tail_copy(b).start()
            return carry

        def tail_wait(b, carry):
            tail_copy(b).wait()
            return carry

        lax.fori_loop(used, n_blocks, tail_start, 0)
        for e in range(N_EXPERTS):
            @pl.when(padded_ref[e] > 0)
            def _():
                zero_copy(e).wait()
        lax.fori_loop(used, n_blocks, tail_wait, 0)

    def row(j, k):
        return pltpu.make_async_copy(hn_ref.at[pl.ds(j, 1)],
                                     xb_hbm.at[pl.ds(slot_ref[0, 0, j * TOP_K + k], 1)], sem)

    def issue(j, carry):
        for k in range(TOP_K):
            row(j, k).start()
        return carry

    lax.fori_loop(0, tm, issue, 0, unroll=4)

    def drain(j, carry):
        for k in range(TOP_K):
            row(0, k).wait()
        return carry

    lax.fori_loop(0, tm, drain, 0, unroll=4)


def _dispatch(hn, slot_flat, pend, padded, cap):
    n = hn.shape[0]
    tm = TM_DISPATCH
    slot3 = slot_flat.reshape(n // tm, 1, tm * TOP_K)
    grid_spec = pltpu.PrefetchScalarGridSpec(
        num_scalar_prefetch=2,
        grid=(n // tm,),
        in_specs=[
            pl.BlockSpec((1, 1, tm * TOP_K), lambda i, pe, pa: (i, 0, 0), memory_space=pltpu.SMEM),
            pl.BlockSpec((tm, D_MODEL), lambda i, pe, pa: (i, 0)),
        ],
        out_specs=pl.BlockSpec(memory_space=pl.ANY),
        scratch_shapes=[pltpu.VMEM((TM_EXP, D_MODEL), f32), pltpu.SemaphoreType.DMA(())],
    )
    return pl.pallas_call(
        functools.partial(_dispatch_kernel, tm=tm, tm_e=TM_EXP, n_blocks=cap // TM_EXP),
        grid_spec=grid_spec,
        out_shape=jax.ShapeDtypeStruct((cap, D_MODEL), f32),
        compiler_params=_cparams(("arbitrary",)),
        name="dispatch",
    )(pend, padded, slot3, hn)


def _experts_kernel(be_ref, nv_ref, xb_ref, wgu_ref, bgu_ref, wd_ref, bd_ref, yb_ref):
    j = pl.program_id(0)

    @pl.when(j < nv_ref[0])
    def _():
        xv = xb_ref[...].astype(bf16)
        h = jnp.dot(xv, wgu_ref[...], preferred_element_type=f32) + bgu_ref[...]
        gt = jnp.minimum(h[:, :D_FF], SWIGLU_LIMIT)
        up = jnp.clip(h[:, D_FF:], -SWIGLU_LIMIT, SWIGLU_LIMIT)
        act = (up + 1.0) * (gt * jax.nn.sigmoid(SWIGLU_ALPHA * gt))
        yb_ref[...] = jnp.dot(act.astype(bf16), wd_ref[...], preferred_element_type=f32) + bd_ref[...]

    @pl.when(j >= nv_ref[0])
    def _():
        yb_ref[...] = jnp.zeros_like(yb_ref)


def _experts(xb, block_e, nvalid, w_gu, b_gu, w_d, b_d):
    cap = xb.shape[0]
    tm = TM_EXP
    grid_spec = pltpu.PrefetchScalarGridSpec(
        num_scalar_prefetch=2,
        grid=(cap // tm,),
        in_specs=[
            pl.BlockSpec((tm, D_MODEL), lambda j, be, nv: (jnp.minimum(j, nv[0] - 1), 0)),
            pl.BlockSpec((None, D_MODEL, 2 * D_FF), lambda j, be, nv: (be[j], 0, 0)),
            pl.BlockSpec((None, 1, 2 * D_FF), lambda j, be, nv: (be[j], 0, 0)),
            pl.BlockSpec((None, D_FF, D_MODEL), lambda j, be, nv: (be[j], 0, 0)),
            pl.BlockSpec((None, 1, D_MODEL), lambda j, be, nv: (be[j], 0, 0)),
        ],
        out_specs=pl.BlockSpec((tm, D_MODEL), lambda j, be, nv: (j, 0)),
    )
    return pl.pallas_call(
        _experts_kernel,
        grid_spec=grid_spec,
        out_shape=jax.ShapeDtypeStruct((cap, D_MODEL), f32),
        compiler_params=_cparams(("arbitrary",)),
        name="experts",
    )(block_e, nvalid, xb, w_gu, b_gu, w_d, b_d)


def _combine_kernel(slot_ref, xmid_ref, gate_ref, p_ref, wproj_ref, gpost_ref, gple_ref, wgate_ref, yb_hbm,
                    out_ref, ybuf_ref, sem, *, tm):
    def row(j, k):
        return pltpu.make_async_copy(yb_hbm.at[pl.ds(slot_ref[0, 0, j * TOP_K + k], 1)],
                                     ybuf_ref.at[k, pl.ds(j, 1)], sem)

    def issue(j, carry):
        for k in range(TOP_K):
            row(j, k).start()
        return carry

    lax.fori_loop(0, tm, issue, 0, unroll=4)

    pe = _rms(jnp.dot(p_ref[...].astype(bf16), wproj_ref[...], preferred_element_type=f32), gpost_ref[...])

    def drain(j, carry):
        for k in range(TOP_K):
            row(0, k).wait()
        return carry

    lax.fori_loop(0, tm, drain, 0, unroll=4)

    gate = gate_ref[...]
    y = gate[:, 0:1] * ybuf_ref[0]
    for k in range(1, TOP_K):
        y = y + gate[:, k:k + 1] * ybuf_ref[k]
    x2 = xmid_ref[...] + y
    gl = jnp.dot(_rms(x2, gple_ref[...]).astype(bf16), wgate_ref[...], preferred_element_type=f32)
    out_ref[...] = x2 + jax.nn.sigmoid(gl) * pe


def _combine(slot_flat, xmid, gate, p2, w_proj, g_post, g_ple, w_gate, yb):
    n = xmid.shape[0]
    tm = TM_CMB
    slot3 = slot_flat.reshape(n // tm, 1, tm * TOP_K)
    row = lambda i: (i, 0)
    const = lambda i: (0, 0)
    return pl.pallas_call(
        functools.partial(_combine_kernel, tm=tm),
        grid=(n // tm,),
        in_specs=[
            pl.BlockSpec((1, 1, tm * TOP_K), lambda i: (i, 0, 0), memory_space=pltpu.SMEM),
            pl.BlockSpec((tm, D_MODEL), row),
            pl.BlockSpec((tm, TOP_K), row),
            pl.BlockSpec((tm, PLE_DIM), row),
            pl.BlockSpec((PLE_DIM, D_MODEL), const),
            pl.BlockSpec((1, D_MODEL), const),
            pl.BlockSpec((1, D_MODEL), const),
            pl.BlockSpec((D_MODEL, D_MODEL), const),
            pl.BlockSpec(memory_space=pl.ANY),
        ],
        out_specs=pl.BlockSpec((tm, D_MODEL), row),
        out_shape=jax.ShapeDtypeStruct((n, D_MODEL), f32),
        scratch_shapes=[pltpu.VMEM((TOP_K, tm, D_MODEL), f32), pltpu.SemaphoreType.DMA(())],
        compiler_params=_cparams(("arbitrary",)),
        name="combine_ple",
    )(slot3, xmid, gate, p2, w_proj, g_post, g_ple, w_gate, yb)


def _wprep_kernel(w_ref, out_ref):
    grp = 2 * LANES
    src = lax.broadcasted_iota(i32, (grp, grp), 0)
    dst = lax.broadcasted_iota(i32, (grp, grp), 1)
    want = jnp.where(dst < LANES, 2 * dst, 2 * (dst - LANES) + 1)
    perm = (src == want).astype(bf16)
    for g in range(2 * D_FF // grp):
        t = jnp.dot(w_ref[:, g * grp:(g + 1) * grp].astype(bf16), perm, preferred_element_type=f32)
        out_ref[:, g * LANES:(g + 1) * LANES] = t[:, :LANES].astype(bf16)
        out_ref[:, D_FF + g * LANES:D_FF + (g + 1) * LANES] = t[:, LANES:].astype(bf16)


def _wprep(w_gate_up):
    ne = w_gate_up.shape[0]
    tr = TR_WPREP
    return pl.pallas_call(
        _wprep_kernel,
        grid=(ne, D_MODEL // tr),
        in_specs=[pl.BlockSpec((None, tr, 2 * D_FF), lambda e, r: (e, r, 0))],
        out_specs=pl.BlockSpec((None, tr, 2 * D_FF), lambda e, r: (e, r, 0)),
        out_shape=jax.ShapeDtypeStruct((ne, D_MODEL, 2 * D_FF), bf16),
        compiler_params=_cparams(("parallel", "parallel")),
        name="wprep",
    )(w_gate_up)


def _prep_weights(g_mix, w_in, b_gates, conv_w, g_m_head, g_q, g_k, rel_bias, g_a_out, w_out, g_ffn,
                  w_router, b_router, w_gate_up, b_gate_up, w_down, b_down, g_ple, w_ple_gate, w_ple_proj,
                  g_ple_post):
    g0 = 4 * M_WIDTH
    w = w_in[0]
    pw = {}
    pw["g_mix"] = g_mix[0][None, :]
    pw["w_main"] = jnp.concatenate([w[:, :g0], w[:, g0 + N_GATES:]], axis=1).astype(bf16)
    pw["w_gate"] = jnp.pad(w[:, g0:g0 + N_GATES], ((0, 0), (0, LANES - N_GATES))).astype(bf16)
    pw["b_gate"] = jnp.pad(b_gates[0], (0, LANES - N_GATES))[None, :]
    pw["conv_w"] = conv_w[0]
    pw["g_m"] = g_m_head[0].reshape(1, M_WIDTH)
    pw["g_q2"] = jnp.tile(g_q[0], 2)[None, :]
    pw["g_k2"] = jnp.tile(g_k[0], 2)[None, :]
    pw["bias_tab"] = _natten_bias_table(rel_bias[0])
    pw["g_a"] = g_a_out[0][None, :]
    pw["w_out"] = w_out[0].astype(bf16)
    pw["g_ffn"] = g_ffn[0][None, :]
    wr = jnp.pad(w_router[0], ((0, 0), (0, LANES - N_EXPERTS)))
    wr1 = wr.astype(bf16)
    pw["wr1"] = wr1
    pw["wr2"] = (wr - wr1.astype(f32)).astype(bf16)
    pw["b_r"] = jnp.pad(b_router[0], (0, LANES - N_EXPERTS), constant_values=NEG_INF)[None, :]
    pw["w_gu"] = _wprep(w_gate_up[0])
    bgu = b_gate_up[0]
    pw["b_gu"] = jnp.concatenate([bgu[:, 0::2], bgu[:, 1::2]], axis=-1)[:, None, :]
    pw["w_d"] = w_down[0].astype(bf16)
    pw["b_d"] = b_down[0][:, None, :]
    pw["g_ple"] = g_ple[0][None, :]
    pw["w_ple_gate"] = w_ple_gate[0].astype(bf16)
    pw["w_ple_proj"] = w_ple_proj[0].astype(bf16)
    pw["g_ple_post"] = g_ple_post[0][None, :]
    return pw


def _trunk(x, p, pw):
    batch, seq_len, _ = x.shape
    n = batch * seq_len
    x2 = x.reshape(n, D_MODEL)
    qk, v_m, o_m, gates, qa, ka, va = _inproj(x2, pw["g_mix"], pw["w_main"], pw["w_gate"], pw["b_gate"],
                                              pw["conv_w"], seq_len)
    g4 = gates[:, :N_GATES].reshape(batch, seq_len, 2, 2 * M_HEADS)
    gcol = g4.transpose(0, 2, 1, 3)
    grow = g4.transpose(0, 2, 3, 1)
    hfb = _mlstm(qk, v_m, gcol, grow, batch, seq_len)
    ha = _natten(qa, ka, va, pw["bias_tab"], pw["g_q2"], pw["g_k2"], batch, seq_len)
    xmid, hn, ids, pos, gate, cnt = _outproj(x2, hfb, o_m, ha, pw["g_m"], pw["g_a"], pw["w_out"], pw["g_ffn"],
                                             pw["wr1"], pw["wr2"], pw["b_r"], batch, seq_len)

    tm_e = TM_EXP
    counts = cnt[0, :N_EXPERTS].astype(i32)
    padded = (counts + tm_e - 1) // tm_e * tm_e
    pend = jnp.cumsum(padded).astype(i32)
    pstart = pend - padded
    eid = jnp.arange(N_EXPERTS, dtype=i32)
    slot = (jnp.sum(jnp.where(ids[:, :, None] == eid, pstart, 0), axis=-1) + pos).reshape(-1)
    nk = n * TOP_K
    n_blocks = (nk + N_EXPERTS * (tm_e - 1) + tm_e - 1) // tm_e
    cap = n_blocks * tm_e
    block_start = jnp.arange(n_blocks, dtype=i32) * tm_e
    block_e = jnp.minimum(jnp.sum((pend[None, :] <= block_start[:, None]).astype(i32), axis=1), N_EXPERTS - 1)
    nvalid = (pend[-1:] // tm_e).astype(i32)

    xb = _dispatch(hn, slot, pend, padded, cap)
    yb = _experts(xb, block_e, nvalid, pw["w_gu"], pw["b_gu"], pw["w_d"], pw["b_d"])
    out = _combine(slot, xmid, gate, p.reshape(n, PLE_DIM), pw["w_ple_proj"], pw["g_ple_post"], pw["g_ple"],
                   pw["w_ple_gate"], yb)
    return out.reshape(batch, seq_len, D_MODEL)


def kernel(x_prompt, x_sample, p_prompt, p_sample, g_mix, w_in, b_gates, conv_w, g_m_head, g_q, g_k, rel_bias,
           g_a_out, w_out, g_ffn, w_router, b_router, w_gate_up, b_gate_up, w_down, b_down, g_ple, w_ple_gate,
           w_ple_proj, g_ple_post):
    assert w_in.shape[0] == 1, "single-layer trunk"
    pw = _prep_weights(g_mix, w_in, b_gates, conv_w, g_m_head, g_q, g_k, rel_bias, g_a_out, w_out, g_ffn,
                       w_router, b_router, w_gate_up, b_gate_up, w_down, b_down, g_ple, w_ple_gate, w_ple_proj,
                       g_ple_post)
    y_prompt = _trunk(x_prompt, p_prompt[0], pw)
    y_sample = _trunk(x_sample, p_sample[0], pw)
    return (y_prompt, y_sample)
```

```python
import functools

import jax
import jax.numpy as jnp
from jax import lax
from jax.experimental import pallas as pl
from jax.experimental.pallas import tpu as pltpu

f32 = jnp.float32
bf16 = jnp.bfloat16
i32 = jnp.int32

D_MODEL = 1024
HEAD_DIM = 64
M_HEADS = 8
A_HEADS = 8
M_WIDTH = M_HEADS * HEAD_DIM
A_WIDTH = A_HEADS * HEAD_DIM
N_GATES = 4 * M_HEADS
GRID_W = 64
WIN_H = 8
WIN_W = 16
N_EXPERTS = 32
TOP_K = 4
D_FF = 1024
SWIGLU_LIMIT = 7.0
SWIGLU_ALPHA = 1.702
PLE_DIM = 256
EPS = 1e-6

LANES = 128
SUBLANES = 8
VMEM_LIMIT = 56 * 1024 * 1024

TM_IN = 512
MLSTM_CHUNK = 256
NAT_ROWS = 8
TM_OUT = 512
TM_EXP = 256
TM_DISPATCH = 256
TR_WPREP = 512
TM_CMB = 256

NEG_INF = float("-inf")


def _cparams(sem):
    return pltpu.CompilerParams(dimension_semantics=sem, vmem_limit_bytes=VMEM_LIMIT)


def _rms(xv, g):
    return xv * lax.rsqrt(jnp.mean(xv * xv, axis=-1, keepdims=True) + EPS) * g


def _split2(a):
    hi = a.astype(bf16)
    lo = (a - hi.astype(f32)).astype(bf16)
    return hi, lo


def _split3(a):
    hi = a.astype(bf16)
    r = a - hi.astype(f32)
    mid = r.astype(bf16)
    lo = (r - mid.astype(f32)).astype(bf16)
    return hi, mid, lo


def _head_mean_sq(xv, width):
    a = lax.broadcasted_iota(i32, (width, width), 0) // HEAD_DIM
    b = lax.broadcasted_iota(i32, (width, width), 1) // HEAD_DIM
    bd = jnp.where(a == b, 1.0 / HEAD_DIM, 0.0).astype(bf16)
    hi, lo = _split2(xv * xv)
    return (jnp.dot(hi, bd, preferred_element_type=f32) + jnp.dot(lo, bd, preferred_element_type=f32))


def _log_sigmoid(x):
    return jnp.minimum(x, 0.0) - jnp.log1p(jnp.exp(-jnp.abs(x)))


def _inproj_kernel(x_ref, xp_ref, xn_ref, g_ref, wm_ref, wg_ref, bg_ref, cw_ref,
                   qk_ref, v_ref, o_ref, gates_ref, qa_ref, ka_ref, va_ref, *, tm, seq_len):
    i = pl.program_id(0)
    g = g_ref[...]
    h = _rms(x_ref[...], g).astype(bf16)
    z = jnp.dot(h, wm_ref[...], preferred_element_type=f32)
    gates_ref[...] = jnp.dot(h, wg_ref[...], preferred_element_type=f32) + bg_ref[...]
    hh = _rms(jnp.concatenate([xp_ref[...], xn_ref[...]], axis=0), g).astype(bf16)
    zh = jnp.dot(hh, wm_ref[:, :2 * M_WIDTH], preferred_element_type=f32)
    start = lax.rem(i * tm, seq_len)
    prev_row = jnp.where(start == 0, 0.0, zh[SUBLANES - 1:SUBLANES, :])
    next_row = jnp.where(start + tm == seq_len, 0.0, zh[SUBLANES:SUBLANES + 1, :])
    u = z[:, :2 * M_WIDTH]
    rid = lax.broadcasted_iota(i32, (tm, 1), 0)
    u_prev = jnp.where(rid == 0, prev_row, pltpu.roll(u, 1, 0))
    u_next = jnp.where(rid == tm - 1, next_row, pltpu.roll(u, tm - 1, 0))
    cw = cw_ref[...]
    c = u_prev * cw[0:1, :] + u * cw[1:2, :] + u_next * cw[2:3, :]
    qk_ref[...] = (c * jax.nn.sigmoid(c)).astype(bf16)
    v_ref[...] = z[:, 2 * M_WIDTH:3 * M_WIDTH].astype(bf16)
    o_ref[...] = z[:, 3 * M_WIDTH:4 * M_WIDTH]
    base = 4 * M_WIDTH
    qa_ref[...] = z[:, base:base + A_WIDTH]
    ka_ref[...] = z[:, base + A_WIDTH:base + 2 * A_WIDTH]
    va_ref[...] = z[:, base + 2 * A_WIDTH:base + 3 * A_WIDTH].astype(bf16)


def _inproj(x2, g_mix, w_main, w_gate, b_gate, conv_w, seq_len):
    n = x2.shape[0]
    tm = TM_IN
    nb8 = n // SUBLANES
    r8 = tm // SUBLANES
    wcols = w_main.shape[1]
    row = lambda i: (i, 0)
    const = lambda i: (0, 0)
    out_shapes = (
        jax.ShapeDtypeStruct((n, 2 * M_WIDTH), bf16),
        jax.ShapeDtypeStruct((n, M_WIDTH), bf16),
        jax.ShapeDtypeStruct((n, M_WIDTH), f32),
        jax.ShapeDtypeStruct((n, LANES), f32),
        jax.ShapeDtypeStruct((n, A_WIDTH), f32),
        jax.ShapeDtypeStruct((n, A_WIDTH), f32),
        jax.ShapeDtypeStruct((n, A_WIDTH), bf16),
    )
    return pl.pallas_call(
        functools.partial(_inproj_kernel, tm=tm, seq_len=seq_len),
        grid=(n // tm,),
        in_specs=[
            pl.BlockSpec((tm, D_MODEL), row),
            pl.BlockSpec((SUBLANES, D_MODEL), lambda i: (jnp.maximum(i * r8 - 1, 0), 0)),
            pl.BlockSpec((SUBLANES, D_MODEL), lambda i: (jnp.minimum((i + 1) * r8, nb8 - 1), 0)),
            pl.BlockSpec((1, D_MODEL), const),
            pl.BlockSpec((D_MODEL, wcols), const),
            pl.BlockSpec((D_MODEL, LANES), const),
            pl.BlockSpec((1, LANES), const),
            pl.BlockSpec((3, 2 * M_WIDTH), const),
        ],
        out_specs=[
            pl.BlockSpec((tm, 2 * M_WIDTH), row),
            pl.BlockSpec((tm, M_WIDTH), row),
            pl.BlockSpec((tm, M_WIDTH), row),
            pl.BlockSpec((tm, LANES), row),
            pl.BlockSpec((tm, A_WIDTH), row),
            pl.BlockSpec((tm, A_WIDTH), row),
            pl.BlockSpec((tm, A_WIDTH), row),
        ],
        out_shape=out_shapes,
        compiler_params=_cparams(("parallel",)),
        name="inproj",
    )(x2, x2, x2, g_mix, w_main, w_gate, b_gate, conv_w)


def _mlstm_kernel(q_ref, k_ref, v_ref, g_ref, out_ref, c_ref, m_ref, *, L):
    d = pl.program_id(1)
    c = pl.program_id(2)

    @pl.when(c == 0)
    def _():
        c_ref[...] = jnp.zeros_like(c_ref)
        m_ref[...] = jnp.zeros_like(m_ref)

    fwd = d == 0
    sgn = jnp.where(fwd, 1, -1)
    row = lax.broadcasted_iota(i32, (L, L), 0)
    col = lax.broadcasted_iota(i32, (L, L), 1)
    mask = (row - col) * sgn >= 0
    t_col = mask.astype(bf16)

    g = g_ref[...]
    gd = jnp.where(fwd, g, pltpu.roll(g, LANES - 2 * M_HEADS, 1))
    lf = _log_sigmoid(pltpu.roll(gd, LANES - M_HEADS, 1))
    b = sum(jnp.dot(t_col, part, preferred_element_type=f32) for part in _split3(lf))
    bl = jnp.sum(lf, axis=0, keepdims=True)
    r = gd - b
    rid = lax.broadcasted_iota(i32, (L, 1), 0)
    cm = r
    sh = 1
    while sh < L:
        from_before = jnp.where(rid >= sh, pltpu.roll(cm, sh, 0), NEG_INF)
        from_after = jnp.where(rid < L - sh, pltpu.roll(cm, L - sh, 0), NEG_INF)
        cm = jnp.maximum(cm, jnp.where(fwd, from_before, from_after))
        sh *= 2
    m_prev = m_ref[0:1, :]
    mx = jnp.maximum(m_prev, cm)
    w_int = jnp.exp(m_prev - mx)
    e_col = jnp.exp(-(b + mx))
    m_new = bl + jnp.maximum(m_prev, jnp.max(r, axis=0, keepdims=True))
    wk = jnp.exp(bl + r - m_new)
    decay = jnp.exp(bl + m_prev - m_new)
    r_row = r.T
    m_ref[0:1, :] = m_new

    lane = lax.broadcasted_iota(i32, (1, LANES), 1)
    halves = (lane < HEAD_DIM, lane >= HEAD_DIM)
    nt = (((1,), (1,)), ((), ()))
    tn = (((0,), (0,)), ((), ()))

    for p in range(M_HEADS // 2):
        sl = slice(p * LANES, (p + 1) * LANES)
        q2 = q_ref[:, sl]
        k2 = k_ref[:, sl]
        v2 = v_ref[:, sl]
        nums, floors = [], []
        for half in range(2):
            h = 2 * p + half
            inh = halves[half]
            qz = jnp.where(inh, q2 * (HEAD_DIM ** -0.5), jnp.zeros_like(q2))
            kz = jnp.where(inh, k2, jnp.zeros_like(k2))
            va = jnp.where(inh, v2, jnp.ones_like(v2))
            arg = r_row[h:h + 1, :] - mx[:, h:h + 1]
            qk = lax.dot_general(qz, k2, nt, preferred_element_type=f32)
            s = jnp.where(mask, qk * jnp.exp(arg), 0.0).astype(bf16)
            cst = c_ref[h]
            nums.append(w_int[:, h:h + 1] * jnp.dot(qz, cst.astype(bf16), preferred_element_type=f32)
                        + jnp.dot(s, va, preferred_element_type=f32))
            floors.append(e_col[:, h:h + 1])
            kw = (kz.astype(f32) * wk[:, h:h + 1]).astype(bf16)
            c_ref[h] = decay[:, h:h + 1] * cst + lax.dot_general(kw, va, tn, preferred_element_type=f32)
        numer = jnp.where(halves[0], nums[0], nums[1])
        den = pltpu.roll(jnp.where(halves[0], nums[1], nums[0]), HEAD_DIM, 1)
        floor = jnp.where(halves[0], floors[0], floors[1])
        out_ref[:, sl] = numer / jnp.maximum(jnp.abs(den), floor)


def _mlstm(qk, v, gates, batch, seq_len):
    L = MLSTM_CHUNK
    nc = seq_len // L
    qk3 = qk.reshape(batch, seq_len, 2 * M_WIDTH)
    v3 = v.reshape(batch, seq_len, M_WIDTH)
    g3 = gates.reshape(batch, seq_len, LANES)
    cidx = lambda d, c: jnp.where(d == 0, c, nc - 1 - c)
    return pl.pallas_call(
        functools.partial(_mlstm_kernel, L=L),
        grid=(batch, 2, nc),
        in_specs=[
            pl.BlockSpec((None, L, M_WIDTH), lambda b, d, c: (b, cidx(d, c), 0)),
            pl.BlockSpec((None, L, M_WIDTH), lambda b, d, c: (b, cidx(d, c), 1)),
            pl.BlockSpec((None, L, M_WIDTH), lambda b, d, c: (b, cidx(d, c), 0)),
            pl.BlockSpec((None, L, LANES), lambda b, d, c: (b, cidx(d, c), 0)),
        ],
        out_specs=pl.BlockSpec((None, None, L, M_WIDTH), lambda b, d, c: (b, d, cidx(d, c), 0)),
        out_shape=jax.ShapeDtypeStruct((batch, 2, seq_len, M_WIDTH), f32),
        scratch_shapes=[pltpu.VMEM((M_HEADS, LANES, LANES), f32), pltpu.VMEM((M_HEADS, LANES), f32)],
        compiler_params=_cparams(("parallel", "parallel", "arbitrary")),
        name="mlstm",
    )(qk3, qk3, v3, g3)


def _natten_kernel(q_ref, k_ref, v_ref, bias_ref, gq_ref, gk_ref, out_ref, kn_ref, *, seq_len, rb_rows):
    rb = pl.program_id(2)
    rows = seq_len // GRID_W
    norm_rows = 512

    @pl.when(rb == 0)
    def _():
        def body(i, carry):
            sl = pl.ds(pl.multiple_of(i * norm_rows, norm_rows), norm_rows)
            kv = k_ref[sl, :]
            kn_ref[sl, :] = (kv * lax.rsqrt(_head_mean_sq(kv, LANES) + EPS) * gk_ref[...]).astype(bf16)
            return carry
        lax.fori_loop(0, seq_len // norm_rows, body, 0)

    qv = q_ref[...]
    qn = (qv * lax.rsqrt(_head_mean_sq(qv, LANES) + EPS) * gq_ref[...] * (HEAD_DIM ** -0.5)).astype(bf16)
    lane = lax.broadcasted_iota(i32, (1, LANES), 1)
    lo = lane < HEAD_DIM
    nt = (((1,), (1,)), ((), ()))
    pair = 2 * GRID_W

    def window(j):
        r = rb * rb_rows + j
        r0 = jnp.clip(r - WIN_H // 2, 0, rows - WIN_H)
        return r - r0, pl.ds(pl.multiple_of(r0 * GRID_W, GRID_W), WIN_H * GRID_W)

    tiles = []
    for j in range(rb_rows):
        delta, ks = window(j)
        qj = qn[j * GRID_W:(j + 1) * GRID_W, :]
        q2 = jnp.concatenate([jnp.where(lo, qj, jnp.zeros_like(qj)), jnp.where(lo, jnp.zeros_like(qj), qj)], axis=0)
        tiles.append(lax.dot_general(q2, kn_ref[ks, :], nt, preferred_element_type=f32) + bias_ref[delta])
    s = jnp.concatenate(tiles, axis=0)
    p = jnp.exp(s - jnp.max(s, axis=-1, keepdims=True))
    inv = 1.0 / jnp.sum(p, axis=-1, keepdims=True)
    pb = p.astype(bf16)
    for j in range(rb_rows):
        _, ks = window(j)
        o = jnp.dot(pb[j * pair:(j + 1) * pair, :], v_ref[ks, :], preferred_element_type=f32)
        o = o * inv[j * pair:(j + 1) * pair, :]
        out_ref[j * GRID_W:(j + 1) * GRID_W, :] = jnp.where(lo, o[:GRID_W, :], o[GRID_W:, :])


def _natten_bias_table(rel_bias):
    cq = jnp.arange(GRID_W)[:, None]
    ck = jnp.arange(GRID_W)[None, :]
    c0 = jnp.clip(cq - WIN_W // 2, 0, GRID_W - WIN_W)
    col_in = (ck >= c0) & (ck < c0 + WIN_W)
    idx_c = jnp.clip(ck - cq, -(WIN_W - 1), WIN_W - 1) + (WIN_W - 1)
    pick = idx_c[:, :, None] == jnp.arange(2 * WIN_W - 1)
    tz = jnp.sum(jnp.where(pick[None, None], rel_bias.astype(f32)[:, :, None, None, :], 0.0), axis=-1)
    tz = jnp.where(col_in[None, None], tz, NEG_INF)
    tab = jnp.stack([tz[:, WIN_H - 1 - dl:2 * WIN_H - 1 - dl] for dl in range(WIN_H)], axis=1)
    tab = tab.transpose(0, 1, 3, 2, 4).reshape(A_HEADS // 2, 2, WIN_H, GRID_W, WIN_H * GRID_W)
    return tab.transpose(0, 2, 1, 3, 4).reshape(A_HEADS // 2, WIN_H, 2 * GRID_W, WIN_H * GRID_W)


def _natten(qa, ka, va, bias_tab, g_q2, g_k2, batch, seq_len):
    rows = seq_len // GRID_W
    rbr = NAT_ROWS
    tq = rbr * GRID_W
    q3 = qa.reshape(batch, seq_len, A_WIDTH)
    k3 = ka.reshape(batch, seq_len, A_WIDTH)
    v3 = va.reshape(batch, seq_len, A_WIDTH)
    return pl.pallas_call(
        functools.partial(_natten_kernel, seq_len=seq_len, rb_rows=rbr),
        grid=(batch, A_HEADS // 2, rows // rbr),
        in_specs=[
            pl.BlockSpec((None, tq, LANES), lambda b, hp, rb: (b, rb, hp)),
            pl.BlockSpec((None, seq_len, LANES), lambda b, hp, rb: (b, 0, hp)),
            pl.BlockSpec((None, seq_len, LANES), lambda b, hp, rb: (b, 0, hp)),
            pl.BlockSpec((None, WIN_H, 2 * GRID_W, WIN_H * GRID_W), lambda b, hp, rb: (hp, 0, 0, 0)),
            pl.BlockSpec((1, LANES), lambda b, hp, rb: (0, 0)),
            pl.BlockSpec((1, LANES), lambda b, hp, rb: (0, 0)),
        ],
        out_specs=pl.BlockSpec((None, tq, LANES), lambda b, hp, rb: (b, rb, hp)),
        out_shape=jax.ShapeDtypeStruct((batch, seq_len, A_WIDTH), f32),
        scratch_shapes=[pltpu.VMEM((seq_len, LANES), bf16)],
        compiler_params=_cparams(("parallel", "parallel", "arbitrary")),
        name="natten",
    )(q3, k3, v3, bias_tab, g_q2, g_k2)


def _outproj_kernel(x_ref, hf_ref, hb_ref, o_ref, ha_ref, gm_ref, ga_ref, wo_ref, gf_ref,
                    wr1_ref, wr2_ref, br_ref,
                    xmid_ref, hn_ref, ids_ref, pos_ref, gate_ref, cnt_ref, carry_ref, *, tm):
    i = pl.program_id(0)

    @pl.when(i == 0)
    def _():
        carry_ref[...] = jnp.zeros_like(carry_ref)

    hm = hf_ref[...] + hb_ref[...]
    hm = hm * lax.rsqrt(_head_mean_sq(hm, M_WIDTH) + EPS) * gm_ref[...] * jax.nn.sigmoid(o_ref[...])
    ha = _rms(ha_ref[...], ga_ref[...])
    mix = (jnp.dot(hm.astype(bf16), wo_ref[0:M_WIDTH, :], preferred_element_type=f32)
           + jnp.dot(ha.astype(bf16), wo_ref[M_WIDTH:M_WIDTH + A_WIDTH, :], preferred_element_type=f32))
    xm = x_ref[...] + mix
    xmid_ref[...] = xm
    hn = _rms(xm, gf_ref[...])
    hn_ref[...] = hn

    h1, h2 = _split2(hn)
    logits = (jnp.dot(h1, wr1_ref[...], preferred_element_type=f32)
              + (jnp.dot(h1, wr2_ref[...], preferred_element_type=f32)
                 + jnp.dot(h2, wr1_ref[...], preferred_element_type=f32))) + br_ref[...]
    lane = lax.broadcasted_iota(i32, (tm, LANES), 1)
    work = logits
    vals, idxs, sels = [], [], []
    for _ in range(TOP_K):
        mx = jnp.max(work, axis=-1, keepdims=True)
        idx = jnp.min(jnp.where(work == mx, lane, LANES), axis=-1, keepdims=True)
        sel = lane == idx
        vals.append(mx)
        idxs.append(idx)
        sels.append(sel)
        work = jnp.where(sel, NEG_INF, work)
    es = [jnp.exp(v - vals[0]) for v in vals]
    tot = es[0] + es[1] + es[2] + es[3]

    onehot = jnp.where(sels[0] | sels[1] | sels[2] | sels[3], 1.0, 0.0)
    tri = (lax.broadcasted_iota(i32, (tm, tm), 0) > lax.broadcasted_iota(i32, (tm, tm), 1)).astype(bf16)
    base = jnp.dot(tri, onehot.astype(bf16), preferred_element_type=f32) + carry_ref[...]
    ids_out = jnp.zeros((tm, LANES), i32)
    pos_out = jnp.zeros((tm, LANES), i32)
    gate_out = jnp.zeros((tm, LANES), f32)
    for k in range(TOP_K):
        pk = jnp.sum(jnp.where(sels[k], base, 0.0), axis=-1, keepdims=True).astype(i32)
        ids_out = jnp.where(lane == k, idxs[k], ids_out)
        pos_out = jnp.where(lane == k, pk, pos_out)
        gate_out = jnp.where(lane == k, es[k] / tot, gate_out)
    ids_ref[...] = ids_out[:, :TOP_K]
    pos_ref[...] = pos_out[:, :TOP_K]
    gate_ref[...] = gate_out[:, :TOP_K]
    carry_ref[...] += jnp.sum(onehot, axis=0, keepdims=True)
    cnt_ref[...] = carry_ref[...]


def _outproj(x2, hfb, o_m, ha, g_m, g_a, w_out, g_ffn, wr1, wr2, b_r, batch, seq_len):
    n = x2.shape[0]
    tm = TM_OUT
    tpb = seq_len // tm
    row = lambda i: (i, 0)
    const = lambda i: (0, 0)
    ha2 = ha.reshape(n, A_WIDTH)
    return pl.pallas_call(
        functools.partial(_outproj_kernel, tm=tm),
        grid=(n // tm,),
        in_specs=[
            pl.BlockSpec((tm, D_MODEL), row),
            pl.BlockSpec((None, None, tm, M_WIDTH), lambda i: (i // tpb, 0, i % tpb, 0)),
            pl.BlockSpec((None, None, tm, M_WIDTH), lambda i: (i // tpb, 1, i % tpb, 0)),
            pl.BlockSpec((tm, M_WIDTH), row),
            pl.BlockSpec((tm, A_WIDTH), row),
            pl.BlockSpec((1, M_WIDTH), const),
            pl.BlockSpec((1, A_WIDTH), const),
            pl.BlockSpec((M_WIDTH + A_WIDTH, D_MODEL), const),
            pl.BlockSpec((1, D_MODEL), const),
            pl.BlockSpec((D_MODEL, LANES), const),
            pl.BlockSpec((D_MODEL, LANES), const),
            pl.BlockSpec((1, LANES), const),
        ],
        out_specs=[
            pl.BlockSpec((tm, D_MODEL), row),
            pl.BlockSpec((tm, D_MODEL), row),
            pl.BlockSpec((tm, TOP_K), row),
            pl.BlockSpec((tm, TOP_K), row),
            pl.BlockSpec((tm, TOP_K), row),
            pl.BlockSpec((1, LANES), const),
        ],
        out_shape=(
            jax.ShapeDtypeStruct((n, D_MODEL), f32),
            jax.ShapeDtypeStruct((n, D_MODEL), f32),
            jax.ShapeDtypeStruct((n, TOP_K), i32),
            jax.ShapeDtypeStruct((n, TOP_K), i32),
            jax.ShapeDtypeStruct((n, TOP_K), f32),
            jax.ShapeDtypeStruct((1, LANES), f32),
        ),
        scratch_shapes=[pltpu.VMEM((1, LANES), f32)],
        compiler_params=_cparams(("arbitrary",)),
        name="outproj_router",
    )(x2, hfb, hfb, o_m, ha2, g_m, g_a, w_out, g_ffn, wr1, wr2, b_r)


def _dispatch_kernel(pend_ref, padded_ref, slot_ref, hn_ref, xb_hbm, zero_ref, sem, *, tm, tm_e, n_blocks):
    i = pl.program_id(0)

    def zero_copy(e):
        return pltpu.make_async_copy(zero_ref, xb_hbm.at[pl.ds(pl.multiple_of(pend_ref[e] - tm_e, tm_e), tm_e)], sem)

    def tail_copy(b):
        return pltpu.make_async_copy(zero_ref, xb_hbm.at[pl.ds(pl.multiple_of(b * tm_e, tm_e), tm_e)], sem)

    @pl.when(i == 0)
    def _():
        zero_ref[...] = jnp.zeros_like(zero_ref)
        for e in range(N_EXPERTS):
            @pl.when(padded_ref[e] > 0)
            def _():
                zero_copy(e).start()
        used = pend_ref[N_EXPERTS - 1] // tm_e

        def tail_start(b, carry):
            tail_copy(b).start()
            return carry

        def tail_wait(b, carry):
            tail_copy(b).wait()
            return carry

        lax.fori_loop(used, n_blocks, tail_start, 0)
        for e in range(N_EXPERTS):
            @pl.when(padded_ref[e] > 0)
            def _():
                zero_copy(e).wait()
        lax.fori_loop(used, n_blocks, tail_wait, 0)

    def row(j, k):
        return pltpu.make_async_copy(hn_ref.at[pl.ds(j, 1)],
                                     xb_hbm.at[pl.ds(slot_ref[0, 0, j * TOP_K + k], 1)], sem)

    def issue(j, carry):
        for k in range(TOP_K):
            row(j, k).start(priority=k % 2)
        return carry

    lax.fori_loop(0, tm, issue, 0, unroll=4)

    def drain(j, carry):
        for k in range(TOP_K):
            row(0, k).wait()
        return carry

    lax.fori_loop(0, tm, drain, 0, unroll=4)


def _dispatch(hn, slot_flat, pend, padded, cap):
    n = hn.shape[0]
    tm = TM_DISPATCH
    slot3 = slot_flat.reshape(n // tm, 1, tm * TOP_K)
    grid_spec = pltpu.PrefetchScalarGridSpec(
        num_scalar_prefetch=2,
        grid=(n // tm,),
        in_specs=[
            pl.BlockSpec((1, 1, tm * TOP_K), lambda i, pe, pa: (i, 0, 0), memory_space=pltpu.SMEM),
            pl.BlockSpec((tm, D_MODEL), lambda i, pe, pa: (i, 0)),
        ],
        out_specs=pl.BlockSpec(memory_space=pl.ANY),
        scratch_shapes=[pltpu.VMEM((TM_EXP, D_MODEL), f32), pltpu.SemaphoreType.DMA(())],
    )
    return pl.pallas_call(
        functools.partial(_dispatch_kernel, tm=tm, tm_e=TM_EXP, n_blocks=cap // TM_EXP),
        grid_spec=grid_spec,
        out_shape=jax.ShapeDtypeStruct((cap, D_MODEL), f32),
        compiler_params=_cparams(("arbitrary",)),
        name="dispatch",
    )(pend, padded, slot3, hn)


def _experts_kernel(be_ref, nv_ref, xb_ref, wgu_ref, bgu_ref, wd_ref, bd_ref, yb_ref):
    j = pl.program_id(0)

    @pl.when(j < nv_ref[0])
    def _():
        xv = xb_ref[...].astype(bf16)
        h = jnp.dot(xv, wgu_ref[...], preferred_element_type=f32) + bgu_ref[...]
        gt = jnp.minimum(h[:, :D_FF], SWIGLU_LIMIT)
        up = jnp.clip(h[:, D_FF:], -SWIGLU_LIMIT, SWIGLU_LIMIT)
        act = (up + 1.0) * (gt * jax.nn.sigmoid(SWIGLU_ALPHA * gt))
        yb_ref[...] = jnp.dot(act.astype(bf16), wd_ref[...], preferred_element_type=f32) + bd_ref[...]

    @pl.when(j >= nv_ref[0])
    def _():
        yb_ref[...] = jnp.zeros_like(yb_ref)


def _experts(xb, block_e, nvalid, w_gu, b_gu, w_d, b_d):
    cap = xb.shape[0]
    tm = TM_EXP
    grid_spec = pltpu.PrefetchScalarGridSpec(
        num_scalar_prefetch=2,
        grid=(cap // tm,),
        in_specs=[
            pl.BlockSpec((tm, D_MODEL), lambda j, be, nv: (jnp.minimum(j, nv[0] - 1), 0)),
            pl.BlockSpec((None, D_MODEL, 2 * D_FF), lambda j, be, nv: (be[j], 0, 0)),
            pl.BlockSpec((None, 1, 2 * D_FF), lambda j, be, nv: (be[j], 0, 0)),
            pl.BlockSpec((None, D_FF, D_MODEL), lambda j, be, nv: (be[j], 0, 0)),
            pl.BlockSpec((None, 1, D_MODEL), lambda j, be, nv: (be[j], 0, 0)),
        ],
        out_specs=pl.BlockSpec((tm, D_MODEL), lambda j, be, nv: (j, 0)),
    )
    return pl.pallas_call(
        _experts_kernel,
        grid_spec=grid_spec,
        out_shape=jax.ShapeDtypeStruct((cap, D_MODEL), f32),
        compiler_params=_cparams(("arbitrary",)),
        name="experts",
    )(block_e, nvalid, xb, w_gu, b_gu, w_d, b_d)


def _combine_kernel(slot_ref, xmid_ref, gate_ref, p_ref, wproj_ref, gpost_ref, gple_ref, wgate_ref, yb_hbm,
                    out_ref, ybuf_ref, sem, *, tm):
    def row(j, k):
        return pltpu.make_async_copy(yb_hbm.at[pl.ds(slot_ref[0, 0, j * TOP_K + k], 1)],
                                     ybuf_ref.at[k, pl.ds(j, 1)], sem)

    def issue(j, carry):
        for k in range(TOP_K):
            row(j, k).start(priority=k % 2)
        return carry

    lax.fori_loop(0, tm, issue, 0, unroll=4)

    pe = _rms(jnp.dot(p_ref[...].astype(bf16), wproj_ref[...], preferred_element_type=f32), gpost_ref[...])

    def drain(j, carry):
        for k in range(TOP_K):
            row(0, k).wait()
        return carry

    lax.fori_loop(0, tm, drain, 0, unroll=4)

    gate = gate_ref[...]
    y = gate[:, 0:1] * ybuf_ref[0]
    for k in range(1, TOP_K):
        y = y + gate[:, k:k + 1] * ybuf_ref[k]
    x2 = xmid_ref[...] + y
    gl = jnp.dot(_rms(x2, gple_ref[...]).astype(bf16), wgate_ref[...], preferred_element_type=f32)
    out_ref[...] = x2 + jax.nn.sigmoid(gl) * pe


def _combine(slot_flat, xmid, gate, p2, w_proj, g_post, g_ple, w_gate, yb):
    n = xmid.shape[0]
    tm = TM_CMB
    slot3 = slot_flat.reshape(n // tm, 1, tm * TOP_K)
    row = lambda i: (i, 0)
    const = lambda i: (0, 0)
    return pl.pallas_call(
        functools.partial(_combine_kernel, tm=tm),
        grid=(n // tm,),
        in_specs=[
            pl.BlockSpec((1, 1, tm * TOP_K), lambda i: (i, 0, 0), memory_space=pltpu.SMEM),
            pl.BlockSpec((tm, D_MODEL), row),
            pl.BlockSpec((tm, TOP_K), row),
            pl.BlockSpec((tm, PLE_DIM), row),
            pl.BlockSpec((PLE_DIM, D_MODEL), const),
            pl.BlockSpec((1, D_MODEL), const),
            pl.BlockSpec((1, D_MODEL), const),
            pl.BlockSpec((D_MODEL, D_MODEL), const),
            pl.BlockSpec(memory_space=pl.ANY),
        ],
        out_specs=pl.BlockSpec((tm, D_MODEL), row),
        out_shape=jax.ShapeDtypeStruct((n, D_MODEL), f32),
        scratch_shapes=[pltpu.VMEM((TOP_K, tm, D_MODEL), f32), pltpu.SemaphoreType.DMA(())],
        compiler_params=_cparams(("arbitrary",)),
        name="combine_ple",
    )(slot3, xmid, gate, p2, w_proj, g_post, g_ple, w_gate, yb)


def _wprep_kernel(w_ref, out_ref):
    grp = 2 * LANES
    src = lax.broadcasted_iota(i32, (grp, grp), 0)
    dst = lax.broadcasted_iota(i32, (grp, grp), 1)
    want = jnp.where(dst < LANES, 2 * dst, 2 * (dst - LANES) + 1)
    perm = (src == want).astype(bf16)
    for g in range(2 * D_FF // grp):
        t = jnp.dot(w_ref[:, g * grp:(g + 1) * grp].astype(bf16), perm, preferred_element_type=f32)
        out_ref[:, g * LANES:(g + 1) * LANES] = t[:, :LANES].astype(bf16)
        out_ref[:, D_FF + g * LANES:D_FF + (g + 1) * LANES] = t[:, LANES:].astype(bf16)


def _wprep(w_gate_up):
    ne = w_gate_up.shape[0]
    tr = TR_WPREP
    return pl.pallas_call(
        _wprep_kernel,
        grid=(ne, D_MODEL // tr),
        in_specs=[pl.BlockSpec((None, tr, 2 * D_FF), lambda e, r: (e, r, 0))],
        out_specs=pl.BlockSpec((None, tr, 2 * D_FF), lambda e, r: (e, r, 0)),
        out_shape=jax.ShapeDtypeStruct((ne, D_MODEL, 2 * D_FF), bf16),
        compiler_params=_cparams(("parallel", "parallel")),
        name="wprep",
    )(w_gate_up)


def _prep_weights(g_mix, w_in, b_gates, conv_w, g_m_head, g_q, g_k, rel_bias, g_a_out, w_out, g_ffn,
                  w_router, b_router, w_gate_up, b_gate_up, w_down, b_down, g_ple, w_ple_gate, w_ple_proj,
                  g_ple_post):
    g0 = 4 * M_WIDTH
    w = w_in[0]
    pw = {}
    pw["g_mix"] = g_mix[0][None, :]
    pw["w_main"] = jnp.concatenate([w[:, :g0], w[:, g0 + N_GATES:]], axis=1).astype(bf16)
    pw["w_gate"] = jnp.pad(w[:, g0:g0 + N_GATES], ((0, 0), (0, LANES - N_GATES))).astype(bf16)
    pw["b_gate"] = jnp.pad(b_gates[0], (0, LANES - N_GATES))[None, :]
    pw["conv_w"] = conv_w[0]
    pw["g_m"] = g_m_head[0].reshape(1, M_WIDTH)
    pw["g_q2"] = jnp.tile(g_q[0], 2)[None, :]
    pw["g_k2"] = jnp.tile(g_k[0], 2)[None, :]
    pw["bias_tab"] = _natten_bias_table(rel_bias[0])
    pw["g_a"] = g_a_out[0][None, :]
    pw["w_out"] = w_out[0].astype(bf16)
    pw["g_ffn"] = g_ffn[0][None, :]
    wr = jnp.pad(w_router[0], ((0, 0), (0, LANES - N_EXPERTS)))
    wr1 = wr.astype(bf16)
    pw["wr1"] = wr1
    pw["wr2"] = (wr - wr1.astype(f32)).astype(bf16)
    pw["b_r"] = jnp.pad(b_router[0], (0, LANES - N_EXPERTS), constant_values=NEG_INF)[None, :]
    pw["w_gu"] = _wprep(w_gate_up[0])
    bgu = b_gate_up[0]
    pw["b_gu"] = jnp.concatenate([bgu[:, 0::2], bgu[:, 1::2]], axis=-1)[:, None, :]
    pw["w_d"] = w_down[0].astype(bf16)
    pw["b_d"] = b_down[0][:, None, :]
    pw["g_ple"] = g_ple[0][None, :]
    pw["w_ple_gate"] = w_ple_gate[0].astype(bf16)
    pw["w_ple_proj"] = w_ple_proj[0].astype(bf16)
    pw["g_ple_post"] = g_ple_post[0][None, :]
    return pw


def _trunk(x, p, pw):
    batch, seq_len, _ = x.shape
    n = batch * seq_len
    x2 = x.reshape(n, D_MODEL)
    qk, v_m, o_m, gates, qa, ka, va = _inproj(x2, pw["g_mix"], pw["w_main"], pw["w_gate"], pw["b_gate"],
                                              pw["conv_w"], seq_len)
    hfb = _mlstm(qk, v_m, gates, batch, seq_len)
    ha = _natten(qa, ka, va, pw["bias_tab"], pw["g_q2"], pw["g_k2"], batch, seq_len)
    xmid, hn, ids, pos, gate, cnt = _outproj(x2, hfb, o_m, ha, pw["g_m"], pw["g_a"], pw["w_out"], pw["g_ffn"],
                                             pw["wr1"], pw["wr2"], pw["b_r"], batch, seq_len)

    tm_e = TM_EXP
    counts = cnt[0, :N_EXPERTS].astype(i32)
    padded = (counts + tm_e - 1) // tm_e * tm_e
    pend = jnp.cumsum(padded).astype(i32)
    pstart = pend - padded
    eid = jnp.arange(N_EXPERTS, dtype=i32)
    slot = (jnp.sum(jnp.where(ids[:, :, None] == eid, pstart, 0), axis=-1) + pos).reshape(-1)
    nk = n * TOP_K
    n_blocks = (nk + N_EXPERTS * (tm_e - 1) + tm_e - 1) // tm_e
    cap = n_blocks * tm_e
    block_start = jnp.arange(n_blocks, dtype=i32) * tm_e
    block_e = jnp.minimum(jnp.sum((pend[None, :] <= block_start[:, None]).astype(i32), axis=1), N_EXPERTS - 1)
    nvalid = (pend[-1:] // tm_e).astype(i32)

    xb = _dispatch(hn, slot, pend, padded, cap)
    yb = _experts(xb, block_e, nvalid, pw["w_gu"], pw["b_gu"], pw["w_d"], pw["b_d"])
    out = _combine(slot, xmid, gate, p.reshape(n, PLE_DIM), pw["w_ple_proj"], pw["g_ple_post"], pw["g_ple"],
                   pw["w_ple_gate"], yb)
    return out.reshape(batch, seq_len, D_MODEL)


def kernel(x_prompt, x_sample, p_prompt, p_sample, g_mix, w_in, b_gates, conv_w, g_m_head, g_q, g_k, rel_bias,
           g_a_out, w_out, g_ffn, w_router, b_router, w_gate_up, b_gate_up, w_down, b_down, g_ple, w_ple_gate,
           w_ple_proj, g_ple_post):
    assert w_in.shape[0] == 1, "single-layer trunk"
    pw = _prep_weights(g_mix, w_in, b_gates, conv_w, g_m_head, g_q, g_k, rel_bias, g_a_out, w_out, g_ffn,
                       w_router, b_router, w_gate_up, b_gate_up, w_down, b_down, g_ple, w_ple_gate, w_ple_proj,
                       g_ple_post)
    y_prompt = _trunk(x_prompt, p_prompt[0], pw)
    y_sample = _trunk(x_sample, p_sample[0], pw)
    return (y_prompt, y_sample)
```

```python
import functools

import jax
import jax.numpy as jnp
from jax import lax
from jax.experimental import pallas as pl
from jax.experimental.pallas import tpu as pltpu

f32 = jnp.float32
bf16 = jnp.bfloat16
i32 = jnp.int32

D_MODEL = 1024
HEAD_DIM = 64
M_HEADS = 8
A_HEADS = 8
M_WIDTH = M_HEADS * HEAD_DIM
A_WIDTH = A_HEADS * HEAD_DIM
N_GATES = 4 * M_HEADS
GRID_W = 64
WIN_H = 8
WIN_W = 16
N_EXPERTS = 32
TOP_K = 4
D_FF = 1024
SWIGLU_LIMIT = 7.0
SWIGLU_ALPHA = 1.702
PLE_DIM = 256
EPS = 1e-6

LANES = 128
SUBLANES = 8
VMEM_LIMIT = 56 * 1024 * 1024

TM_IN = 512
MLSTM_CHUNK = 256
NAT_ROWS = 8
TM_OUT = 512
TM_EXP = 256
TM_DISPATCH = 256
TR_WPREP = 512
TM_CMB = 256

NEG_INF = float("-inf")


def _cparams(sem):
    return pltpu.CompilerParams(dimension_semantics=sem, vmem_limit_bytes=VMEM_LIMIT)


def _rms(xv, g):
    return xv * lax.rsqrt(jnp.mean(xv * xv, axis=-1, keepdims=True) + EPS) * g


def _split2(a):
    hi = a.astype(bf16)
    lo = (a - hi.astype(f32)).astype(bf16)
    return hi, lo


def _split3(a):
    hi = a.astype(bf16)
    r = a - hi.astype(f32)
    mid = r.astype(bf16)
    lo = (r - mid.astype(f32)).astype(bf16)
    return hi, mid, lo


def _head_mean_sq(xv, width):
    a = lax.broadcasted_iota(i32, (width, width), 0) // HEAD_DIM
    b = lax.broadcasted_iota(i32, (width, width), 1) // HEAD_DIM
    bd = jnp.where(a == b, 1.0 / HEAD_DIM, 0.0).astype(bf16)
    hi, lo = _split2(xv * xv)
    return (jnp.dot(hi, bd, preferred_element_type=f32) + jnp.dot(lo, bd, preferred_element_type=f32))


TOKEN_TILE_ROWS = D_MODEL // LANES


def _store_token_tiles(ref, val):
    n = val.shape[0]
    for c in range(TOKEN_TILE_ROWS):
        ref[pl.ds(c, n, stride=TOKEN_TILE_ROWS), :] = val[:, c * LANES:(c + 1) * LANES]


def _token_tile(ref, t):
    return ref.at[pl.ds(pl.multiple_of(t * TOKEN_TILE_ROWS, TOKEN_TILE_ROWS), TOKEN_TILE_ROWS)]


def _load_token_tiles(ref, n):
    return jnp.concatenate([ref[pl.ds(c, n, stride=TOKEN_TILE_ROWS), :] for c in range(TOKEN_TILE_ROWS)], axis=1)


def _log_sigmoid(x):
    return jnp.minimum(x, 0.0) - jnp.log1p(jnp.exp(-jnp.abs(x)))


def _inproj_kernel(x_ref, xp_ref, xn_ref, g_ref, wm_ref, wg_ref, bg_ref, cw_ref,
                   qk_ref, v_ref, o_ref, gates_ref, qa_ref, ka_ref, va_ref, *, tm, seq_len):
    i = pl.program_id(0)
    g = g_ref[...]
    h = _rms(x_ref[...], g).astype(bf16)
    z = jnp.dot(h, wm_ref[...], preferred_element_type=f32)
    gates_ref[...] = jnp.dot(h, wg_ref[...], preferred_element_type=f32) + bg_ref[...]
    hh = _rms(jnp.concatenate([xp_ref[...], xn_ref[...]], axis=0), g).astype(bf16)
    zh = jnp.dot(hh, wm_ref[:, :2 * M_WIDTH], preferred_element_type=f32)
    start = lax.rem(i * tm, seq_len)
    prev_row = jnp.where(start == 0, 0.0, zh[SUBLANES - 1:SUBLANES, :])
    next_row = jnp.where(start + tm == seq_len, 0.0, zh[SUBLANES:SUBLANES + 1, :])
    u = z[:, :2 * M_WIDTH]
    rid = lax.broadcasted_iota(i32, (tm, 1), 0)
    u_prev = jnp.where(rid == 0, prev_row, pltpu.roll(u, 1, 0))
    u_next = jnp.where(rid == tm - 1, next_row, pltpu.roll(u, tm - 1, 0))
    cw = cw_ref[...]
    c = u_prev * cw[0:1, :] + u * cw[1:2, :] + u_next * cw[2:3, :]
    qk_ref[...] = (c * jax.nn.sigmoid(c)).astype(bf16)
    v_ref[...] = z[:, 2 * M_WIDTH:3 * M_WIDTH].astype(bf16)
    o_ref[...] = z[:, 3 * M_WIDTH:4 * M_WIDTH]
    base = 4 * M_WIDTH
    qa_ref[...] = z[:, base:base + A_WIDTH]
    ka_ref[...] = z[:, base + A_WIDTH:base + 2 * A_WIDTH]
    va_ref[...] = z[:, base + 2 * A_WIDTH:base + 3 * A_WIDTH].astype(bf16)


def _inproj(x2, g_mix, w_main, w_gate, b_gate, conv_w, seq_len):
    n = x2.shape[0]
    tm = TM_IN
    nb8 = n // SUBLANES
    r8 = tm // SUBLANES
    wcols = w_main.shape[1]
    row = lambda i: (i, 0)
    const = lambda i: (0, 0)
    out_shapes = (
        jax.ShapeDtypeStruct((n, 2 * M_WIDTH), bf16),
        jax.ShapeDtypeStruct((n, M_WIDTH), bf16),
        jax.ShapeDtypeStruct((n, M_WIDTH), f32),
        jax.ShapeDtypeStruct((n, LANES), f32),
        jax.ShapeDtypeStruct((n, A_WIDTH), f32),
        jax.ShapeDtypeStruct((n, A_WIDTH), f32),
        jax.ShapeDtypeStruct((n, A_WIDTH), bf16),
    )
    return pl.pallas_call(
        functools.partial(_inproj_kernel, tm=tm, seq_len=seq_len),
        grid=(n // tm,),
        in_specs=[
            pl.BlockSpec((tm, D_MODEL), row),
            pl.BlockSpec((SUBLANES, D_MODEL), lambda i: (jnp.maximum(i * r8 - 1, 0), 0)),
            pl.BlockSpec((SUBLANES, D_MODEL), lambda i: (jnp.minimum((i + 1) * r8, nb8 - 1), 0)),
            pl.BlockSpec((1, D_MODEL), const),
            pl.BlockSpec((D_MODEL, wcols), const),
            pl.BlockSpec((D_MODEL, LANES), const),
            pl.BlockSpec((1, LANES), const),
            pl.BlockSpec((3, 2 * M_WIDTH), const),
        ],
        out_specs=[
            pl.BlockSpec((tm, 2 * M_WIDTH), row),
            pl.BlockSpec((tm, M_WIDTH), row),
            pl.BlockSpec((tm, M_WIDTH), row),
            pl.BlockSpec((tm, LANES), row),
            pl.BlockSpec((tm, A_WIDTH), row),
            pl.BlockSpec((tm, A_WIDTH), row),
            pl.BlockSpec((tm, A_WIDTH), row),
        ],
        out_shape=out_shapes,
        compiler_params=_cparams(("parallel",)),
        name="inproj",
    )(x2, x2, x2, g_mix, w_main, w_gate, b_gate, conv_w)


def _mlstm_kernel(q_ref, k_ref, v_ref, g_ref, out_ref, c_ref, m_ref, *, L):
    d = pl.program_id(1)
    c = pl.program_id(2)

    @pl.when(c == 0)
    def _():
        c_ref[...] = jnp.zeros_like(c_ref)
        m_ref[...] = jnp.zeros_like(m_ref)

    fwd = d == 0
    sgn = jnp.where(fwd, 1, -1)
    row = lax.broadcasted_iota(i32, (L, L), 0)
    col = lax.broadcasted_iota(i32, (L, L), 1)
    mask = (row - col) * sgn >= 0
    t_col = mask.astype(bf16)

    g = g_ref[...]
    gd = jnp.where(fwd, g, pltpu.roll(g, LANES - 2 * M_HEADS, 1))
    lf = _log_sigmoid(pltpu.roll(gd, LANES - M_HEADS, 1))
    b = sum(jnp.dot(t_col, part, preferred_element_type=f32) for part in _split3(lf))
    bl = jnp.sum(lf, axis=0, keepdims=True)
    r = gd - b
    rid = lax.broadcasted_iota(i32, (L, 1), 0)
    cm = r
    sh = 1
    while sh < L:
        from_before = jnp.where(rid >= sh, pltpu.roll(cm, sh, 0), NEG_INF)
        from_after = jnp.where(rid < L - sh, pltpu.roll(cm, L - sh, 0), NEG_INF)
        cm = jnp.maximum(cm, jnp.where(fwd, from_before, from_after))
        sh *= 2
    m_prev = m_ref[0:1, :]
    mx = jnp.maximum(m_prev, cm)
    w_int = jnp.exp(m_prev - mx)
    e_col = jnp.exp(-(b + mx))
    m_new = bl + jnp.maximum(m_prev, jnp.max(r, axis=0, keepdims=True))
    wk = jnp.exp(bl + r - m_new)
    decay = jnp.exp(bl + m_prev - m_new)
    r_row = r.T
    m_ref[0:1, :] = m_new

    lane = lax.broadcasted_iota(i32, (1, LANES), 1)
    halves = (lane < HEAD_DIM, lane >= HEAD_DIM)
    nt = (((1,), (1,)), ((), ()))
    tn = (((0,), (0,)), ((), ()))

    for p in range(M_HEADS // 2):
        sl = slice(p * LANES, (p + 1) * LANES)
        q2 = q_ref[:, sl]
        k2 = k_ref[:, sl]
        v2 = v_ref[:, sl]
        nums, floors = [], []
        for half in range(2):
            h = 2 * p + half
            inh = halves[half]
            qz = jnp.where(inh, q2 * (HEAD_DIM ** -0.5), jnp.zeros_like(q2))
            kz = jnp.where(inh, k2, jnp.zeros_like(k2))
            va = jnp.where(inh, v2, jnp.ones_like(v2))
            arg = r_row[h:h + 1, :] - mx[:, h:h + 1]
            qk = lax.dot_general(qz, k2, nt, preferred_element_type=f32)
            s = jnp.where(mask, qk * jnp.exp(arg), 0.0).astype(bf16)
            cst = c_ref[h]
            nums.append(w_int[:, h:h + 1] * jnp.dot(qz, cst.astype(bf16), preferred_element_type=f32)
                        + jnp.dot(s, va, preferred_element_type=f32))
            floors.append(e_col[:, h:h + 1])
            kw = (kz.astype(f32) * wk[:, h:h + 1]).astype(bf16)
            c_ref[h] = decay[:, h:h + 1] * cst + lax.dot_general(kw, va, tn, preferred_element_type=f32)
        numer = jnp.where(halves[0], nums[0], nums[1])
        den = pltpu.roll(jnp.where(halves[0], nums[1], nums[0]), HEAD_DIM, 1)
        floor = jnp.where(halves[0], floors[0], floors[1])
        out_ref[:, sl] = numer / jnp.maximum(jnp.abs(den), floor)


def _mlstm(qk, v, gates, batch, seq_len):
    L = MLSTM_CHUNK
    nc = seq_len // L
    qk3 = qk.reshape(batch, seq_len, 2 * M_WIDTH)
    v3 = v.reshape(batch, seq_len, M_WIDTH)
    g3 = gates.reshape(batch, seq_len, LANES)
    cidx = lambda d, c: jnp.where(d == 0, c, nc - 1 - c)
    return pl.pallas_call(
        functools.partial(_mlstm_kernel, L=L),
        grid=(batch, 2, nc),
        in_specs=[
            pl.BlockSpec((None, L, M_WIDTH), lambda b, d, c: (b, cidx(d, c), 0)),
            pl.BlockSpec((None, L, M_WIDTH), lambda b, d, c: (b, cidx(d, c), 1)),
            pl.BlockSpec((None, L, M_WIDTH), lambda b, d, c: (b, cidx(d, c), 0)),
            pl.BlockSpec((None, L, LANES), lambda b, d, c: (b, cidx(d, c), 0)),
        ],
        out_specs=pl.BlockSpec((None, None, L, M_WIDTH), lambda b, d, c: (b, d, cidx(d, c), 0)),
        out_shape=jax.ShapeDtypeStruct((batch, 2, seq_len, M_WIDTH), f32),
        scratch_shapes=[pltpu.VMEM((M_HEADS, LANES, LANES), f32), pltpu.VMEM((M_HEADS, LANES), f32)],
        compiler_params=_cparams(("parallel", "parallel", "arbitrary")),
        name="mlstm",
    )(qk3, qk3, v3, g3)


def _natten_kernel(q_ref, k_ref, v_ref, bias_ref, gq_ref, gk_ref, out_ref, kn_ref, *, seq_len, rb_rows):
    rb = pl.program_id(2)
    rows = seq_len // GRID_W
    norm_rows = 512

    @pl.when(rb == 0)
    def _():
        def body(i, carry):
            sl = pl.ds(pl.multiple_of(i * norm_rows, norm_rows), norm_rows)
            kv = k_ref[sl, :]
            kn_ref[sl, :] = (kv * lax.rsqrt(_head_mean_sq(kv, LANES) + EPS) * gk_ref[...]).astype(bf16)
            return carry
        lax.fori_loop(0, seq_len // norm_rows, body, 0)

    qv = q_ref[...]
    qn = (qv * lax.rsqrt(_head_mean_sq(qv, LANES) + EPS) * gq_ref[...] * (HEAD_DIM ** -0.5)).astype(bf16)
    lane = lax.broadcasted_iota(i32, (1, LANES), 1)
    lo = lane < HEAD_DIM
    nt = (((1,), (1,)), ((), ()))
    pair = 2 * GRID_W

    def window(j):
        r = rb * rb_rows + j
        r0 = jnp.clip(r - WIN_H // 2, 0, rows - WIN_H)
        return r - r0, pl.ds(pl.multiple_of(r0 * GRID_W, GRID_W), WIN_H * GRID_W)

    tiles = []
    for j in range(rb_rows):
        delta, ks = window(j)
        qj = qn[j * GRID_W:(j + 1) * GRID_W, :]
        q2 = jnp.concatenate([jnp.where(lo, qj, jnp.zeros_like(qj)), jnp.where(lo, jnp.zeros_like(qj), qj)], axis=0)
        tiles.append(lax.dot_general(q2, kn_ref[ks, :], nt, preferred_element_type=f32) + bias_ref[delta])
    s = jnp.concatenate(tiles, axis=0)
    p = jnp.exp(s - jnp.max(s, axis=-1, keepdims=True))
    inv = 1.0 / jnp.sum(p, axis=-1, keepdims=True)
    pb = p.astype(bf16)
    for j in range(rb_rows):
        _, ks = window(j)
        o = jnp.dot(pb[j * pair:(j + 1) * pair, :], v_ref[ks, :], preferred_element_type=f32)
        o = o * inv[j * pair:(j + 1) * pair, :]
        out_ref[j * GRID_W:(j + 1) * GRID_W, :] = jnp.where(lo, o[:GRID_W, :], o[GRID_W:, :])


def _natten_bias_table(rel_bias):
    cq = jnp.arange(GRID_W)[:, None]
    ck = jnp.arange(GRID_W)[None, :]
    c0 = jnp.clip(cq - WIN_W // 2, 0, GRID_W - WIN_W)
    col_in = (ck >= c0) & (ck < c0 + WIN_W)
    idx_c = jnp.clip(ck - cq, -(WIN_W - 1), WIN_W - 1) + (WIN_W - 1)
    pick = idx_c[:, :, None] == jnp.arange(2 * WIN_W - 1)
    tz = jnp.sum(jnp.where(pick[None, None], rel_bias.astype(f32)[:, :, None, None, :], 0.0), axis=-1)
    tz = jnp.where(col_in[None, None], tz, NEG_INF)
    tab = jnp.stack([tz[:, WIN_H - 1 - dl:2 * WIN_H - 1 - dl] for dl in range(WIN_H)], axis=1)
    tab = tab.transpose(0, 1, 3, 2, 4).reshape(A_HEADS // 2, 2, WIN_H, GRID_W, WIN_H * GRID_W)
    return tab.transpose(0, 2, 1, 3, 4).reshape(A_HEADS // 2, WIN_H, 2 * GRID_W, WIN_H * GRID_W)


def _natten(qa, ka, va, bias_tab, g_q2, g_k2, batch, seq_len):
    rows = seq_len // GRID_W
    rbr = NAT_ROWS
    tq = rbr * GRID_W
    q3 = qa.reshape(batch, seq_len, A_WIDTH)
    k3 = ka.reshape(batch, seq_len, A_WIDTH)
    v3 = va.reshape(batch, seq_len, A_WIDTH)
    return pl.pallas_call(
        functools.partial(_natten_kernel, seq_len=seq_len, rb_rows=rbr),
        grid=(batch, A_HEADS // 2, rows // rbr),
        in_specs=[
            pl.BlockSpec((None, tq, LANES), lambda b, hp, rb: (b, rb, hp)),
            pl.BlockSpec((None, seq_len, LANES), lambda b, hp, rb: (b, 0, hp)),
            pl.BlockSpec((None, seq_len, LANES), lambda b, hp, rb: (b, 0, hp)),
            pl.BlockSpec((None, WIN_H, 2 * GRID_W, WIN_H * GRID_W), lambda b, hp, rb: (hp, 0, 0, 0)),
            pl.BlockSpec((1, LANES), lambda b, hp, rb: (0, 0)),
            pl.BlockSpec((1, LANES), lambda b, hp, rb: (0, 0)),
        ],
        out_specs=pl.BlockSpec((None, tq, LANES), lambda b, hp, rb: (b, rb, hp)),
        out_shape=jax.ShapeDtypeStruct((batch, seq_len, A_WIDTH), f32),
        scratch_shapes=[pltpu.VMEM((seq_len, LANES), bf16)],
        compiler_params=_cparams(("parallel", "parallel", "arbitrary")),
        name="natten",
    )(q3, k3, v3, bias_tab, g_q2, g_k2)


def _outproj_kernel(x_ref, hf_ref, hb_ref, o_ref, ha_ref, gm_ref, ga_ref, wo_ref, gf_ref,
                    wr1_ref, wr2_ref, br_ref,
                    xmid_ref, hn_ref, ids_ref, pos_ref, gate_ref, cnt_ref, carry_ref, *, tm):
    i = pl.program_id(0)

    @pl.when(i == 0)
    def _():
        carry_ref[...] = jnp.zeros_like(carry_ref)

    hm = hf_ref[...] + hb_ref[...]
    hm = hm * lax.rsqrt(_head_mean_sq(hm, M_WIDTH) + EPS) * gm_ref[...] * jax.nn.sigmoid(o_ref[...])
    ha = _rms(ha_ref[...], ga_ref[...])
    mix = (jnp.dot(hm.astype(bf16), wo_ref[0:M_WIDTH, :], preferred_element_type=f32)
           + jnp.dot(ha.astype(bf16), wo_ref[M_WIDTH:M_WIDTH + A_WIDTH, :], preferred_element_type=f32))
    xm = x_ref[...] + mix
    xmid_ref[...] = xm
    hn = _rms(xm, gf_ref[...])
    _store_token_tiles(hn_ref, hn)

    h1, h2 = _split2(hn)
    logits = (jnp.dot(h1, wr1_ref[...], preferred_element_type=f32)
              + (jnp.dot(h1, wr2_ref[...], preferred_element_type=f32)
                 + jnp.dot(h2, wr1_ref[...], preferred_element_type=f32))) + br_ref[...]
    lane = lax.broadcasted_iota(i32, (tm, LANES), 1)
    work = logits
    vals, idxs, sels = [], [], []
    for _ in range(TOP_K):
        mx = jnp.max(work, axis=-1, keepdims=True)
        idx = jnp.min(jnp.where(work == mx, lane, LANES), axis=-1, keepdims=True)
        sel = lane == idx
        vals.append(mx)
        idxs.append(idx)
        sels.append(sel)
        work = jnp.where(sel, NEG_INF, work)
    es = [jnp.exp(v - vals[0]) for v in vals]
    tot = es[0] + es[1] + es[2] + es[3]

    onehot = jnp.where(sels[0] | sels[1] | sels[2] | sels[3], 1.0, 0.0)
    tri = (lax.broadcasted_iota(i32, (tm, tm), 0) > lax.broadcasted_iota(i32, (tm, tm), 1)).astype(bf16)
    base = jnp.dot(tri, onehot.astype(bf16), preferred_element_type=f32) + carry_ref[...]
    ids_out = jnp.zeros((tm, LANES), i32)
    pos_out = jnp.zeros((tm, LANES), i32)
    gate_out = jnp.zeros((tm, LANES), f32)
    for k in range(TOP_K):
        pk = jnp.sum(jnp.where(sels[k], base, 0.0), axis=-1, keepdims=True).astype(i32)
        ids_out = jnp.where(lane == k, idxs[k], ids_out)
        pos_out = jnp.where(lane == k, pk, pos_out)
        gate_out = jnp.where(lane == k, es[k] / tot, gate_out)
    ids_ref[...] = ids_out[:, :TOP_K]
    pos_ref[...] = pos_out[:, :TOP_K]
    gate_ref[...] = gate_out[:, :TOP_K]
    carry_ref[...] += jnp.sum(onehot, axis=0, keepdims=True)
    cnt_ref[...] = carry_ref[...]


def _outproj(x2, hfb, o_m, ha, g_m, g_a, w_out, g_ffn, wr1, wr2, b_r, batch, seq_len):
    n = x2.shape[0]
    tm = TM_OUT
    tpb = seq_len // tm
    row = lambda i: (i, 0)
    const = lambda i: (0, 0)
    ha2 = ha.reshape(n, A_WIDTH)
    return pl.pallas_call(
        functools.partial(_outproj_kernel, tm=tm),
        grid=(n // tm,),
        in_specs=[
            pl.BlockSpec((tm, D_MODEL), row),
            pl.BlockSpec((None, None, tm, M_WIDTH), lambda i: (i // tpb, 0, i % tpb, 0)),
            pl.BlockSpec((None, None, tm, M_WIDTH), lambda i: (i // tpb, 1, i % tpb, 0)),
            pl.BlockSpec((tm, M_WIDTH), row),
            pl.BlockSpec((tm, A_WIDTH), row),
            pl.BlockSpec((1, M_WIDTH), const),
            pl.BlockSpec((1, A_WIDTH), const),
            pl.BlockSpec((M_WIDTH + A_WIDTH, D_MODEL), const),
            pl.BlockSpec((1, D_MODEL), const),
            pl.BlockSpec((D_MODEL, LANES), const),
            pl.BlockSpec((D_MODEL, LANES), const),
            pl.BlockSpec((1, LANES), const),
        ],
        out_specs=[
            pl.BlockSpec((tm, D_MODEL), row),
            pl.BlockSpec((tm * TOKEN_TILE_ROWS, LANES), row),
            pl.BlockSpec((tm, TOP_K), row),
            pl.BlockSpec((tm, TOP_K), row),
            pl.BlockSpec((tm, TOP_K), row),
            pl.BlockSpec((1, LANES), const),
        ],
        out_shape=(
            jax.ShapeDtypeStruct((n, D_MODEL), f32),
            jax.ShapeDtypeStruct((n * TOKEN_TILE_ROWS, LANES), f32),
            jax.ShapeDtypeStruct((n, TOP_K), i32),
            jax.ShapeDtypeStruct((n, TOP_K), i32),
            jax.ShapeDtypeStruct((n, TOP_K), f32),
            jax.ShapeDtypeStruct((1, LANES), f32),
        ),
        scratch_shapes=[pltpu.VMEM((1, LANES), f32)],
        compiler_params=_cparams(("arbitrary",)),
        name="outproj_router",
    )(x2, hfb, hfb, o_m, ha2, g_m, g_a, w_out, g_ffn, wr1, wr2, b_r)


def _dispatch_kernel(pend_ref, padded_ref, slot_ref, hn_ref, xb_hbm, zero_ref, sem, *, tm, tm_e, n_blocks):
    i = pl.program_id(0)

    blk = tm_e * TOKEN_TILE_ROWS

    def zero_copy(e):
        start = (pend_ref[e] - tm_e) * TOKEN_TILE_ROWS
        return pltpu.make_async_copy(zero_ref, xb_hbm.at[pl.ds(pl.multiple_of(start, blk), blk)], sem)

    def tail_copy(b):
        return pltpu.make_async_copy(zero_ref, xb_hbm.at[pl.ds(pl.multiple_of(b * blk, blk), blk)], sem)

    @pl.when(i == 0)
    def _():
        zero_ref[...] = jnp.zeros_like(zero_ref)
        for e in range(N_EXPERTS):
            @pl.when(padded_ref[e] > 0)
            def _():
                zero_copy(e).start()
        used = pend_ref[N_EXPERTS - 1] // tm_e

        def tail_start(b, carry):
            tail_copy(b).start()
            return carry

        def tail_wait(b, carry):
            tail_copy(b).wait()
            return carry

        lax.fori_loop(used, n_blocks, tail_start, 0)
        for e in range(N_EXPERTS):
            @pl.when(padded_ref[e] > 0)
            def _():
                zero_copy(e).wait()
        lax.fori_loop(used, n_blocks, tail_wait, 0)

    def row(j, k):
        return pltpu.make_async_copy(_token_tile(hn_ref, j), _token_tile(xb_hbm, slot_ref[0, 0, j * TOP_K + k]), sem)

    def issue(j, carry):
        for k in range(TOP_K):
            row(j, k).start(priority=k % 2)
        return carry

    lax.fori_loop(0, tm, issue, 0, unroll=4)

    def drain(j, carry):
        for k in range(TOP_K):
            row(0, k).wait()
        return carry

    lax.fori_loop(0, tm, drain, 0, unroll=4)


def _dispatch(hn8, slot_flat, pend, padded, cap):
    n = hn8.shape[0] // TOKEN_TILE_ROWS
    tm = TM_DISPATCH
    slot3 = slot_flat.reshape(n // tm, 1, tm * TOP_K)
    grid_spec = pltpu.PrefetchScalarGridSpec(
        num_scalar_prefetch=2,
        grid=(n // tm,),
        in_specs=[
            pl.BlockSpec((1, 1, tm * TOP_K), lambda i, pe, pa: (i, 0, 0), memory_space=pltpu.SMEM),
            pl.BlockSpec((tm * TOKEN_TILE_ROWS, LANES), lambda i, pe, pa: (i, 0)),
        ],
        out_specs=pl.BlockSpec(memory_space=pl.ANY),
        scratch_shapes=[pltpu.VMEM((TM_EXP * TOKEN_TILE_ROWS, LANES), f32), pltpu.SemaphoreType.DMA(())],
    )
    return pl.pallas_call(
        functools.partial(_dispatch_kernel, tm=tm, tm_e=TM_EXP, n_blocks=cap // TM_EXP),
        grid_spec=grid_spec,
        out_shape=jax.ShapeDtypeStruct((cap * TOKEN_TILE_ROWS, LANES), f32),
        compiler_params=_cparams(("arbitrary",)),
        name="dispatch",
    )(pend, padded, slot3, hn8)


def _experts_kernel(be_ref, nv_ref, xb_ref, wgu_ref, bgu_ref, wd_ref, bd_ref, yb_ref, *, tm):
    j = pl.program_id(0)

    @pl.when(j < nv_ref[0])
    def _():
        xv = _load_token_tiles(xb_ref, tm).astype(bf16)
        h = jnp.dot(xv, wgu_ref[...], preferred_element_type=f32) + bgu_ref[...]
        gt = jnp.minimum(h[:, :D_FF], SWIGLU_LIMIT)
        up = jnp.clip(h[:, D_FF:], -SWIGLU_LIMIT, SWIGLU_LIMIT)
        act = (up + 1.0) * (gt * jax.nn.sigmoid(SWIGLU_ALPHA * gt))
        _store_token_tiles(yb_ref, jnp.dot(act.astype(bf16), wd_ref[...], preferred_element_type=f32) + bd_ref[...])

    @pl.when(j >= nv_ref[0])
    def _():
        yb_ref[...] = jnp.zeros_like(yb_ref)


def _experts(xb, block_e, nvalid, w_gu, b_gu, w_d, b_d):
    cap = xb.shape[0] // TOKEN_TILE_ROWS
    tm = TM_EXP
    grid_spec = pltpu.PrefetchScalarGridSpec(
        num_scalar_prefetch=2,
        grid=(cap // tm,),
        in_specs=[
            pl.BlockSpec((tm * TOKEN_TILE_ROWS, LANES), lambda j, be, nv: (jnp.minimum(j, nv[0] - 1), 0)),
            pl.BlockSpec((None, D_MODEL, 2 * D_FF), lambda j, be, nv: (be[j], 0, 0)),
            pl.BlockSpec((None, 1, 2 * D_FF), lambda j, be, nv: (be[j], 0, 0)),
            pl.BlockSpec((None, D_FF, D_MODEL), lambda j, be, nv: (be[j], 0, 0)),
            pl.BlockSpec((None, 1, D_MODEL), lambda j, be, nv: (be[j], 0, 0)),
        ],
        out_specs=pl.BlockSpec((tm * TOKEN_TILE_ROWS, LANES), lambda j, be, nv: (j, 0)),
    )
    return pl.pallas_call(
        functools.partial(_experts_kernel, tm=tm),
        grid_spec=grid_spec,
        out_shape=jax.ShapeDtypeStruct((cap * TOKEN_TILE_ROWS, LANES), f32),
        compiler_params=_cparams(("arbitrary",)),
        name="experts",
    )(block_e, nvalid, xb, w_gu, b_gu, w_d, b_d)


def _combine_kernel(slot_ref, xmid_ref, gate_ref, p_ref, wproj_ref, gpost_ref, gple_ref, wgate_ref, yb_hbm,
                    out_ref, ybuf_ref, sem, *, tm):
    def row(j, k):
        return pltpu.make_async_copy(_token_tile(yb_hbm, slot_ref[0, 0, j * TOP_K + k]),
                                     _token_tile(ybuf_ref.at[k], j), sem)

    def issue(j, carry):
        for k in range(TOP_K):
            row(j, k).start(priority=k % 2)
        return carry

    lax.fori_loop(0, tm, issue, 0, unroll=4)

    pe = _rms(jnp.dot(p_ref[...].astype(bf16), wproj_ref[...], preferred_element_type=f32), gpost_ref[...])

    def drain(j, carry):
        for k in range(TOP_K):
            row(0, k).wait()
        return carry

    lax.fori_loop(0, tm, drain, 0, unroll=4)

    gate = gate_ref[...]
    y = gate[:, 0:1] * _load_token_tiles(ybuf_ref.at[0], tm)
    for k in range(1, TOP_K):
        y = y + gate[:, k:k + 1] * _load_token_tiles(ybuf_ref.at[k], tm)
    x2 = xmid_ref[...] + y
    gl = jnp.dot(_rms(x2, gple_ref[...]).astype(bf16), wgate_ref[...], preferred_element_type=f32)
    out_ref[...] = x2 + jax.nn.sigmoid(gl) * pe


def _combine(slot_flat, xmid, gate, p2, w_proj, g_post, g_ple, w_gate, yb):
    n = xmid.shape[0]
    tm = TM_CMB
    slot3 = slot_flat.reshape(n // tm, 1, tm * TOP_K)
    row = lambda i: (i, 0)
    const = lambda i: (0, 0)
    return pl.pallas_call(
        functools.partial(_combine_kernel, tm=tm),
        grid=(n // tm,),
        in_specs=[
            pl.BlockSpec((1, 1, tm * TOP_K), lambda i: (i, 0, 0), memory_space=pltpu.SMEM),
            pl.BlockSpec((tm, D_MODEL), row),
            pl.BlockSpec((tm, TOP_K), row),
            pl.BlockSpec((tm, PLE_DIM), row),
            pl.BlockSpec((PLE_DIM, D_MODEL), const),
            pl.BlockSpec((1, D_MODEL), const),
            pl.BlockSpec((1, D_MODEL), const),
            pl.BlockSpec((D_MODEL, D_MODEL), const),
            pl.BlockSpec(memory_space=pl.ANY),
        ],
        out_specs=pl.BlockSpec((tm, D_MODEL), row),
        out_shape=jax.ShapeDtypeStruct((n, D_MODEL), f32),
        scratch_shapes=[pltpu.VMEM((TOP_K, tm * TOKEN_TILE_ROWS, LANES), f32), pltpu.SemaphoreType.DMA(())],
        compiler_params=_cparams(("arbitrary",)),
        name="combine_ple",
    )(slot3, xmid, gate, p2, w_proj, g_post, g_ple, w_gate, yb)


def _wprep_kernel(w_ref, out_ref):
    grp = 2 * LANES
    src = lax.broadcasted_iota(i32, (grp, grp), 0)
    dst = lax.broadcasted_iota(i32, (grp, grp), 1)
    want = jnp.where(dst < LANES, 2 * dst, 2 * (dst - LANES) + 1)
    perm = (src == want).astype(bf16)
    for g in range(2 * D_FF // grp):
        t = jnp.dot(w_ref[:, g * grp:(g + 1) * grp].astype(bf16), perm, preferred_element_type=f32)
        out_ref[:, g * LANES:(g + 1) * LANES] = t[:, :LANES].astype(bf16)
        out_ref[:, D_FF + g * LANES:D_FF + (g + 1) * LANES] = t[:, LANES:].astype(bf16)


def _wprep(w_gate_up):
    ne = w_gate_up.shape[0]
    tr = TR_WPREP
    return pl.pallas_call(
        _wprep_kernel,
        grid=(ne, D_MODEL // tr),
        in_specs=[pl.BlockSpec((None, tr, 2 * D_FF), lambda e, r: (e, r, 0))],
        out_specs=pl.BlockSpec((None, tr, 2 * D_FF), lambda e, r: (e, r, 0)),
        out_shape=jax.ShapeDtypeStruct((ne, D_MODEL, 2 * D_FF), bf16),
        compiler_params=_cparams(("parallel", "parallel")),
        name="wprep",
    )(w_gate_up)


def _prep_weights(g_mix, w_in, b_gates, conv_w, g_m_head, g_q, g_k, rel_bias, g_a_out, w_out, g_ffn,
                  w_router, b_router, w_gate_up, b_gate_up, w_down, b_down, g_ple, w_ple_gate, w_ple_proj,
                  g_ple_post):
    g0 = 4 * M_WIDTH
    w = w_in[0]
    pw = {}
    pw["g_mix"] = g_mix[0][None, :]
    pw["w_main"] = jnp.concatenate([w[:, :g0], w[:, g0 + N_GATES:]], axis=1).astype(bf16)
    pw["w_gate"] = jnp.pad(w[:, g0:g0 + N_GATES], ((0, 0), (0, LANES - N_GATES))).astype(bf16)
    pw["b_gate"] = jnp.pad(b_gates[0], (0, LANES - N_GATES))[None, :]
    pw["conv_w"] = conv_w[0]
    pw["g_m"] = g_m_head[0].reshape(1, M_WIDTH)
    pw["g_q2"] = jnp.tile(g_q[0], 2)[None, :]
    pw["g_k2"] = jnp.tile(g_k[0], 2)[None, :]
    pw["bias_tab"] = _natten_bias_table(rel_bias[0])
    pw["g_a"] = g_a_out[0][None, :]
    pw["w_out"] = w_out[0].astype(bf16)
    pw["g_ffn"] = g_ffn[0][None, :]
    wr = jnp.pad(w_router[0], ((0, 0), (0, LANES - N_EXPERTS)))
    wr1 = wr.astype(bf16)
    pw["wr1"] = wr1
    pw["wr2"] = (wr - wr1.astype(f32)).astype(bf16)
    pw["b_r"] = jnp.pad(b_router[0], (0, LANES - N_EXPERTS), constant_values=NEG_INF)[None, :]
    pw["w_gu"] = _wprep(w_gate_up[0])
    bgu = b_gate_up[0]
    pw["b_gu"] = jnp.concatenate([bgu[:, 0::2], bgu[:, 1::2]], axis=-1)[:, None, :]
    pw["w_d"] = w_down[0].astype(bf16)
    pw["b_d"] = b_down[0][:, None, :]
    pw["g_ple"] = g_ple[0][None, :]
    pw["w_ple_gate"] = w_ple_gate[0].astype(bf16)
    pw["w_ple_proj"] = w_ple_proj[0].astype(bf16)
    pw["g_ple_post"] = g_ple_post[0][None, :]
    return pw


def _trunk(x, p, pw):
    batch, seq_len, _ = x.shape
    n = batch * seq_len
    x2 = x.reshape(n, D_MODEL)
    qk, v_m, o_m, gates, qa, ka, va = _inproj(x2, pw["g_mix"], pw["w_main"], pw["w_gate"], pw["b_gate"],
                                              pw["conv_w"], seq_len)
    hfb = _mlstm(qk, v_m, gates, batch, seq_len)
    ha = _natten(qa, ka, va, pw["bias_tab"], pw["g_q2"], pw["g_k2"], batch, seq_len)
    xmid, hn, ids, pos, gate, cnt = _outproj(x2, hfb, o_m, ha, pw["g_m"], pw["g_a"], pw["w_out"], pw["g_ffn"],
                                             pw["wr1"], pw["wr2"], pw["b_r"], batch, seq_len)

    tm_e = TM_EXP
    counts = cnt[0, :N_EXPERTS].astype(i32)
    padded = (counts + tm_e - 1) // tm_e * tm_e
    pend = jnp.cumsum(padded).astype(i32)
    pstart = pend - padded
    eid = jnp.arange(N_EXPERTS, dtype=i32)
    slot = (jnp.sum(jnp.where(ids[:, :, None] == eid, pstart, 0), axis=-1) + pos).reshape(-1)
    nk = n * TOP_K
    n_blocks = (nk + N_EXPERTS * (tm_e - 1) + tm_e - 1) // tm_e
    cap = n_blocks * tm_e
    block_start = jnp.arange(n_blocks, dtype=i32) * tm_e
    block_e = jnp.minimum(jnp.sum((pend[None, :] <= block_start[:, None]).astype(i32), axis=1), N_EXPERTS - 1)
    nvalid = (pend[-1:] // tm_e).astype(i32)

    xb = _dispatch(hn, slot, pend, padded, cap)
    yb = _experts(xb, block_e, nvalid, pw["w_gu"], pw["b_gu"], pw["w_d"], pw["b_d"])
    out = _combine(slot, xmid, gate, p.reshape(n, PLE_DIM), pw["w_ple_proj"], pw["g_ple_post"], pw["g_ple"],
                   pw["w_ple_gate"], yb)
    return out.reshape(batch, seq_len, D_MODEL)


def kernel(x_prompt, x_sample, p_prompt, p_sample, g_mix, w_in, b_gates, conv_w, g_m_head, g_q, g_k, rel_bias,
           g_a_out, w_out, g_ffn, w_router, b_router, w_gate_up, b_gate_up, w_down, b_down, g_ple, w_ple_gate,
           w_ple_proj, g_ple_post):
    assert w_in.shape[0] == 1, "single-layer trunk"
    pw = _prep_weights(g_mix, w_in, b_gates, conv_w, g_m_head, g_q, g_k, rel_bias, g_a_out, w_out, g_ffn,
                       w_router, b_router, w_gate_up, b_gate_up, w_down, b_down, g_ple, w_ple_gate, w_ple_proj,
                       g_ple_post)
    y_prompt = _trunk(x_prompt, p_prompt[0], pw)
    y_sample = _trunk(x_sample, p_sample[0], pw)
    return (y_prompt, y_sample)
```

```python
import functools

import jax
import jax.numpy as jnp
from jax import lax
from jax.experimental import pallas as pl
from jax.experimental.pallas import tpu as pltpu

f32 = jnp.float32
bf16 = jnp.bfloat16
i32 = jnp.int32

D_MODEL = 1024
HEAD_DIM = 64
M_HEADS = 8
A_HEADS = 8
M_WIDTH = M_HEADS * HEAD_DIM
A_WIDTH = A_HEADS * HEAD_DIM
N_GATES = 4 * M_HEADS
GRID_W = 64
WIN_H = 8
WIN_W = 16
N_EXPERTS = 32
TOP_K = 4
D_FF = 1024
SWIGLU_LIMIT = 7.0
SWIGLU_ALPHA = 1.702
PLE_DIM = 256
EPS = 1e-6

LANES = 128
SUBLANES = 8
VMEM_LIMIT = 56 * 1024 * 1024

TM_IN = 512
MLSTM_CHUNK = 256
NAT_ROWS = 8
TM_OUT = 512
TM_EXP = 512
TM_DISPATCH = 256
TR_WPREP = 512
TM_CMB = 256

NEG_INF = float("-inf")


def _cparams(sem):
    return pltpu.CompilerParams(dimension_semantics=sem, vmem_limit_bytes=VMEM_LIMIT)


def _rms(xv, g):
    return xv * lax.rsqrt(jnp.mean(xv * xv, axis=-1, keepdims=True) + EPS) * g


def _split2(a):
    hi = a.astype(bf16)
    lo = (a - hi.astype(f32)).astype(bf16)
    return hi, lo


def _split3(a):
    hi = a.astype(bf16)
    r = a - hi.astype(f32)
    mid = r.astype(bf16)
    lo = (r - mid.astype(f32)).astype(bf16)
    return hi, mid, lo


def _head_mean_sq(xv, width):
    a = lax.broadcasted_iota(i32, (width, width), 0) // HEAD_DIM
    b = lax.broadcasted_iota(i32, (width, width), 1) // HEAD_DIM
    bd = jnp.where(a == b, 1.0 / HEAD_DIM, 0.0).astype(bf16)
    hi, lo = _split2(xv * xv)
    return (jnp.dot(hi, bd, preferred_element_type=f32) + jnp.dot(lo, bd, preferred_element_type=f32))


TOKEN_TILE_ROWS = D_MODEL // LANES


def _store_token_tiles(ref, val):
    n = val.shape[0]
    for c in range(TOKEN_TILE_ROWS):
        ref[pl.ds(c, n, stride=TOKEN_TILE_ROWS), :] = val[:, c * LANES:(c + 1) * LANES]


def _token_tile(ref, t):
    return ref.at[pl.ds(pl.multiple_of(t * TOKEN_TILE_ROWS, TOKEN_TILE_ROWS), TOKEN_TILE_ROWS)]


def _load_token_tiles(ref, n):
    return jnp.concatenate([ref[pl.ds(c, n, stride=TOKEN_TILE_ROWS), :] for c in range(TOKEN_TILE_ROWS)], axis=1)


def _log_sigmoid(x):
    return jnp.minimum(x, 0.0) - jnp.log1p(jnp.exp(-jnp.abs(x)))


def _inproj_kernel(x_ref, xp_ref, xn_ref, g_ref, wm_ref, wg_ref, bg_ref, cw_ref,
                   qk_ref, v_ref, o_ref, gates_ref, qa_ref, ka_ref, va_ref, *, tm, seq_len):
    i = pl.program_id(0)
    g = g_ref[...]
    h = _rms(x_ref[...], g).astype(bf16)
    z = jnp.dot(h, wm_ref[...], preferred_element_type=f32)
    zg = jnp.dot(h, wg_ref[...], preferred_element_type=f32) + bg_ref[...]
    gates_ref[...] = zg.T[:N_GATES, :]
    hh = _rms(jnp.concatenate([xp_ref[...], xn_ref[...]], axis=0), g).astype(bf16)
    zh = jnp.dot(hh, wm_ref[:, :2 * M_WIDTH], preferred_element_type=f32)
    start = lax.rem(i * tm, seq_len)
    prev_row = jnp.where(start == 0, 0.0, zh[SUBLANES - 1:SUBLANES, :])
    next_row = jnp.where(start + tm == seq_len, 0.0, zh[SUBLANES:SUBLANES + 1, :])
    u = z[:, :2 * M_WIDTH]
    rid = lax.broadcasted_iota(i32, (tm, 1), 0)
    u_prev = jnp.where(rid == 0, prev_row, pltpu.roll(u, 1, 0))
    u_next = jnp.where(rid == tm - 1, next_row, pltpu.roll(u, tm - 1, 0))
    cw = cw_ref[...]
    c = u_prev * cw[0:1, :] + u * cw[1:2, :] + u_next * cw[2:3, :]
    qk_ref[...] = (c * jax.nn.sigmoid(c)).astype(bf16)
    v_ref[...] = z[:, 2 * M_WIDTH:3 * M_WIDTH].T.astype(bf16)
    o_ref[...] = z[:, 3 * M_WIDTH:4 * M_WIDTH]
    base = 4 * M_WIDTH
    qa_ref[...] = z[:, base:base + A_WIDTH]
    ka_ref[...] = z[:, base + A_WIDTH:base + 2 * A_WIDTH]
    va_ref[...] = z[:, base + 2 * A_WIDTH:base + 3 * A_WIDTH].astype(bf16)


def _inproj(x2, g_mix, w_main, w_gate, b_gate, conv_w, seq_len):
    n = x2.shape[0]
    tm = TM_IN
    nb8 = n // SUBLANES
    r8 = tm // SUBLANES
    wcols = w_main.shape[1]
    row = lambda i: (i, 0)
    const = lambda i: (0, 0)
    out_shapes = (
        jax.ShapeDtypeStruct((n, 2 * M_WIDTH), bf16),
        jax.ShapeDtypeStruct((M_WIDTH, n), bf16),
        jax.ShapeDtypeStruct((n, M_WIDTH), f32),
        jax.ShapeDtypeStruct((N_GATES, n), f32),
        jax.ShapeDtypeStruct((n, A_WIDTH), f32),
        jax.ShapeDtypeStruct((n, A_WIDTH), f32),
        jax.ShapeDtypeStruct((n, A_WIDTH), bf16),
    )
    return pl.pallas_call(
        functools.partial(_inproj_kernel, tm=tm, seq_len=seq_len),
        grid=(n // tm,),
        in_specs=[
            pl.BlockSpec((tm, D_MODEL), row),
            pl.BlockSpec((SUBLANES, D_MODEL), lambda i: (jnp.maximum(i * r8 - 1, 0), 0)),
            pl.BlockSpec((SUBLANES, D_MODEL), lambda i: (jnp.minimum((i + 1) * r8, nb8 - 1), 0)),
            pl.BlockSpec((1, D_MODEL), const),
            pl.BlockSpec((D_MODEL, wcols), const),
            pl.BlockSpec((D_MODEL, LANES), const),
            pl.BlockSpec((1, LANES), const),
            pl.BlockSpec((3, 2 * M_WIDTH), const),
        ],
        out_specs=[
            pl.BlockSpec((tm, 2 * M_WIDTH), row),
            pl.BlockSpec((M_WIDTH, tm), lambda i: (0, i)),
            pl.BlockSpec((tm, M_WIDTH), row),
            pl.BlockSpec((N_GATES, tm), lambda i: (0, i)),
            pl.BlockSpec((tm, A_WIDTH), row),
            pl.BlockSpec((tm, A_WIDTH), row),
            pl.BlockSpec((tm, A_WIDTH), row),
        ],
        out_shape=out_shapes,
        compiler_params=_cparams(("parallel",)),
        name="inproj",
    )(x2, x2, x2, g_mix, w_main, w_gate, b_gate, conv_w)


def _mlstm_kernel(q_ref, k_ref, vt_ref, g_ref, out_ref, c_ref, m_ref, *, L):
    d = pl.program_id(1)
    c = pl.program_id(2)

    @pl.when(c == 0)
    def _():
        c_ref[...] = jnp.zeros_like(c_ref)
        m_ref[...] = jnp.zeros_like(m_ref)

    fwd = d == 0
    sgn = jnp.where(fwd, 1, -1)
    row = lax.broadcasted_iota(i32, (L, L), 0)
    col = lax.broadcasted_iota(i32, (L, L), 1)
    vis = (col - row) * sgn >= 0
    t_row = vis.astype(bf16)

    g = g_ref[...]
    h8 = M_HEADS
    li = jnp.where(fwd, g[0:h8, :], g[2 * h8:3 * h8, :])
    lf = _log_sigmoid(jnp.where(fwd, g[h8:2 * h8, :], g[3 * h8:4 * h8, :]))
    b = sum(jnp.dot(part, t_row, preferred_element_type=f32) for part in _split3(lf))
    bl = jnp.sum(lf, axis=1, keepdims=True)
    r_row = li - b
    pos = lax.broadcasted_iota(i32, (1, L), 1)
    cm = r_row
    sh = 1
    while sh < L:
        from_before = jnp.where(pos >= sh, pltpu.roll(cm, sh, 1), NEG_INF)
        from_after = jnp.where(pos < L - sh, pltpu.roll(cm, L - sh, 1), NEG_INF)
        cm = jnp.maximum(cm, jnp.where(fwd, from_before, from_after))
        sh *= 2
    m_prev = m_ref[:, 0:1]
    mx_r = jnp.maximum(m_prev, cm)
    m_new = bl + jnp.maximum(m_prev, jnp.max(r_row, axis=1, keepdims=True))
    decay = jnp.exp(bl + m_prev - m_new)
    m_ref[...] = jnp.broadcast_to(m_new, (h8, LANES))
    w_int = jnp.exp(m_prev - mx_r)
    floor = jnp.exp(-(b + mx_r))
    wk = jnp.exp(bl + r_row - m_new)
    r_col = jnp.concatenate([r_row, jnp.zeros((LANES - h8, L), f32)], axis=0).T

    lane = lax.broadcasted_iota(i32, (1, LANES), 1)
    sub = lax.broadcasted_iota(i32, (LANES, 1), 0)
    lane_half = (lane < HEAD_DIM, lane >= HEAD_DIM)
    sub_half = (sub < HEAD_DIM, sub >= HEAD_DIM)
    nt = (((1,), (1,)), ((), ()))

    heads = range(M_HEADS)
    pair = lambda h: slice((h // 2) * LANES, (h // 2 + 1) * LANES)
    qz = [jnp.where(lane_half[h % 2], q_ref[:, pair(h)] * (HEAD_DIM ** -0.5), 0).astype(bf16) for h in heads]
    kz = [jnp.where(lane_half[h % 2], k_ref[:, pair(h)], 0).astype(bf16) for h in heads]
    va_t = [jnp.where(sub_half[h % 2], vt_ref[pair(h), :], 1).astype(bf16) for h in heads]
    kq = [lax.dot_general(kz[h], qz[h], nt, preferred_element_type=f32) for h in heads]
    s_t = [jnp.where(vis, kq[h] * jnp.exp(r_col[:, h:h + 1] - mx_r[h:h + 1, :]), 0.0).astype(bf16) for h in heads]
    cst = [c_ref[h] for h in heads]
    nums = [w_int[h:h + 1, :] * lax.dot_general(cst[h].astype(bf16), qz[h], nt, preferred_element_type=f32)
            + jnp.dot(va_t[h], s_t[h], preferred_element_type=f32) for h in heads]
    for h in heads:
        vw = (va_t[h].astype(f32) * wk[h:h + 1, :]).astype(bf16)
        c_ref[h] = decay[h:h + 1, :] * cst[h] + jnp.dot(vw, kz[h], preferred_element_type=f32)
    for p in range(M_HEADS // 2):
        ev, od = 2 * p, 2 * p + 1
        numer = jnp.where(sub_half[0], nums[ev], nums[od])
        den = jnp.where(sub_half[0], nums[ev][HEAD_DIM:HEAD_DIM + 1, :], nums[od][0:1, :])
        lim = jnp.where(sub_half[0], floor[ev:ev + 1, :], floor[od:od + 1, :])
        out_ref[:, pair(ev)] = (numer / jnp.maximum(jnp.abs(den), lim)).T


def _mlstm(qk, v_t, gates_t, batch, seq_len):
    L = MLSTM_CHUNK
    nc = seq_len // L
    qk3 = qk.reshape(batch, seq_len, 2 * M_WIDTH)
    cidx = lambda d, c: jnp.where(d == 0, c, nc - 1 - c)
    return pl.pallas_call(
        functools.partial(_mlstm_kernel, L=L),
        grid=(batch, 2, nc),
        in_specs=[
            pl.BlockSpec((None, L, M_WIDTH), lambda b, d, c: (b, cidx(d, c), 0)),
            pl.BlockSpec((None, L, M_WIDTH), lambda b, d, c: (b, cidx(d, c), 1)),
            pl.BlockSpec((M_WIDTH, L), lambda b, d, c: (0, b * nc + cidx(d, c))),
            pl.BlockSpec((N_GATES, L), lambda b, d, c: (0, b * nc + cidx(d, c))),
        ],
        out_specs=pl.BlockSpec((None, None, L, M_WIDTH), lambda b, d, c: (b, d, cidx(d, c), 0)),
        out_shape=jax.ShapeDtypeStruct((batch, 2, seq_len, M_WIDTH), f32),
        scratch_shapes=[pltpu.VMEM((M_HEADS, LANES, LANES), f32), pltpu.VMEM((M_HEADS, LANES), f32)],
        compiler_params=_cparams(("parallel", "parallel", "arbitrary")),
        name="mlstm",
    )(qk3, qk3, v_t, gates_t)


def _natten_kernel(q_ref, k_ref, v_ref, bias_ref, gq_ref, gk_ref, out_ref, kn_ref, *, seq_len, rb_rows):
    rb = pl.program_id(2)
    rows = seq_len // GRID_W
    norm_rows = 512

    @pl.when(rb == 0)
    def _():
        def body(i, carry):
            sl = pl.ds(pl.multiple_of(i * norm_rows, norm_rows), norm_rows)
            kv = k_ref[sl, :]
            kn_ref[sl, :] = (kv * lax.rsqrt(_head_mean_sq(kv, LANES) + EPS) * gk_ref[...]).astype(bf16)
            return carry
        lax.fori_loop(0, seq_len // norm_rows, body, 0)

    qv = q_ref[...]
    qn = (qv * lax.rsqrt(_head_mean_sq(qv, LANES) + EPS) * gq_ref[...] * (HEAD_DIM ** -0.5)).astype(bf16)
    lane = lax.broadcasted_iota(i32, (1, LANES), 1)
    lo = lane < HEAD_DIM
    nt = (((1,), (1,)), ((), ()))
    pair = 2 * GRID_W

    def window(j):
        r = rb * rb_rows + j
        r0 = jnp.clip(r - WIN_H // 2, 0, rows - WIN_H)
        return r - r0, pl.ds(pl.multiple_of(r0 * GRID_W, GRID_W), WIN_H * GRID_W)

    tiles = []
    for j in range(rb_rows):
        delta, ks = window(j)
        qj = qn[j * GRID_W:(j + 1) * GRID_W, :]
        q2 = jnp.concatenate([jnp.where(lo, qj, jnp.zeros_like(qj)), jnp.where(lo, jnp.zeros_like(qj), qj)], axis=0)
        tiles.append(lax.dot_general(q2, kn_ref[ks, :], nt, preferred_element_type=f32) + bias_ref[delta])
    s = jnp.concatenate(tiles, axis=0)
    p = jnp.exp(s - jnp.max(s, axis=-1, keepdims=True))
    inv = 1.0 / jnp.sum(p, axis=-1, keepdims=True)
    pb = p.astype(bf16)
    for j in range(rb_rows):
        _, ks = window(j)
        o = jnp.dot(pb[j * pair:(j + 1) * pair, :], v_ref[ks, :], preferred_element_type=f32)
        o = o * inv[j * pair:(j + 1) * pair, :]
        out_ref[j * GRID_W:(j + 1) * GRID_W, :] = jnp.where(lo, o[:GRID_W, :], o[GRID_W:, :])


def _natten_bias_table(rel_bias):
    cq = jnp.arange(GRID_W)[:, None]
    ck = jnp.arange(GRID_W)[None, :]
    c0 = jnp.clip(cq - WIN_W // 2, 0, GRID_W - WIN_W)
    col_in = (ck >= c0) & (ck < c0 + WIN_W)
    idx_c = jnp.clip(ck - cq, -(WIN_W - 1), WIN_W - 1) + (WIN_W - 1)
    pick = idx_c[:, :, None] == jnp.arange(2 * WIN_W - 1)
    tz = jnp.sum(jnp.where(pick[None, None], rel_bias.astype(f32)[:, :, None, None, :], 0.0), axis=-1)
    tz = jnp.where(col_in[None, None], tz, NEG_INF)
    tab = jnp.stack([tz[:, WIN_H - 1 - dl:2 * WIN_H - 1 - dl] for dl in range(WIN_H)], axis=1)
    tab = tab.transpose(0, 1, 3, 2, 4).reshape(A_HEADS // 2, 2, WIN_H, GRID_W, WIN_H * GRID_W)
    return tab.transpose(0, 2, 1, 3, 4).reshape(A_HEADS // 2, WIN_H, 2 * GRID_W, WIN_H * GRID_W)


def _natten(qa, ka, va, bias_tab, g_q2, g_k2, batch, seq_len):
    rows = seq_len // GRID_W
    rbr = NAT_ROWS
    tq = rbr * GRID_W
    q3 = qa.reshape(batch, seq_len, A_WIDTH)
    k3 = ka.reshape(batch, seq_len, A_WIDTH)
    v3 = va.reshape(batch, seq_len, A_WIDTH)
    return pl.pallas_call(
        functools.partial(_natten_kernel, seq_len=seq_len, rb_rows=rbr),
        grid=(batch, A_HEADS // 2, rows // rbr),
        in_specs=[
            pl.BlockSpec((None, tq, LANES), lambda b, hp, rb: (b, rb, hp)),
            pl.BlockSpec((None, seq_len, LANES), lambda b, hp, rb: (b, 0, hp)),
            pl.BlockSpec((None, seq_len, LANES), lambda b, hp, rb: (b, 0, hp)),
            pl.BlockSpec((None, WIN_H, 2 * GRID_W, WIN_H * GRID_W), lambda b, hp, rb: (hp, 0, 0, 0)),
            pl.BlockSpec((1, LANES), lambda b, hp, rb: (0, 0)),
            pl.BlockSpec((1, LANES), lambda b, hp, rb: (0, 0)),
        ],
        out_specs=pl.BlockSpec((None, tq, LANES), lambda b, hp, rb: (b, rb, hp)),
        out_shape=jax.ShapeDtypeStruct((batch, seq_len, A_WIDTH), f32),
        scratch_shapes=[pltpu.VMEM((seq_len, LANES), bf16)],
        compiler_params=_cparams(("parallel", "parallel", "arbitrary")),
        name="natten",
    )(q3, k3, v3, bias_tab, g_q2, g_k2)


def _outproj_kernel(x_ref, hf_ref, hb_ref, o_ref, ha_ref, gm_ref, ga_ref, wo_ref, gf_ref,
                    wr1_ref, wr2_ref, br_ref,
                    xmid_ref, hn_ref, ids_ref, pos_ref, gate_ref, cnt_ref, carry_ref, *, tm):
    i = pl.program_id(0)

    @pl.when(i == 0)
    def _():
        carry_ref[...] = jnp.zeros_like(carry_ref)

    hm = hf_ref[...] + hb_ref[...]
    hm = hm * lax.rsqrt(_head_mean_sq(hm, M_WIDTH) + EPS) * gm_ref[...] * jax.nn.sigmoid(o_ref[...])
    ha = _rms(ha_ref[...], ga_ref[...])
    mix = (jnp.dot(hm.astype(bf16), wo_ref[0:M_WIDTH, :], preferred_element_type=f32)
           + jnp.dot(ha.astype(bf16), wo_ref[M_WIDTH:M_WIDTH + A_WIDTH, :], preferred_element_type=f32))
    xm = x_ref[...] + mix
    xmid_ref[...] = xm
    hn = _rms(xm, gf_ref[...])
    _store_token_tiles(hn_ref, hn)

    h1, h2 = _split2(hn)
    logits = (jnp.dot(h1, wr1_ref[...], preferred_element_type=f32)
              + (jnp.dot(h1, wr2_ref[...], preferred_element_type=f32)
                 + jnp.dot(h2, wr1_ref[...], preferred_element_type=f32))) + br_ref[...]
    lane = lax.broadcasted_iota(i32, (tm, LANES), 1)
    work = logits
    vals, idxs, sels = [], [], []
    for _ in range(TOP_K):
        mx = jnp.max(work, axis=-1, keepdims=True)
        idx = jnp.min(jnp.where(work == mx, lane, LANES), axis=-1, keepdims=True)
        sel = lane == idx
        vals.append(mx)
        idxs.append(idx)
        sels.append(sel)
        work = jnp.where(sel, NEG_INF, work)
    es = [jnp.exp(v - vals[0]) for v in vals]
    tot = es[0] + es[1] + es[2] + es[3]

    onehot = jnp.where(sels[0] | sels[1] | sels[2] | sels[3], 1.0, 0.0)
    tri = (lax.broadcasted_iota(i32, (tm, tm), 0) > lax.broadcasted_iota(i32, (tm, tm), 1)).astype(bf16)
    base = jnp.dot(tri, onehot.astype(bf16), preferred_element_type=f32) + carry_ref[...]
    ids_out = jnp.zeros((tm, LANES), i32)
    pos_out = jnp.zeros((tm, LANES), i32)
    gate_out = jnp.zeros((tm, LANES), f32)
    for k in range(TOP_K):
        pk = jnp.sum(jnp.where(sels[k], base, 0.0), axis=-1, keepdims=True).astype(i32)
        ids_out = jnp.where(lane == k, idxs[k], ids_out)
        pos_out = jnp.where(lane == k, pk, pos_out)
        gate_out = jnp.where(lane == k, es[k] / tot, gate_out)
    ids_ref[...] = ids_out[:, :TOP_K]
    pos_ref[...] = pos_out[:, :TOP_K]
    gate_ref[...] = gate_out[:, :TOP_K]
    carry_ref[...] += jnp.sum(onehot, axis=0, keepdims=True)
    cnt_ref[...] = carry_ref[...]


def _outproj(x2, hfb, o_m, ha, g_m, g_a, w_out, g_ffn, wr1, wr2, b_r, batch, seq_len):
    n = x2.shape[0]
    tm = TM_OUT
    tpb = seq_len // tm
    row = lambda i: (i, 0)
    const = lambda i: (0, 0)
    ha2 = ha.reshape(n, A_WIDTH)
    return pl.pallas_call(
        functools.partial(_outproj_kernel, tm=tm),
        grid=(n // tm,),
        in_specs=[
            pl.BlockSpec((tm, D_MODEL), row),
            pl.BlockSpec((None, None, tm, M_WIDTH), lambda i: (i // tpb, 0, i % tpb, 0)),
            pl.BlockSpec((None, None, tm, M_WIDTH), lambda i: (i // tpb, 1, i % tpb, 0)),
            pl.BlockSpec((tm, M_WIDTH), row),
            pl.BlockSpec((tm, A_WIDTH), row),
            pl.BlockSpec((1, M_WIDTH), const),
            pl.BlockSpec((1, A_WIDTH), const),
            pl.BlockSpec((M_WIDTH + A_WIDTH, D_MODEL), const),
            pl.BlockSpec((1, D_MODEL), const),
            pl.BlockSpec((D_MODEL, LANES), const),
            pl.BlockSpec((D_MODEL, LANES), const),
            pl.BlockSpec((1, LANES), const),
        ],
        out_specs=[
            pl.BlockSpec((tm, D_MODEL), row),
            pl.BlockSpec((tm * TOKEN_TILE_ROWS, LANES), row),
            pl.BlockSpec((tm, TOP_K), row),
            pl.BlockSpec((tm, TOP_K), row),
            pl.BlockSpec((tm, TOP_K), row),
            pl.BlockSpec((1, LANES), const),
        ],
        out_shape=(
            jax.ShapeDtypeStruct((n, D_MODEL), f32),
            jax.ShapeDtypeStruct((n * TOKEN_TILE_ROWS, LANES), f32),
            jax.ShapeDtypeStruct((n, TOP_K), i32),
            jax.ShapeDtypeStruct((n, TOP_K), i32),
            jax.ShapeDtypeStruct((n, TOP_K), f32),
            jax.ShapeDtypeStruct((1, LANES), f32),
        ),
        scratch_shapes=[pltpu.VMEM((1, LANES), f32)],
        compiler_params=_cparams(("arbitrary",)),
        name="outproj_router",
    )(x2, hfb, hfb, o_m, ha2, g_m, g_a, w_out, g_ffn, wr1, wr2, b_r)


def _dispatch_kernel(pend_ref, padded_ref, slot_ref, hn_ref, xb_hbm, zero_ref, sem, *, tm, tm_e, n_blocks):
    i = pl.program_id(0)

    blk = tm_e * TOKEN_TILE_ROWS

    def zero_copy(e):
        start = (pend_ref[e] - tm_e) * TOKEN_TILE_ROWS
        return pltpu.make_async_copy(zero_ref, xb_hbm.at[pl.ds(pl.multiple_of(start, blk), blk)], sem)

    def tail_copy(b):
        return pltpu.make_async_copy(zero_ref, xb_hbm.at[pl.ds(pl.multiple_of(b * blk, blk), blk)], sem)

    @pl.when(i == 0)
    def _():
        zero_ref[...] = jnp.zeros_like(zero_ref)
        for e in range(N_EXPERTS):
            @pl.when(padded_ref[e] > 0)
            def _():
                zero_copy(e).start()
        used = pend_ref[N_EXPERTS - 1] // tm_e

        def tail_start(b, carry):
            tail_copy(b).start()
            return carry

        def tail_wait(b, carry):
            tail_copy(b).wait()
            return carry

        lax.fori_loop(used, n_blocks, tail_start, 0)
        for e in range(N_EXPERTS):
            @pl.when(padded_ref[e] > 0)
            def _():
                zero_copy(e).wait()
        lax.fori_loop(used, n_blocks, tail_wait, 0)

    def row(j, k):
        return pltpu.make_async_copy(_token_tile(hn_ref, j), _token_tile(xb_hbm, slot_ref[0, 0, j * TOP_K + k]), sem)

    def issue(j, carry):
        for k in range(TOP_K):
            row(j, k).start(priority=k % 2)
        return carry

    lax.fori_loop(0, tm, issue, 0, unroll=4)

    def drain(j, carry):
        for k in range(TOP_K):
            row(0, k).wait()
        return carry

    lax.fori_loop(0, tm, drain, 0, unroll=4)


def _dispatch(hn8, slot_flat, pend, padded, cap):
    n = hn8.shape[0] // TOKEN_TILE_ROWS
    tm = TM_DISPATCH
    slot3 = slot_flat.reshape(n // tm, 1, tm * TOP_K)
    grid_spec = pltpu.PrefetchScalarGridSpec(
        num_scalar_prefetch=2,
        grid=(n // tm,),
        in_specs=[
            pl.BlockSpec((1, 1, tm * TOP_K), lambda i, pe, pa: (i, 0, 0), memory_space=pltpu.SMEM),
            pl.BlockSpec((tm * TOKEN_TILE_ROWS, LANES), lambda i, pe, pa: (i, 0)),
        ],
        out_specs=pl.BlockSpec(memory_space=pl.ANY),
        scratch_shapes=[pltpu.VMEM((TM_EXP * TOKEN_TILE_ROWS, LANES), f32), pltpu.SemaphoreType.DMA(())],
    )
    return pl.pallas_call(
        functools.partial(_dispatch_kernel, tm=tm, tm_e=TM_EXP, n_blocks=cap // TM_EXP),
        grid_spec=grid_spec,
        out_shape=jax.ShapeDtypeStruct((cap * TOKEN_TILE_ROWS, LANES), f32),
        compiler_params=_cparams(("arbitrary",)),
        name="dispatch",
    )(pend, padded, slot3, hn8)


def _experts_kernel(be_ref, nv_ref, xb_ref, wgu_ref, bgu_ref, wd_ref, bd_ref, yb_ref, *, tm):
    j = pl.program_id(0)

    @pl.when(j < nv_ref[0])
    def _():
        xv = _load_token_tiles(xb_ref, tm).astype(bf16)
        h = jnp.dot(xv, wgu_ref[...], preferred_element_type=f32) + bgu_ref[...]
        gt = jnp.minimum(h[:, :D_FF], SWIGLU_LIMIT)
        up = jnp.clip(h[:, D_FF:], -SWIGLU_LIMIT, SWIGLU_LIMIT)
        act = (up + 1.0) * (gt * jax.nn.sigmoid(SWIGLU_ALPHA * gt))
        _store_token_tiles(yb_ref, jnp.dot(act.astype(bf16), wd_ref[...], preferred_element_type=f32) + bd_ref[...])

    @pl.when(j >= nv_ref[0])
    def _():
        yb_ref[...] = jnp.zeros_like(yb_ref)


def _experts(xb, block_e, nvalid, w_gu, b_gu, w_d, b_d):
    cap = xb.shape[0] // TOKEN_TILE_ROWS
    tm = TM_EXP
    grid_spec = pltpu.PrefetchScalarGridSpec(
        num_scalar_prefetch=2,
        grid=(cap // tm,),
        in_specs=[
            pl.BlockSpec((tm * TOKEN_TILE_ROWS, LANES), lambda j, be, nv: (jnp.minimum(j, nv[0] - 1), 0)),
            pl.BlockSpec((None, D_MODEL, 2 * D_FF), lambda j, be, nv: (be[j], 0, 0)),
            pl.BlockSpec((None, 1, 2 * D_FF), lambda j, be, nv: (be[j], 0, 0)),
            pl.BlockSpec((None, D_FF, D_MODEL), lambda j, be, nv: (be[j], 0, 0)),
            pl.BlockSpec((None, 1, D_MODEL), lambda j, be, nv: (be[j], 0, 0)),
        ],
        out_specs=pl.BlockSpec((tm * TOKEN_TILE_ROWS, LANES), lambda j, be, nv: (j, 0)),
    )
    return pl.pallas_call(
        functools.partial(_experts_kernel, tm=tm),
        grid_spec=grid_spec,
        out_shape=jax.ShapeDtypeStruct((cap * TOKEN_TILE_ROWS, LANES), f32),
        compiler_params=_cparams(("arbitrary",)),
        name="experts",
    )(block_e, nvalid, xb, w_gu, b_gu, w_d, b_d)


def _combine_kernel(slot_ref, slot_next_ref, xmid_ref, gate_ref, p_ref, wproj_ref, gpost_ref, gple_ref, wgate_ref,
                    yb_hbm, out_ref, ybuf_ref, sem, *, tm):
    i = pl.program_id(0)
    cur = lax.rem(i, 2)

    def row(slots, buf, j, k):
        return pltpu.make_async_copy(_token_tile(yb_hbm, slots[0, 0, j * TOP_K + k]),
                                     _token_tile(ybuf_ref.at[buf, k], j), sem.at[buf])

    def gather(slots, buf):
        def issue(j, carry):
            for k in range(TOP_K):
                row(slots, buf, j, k).start(priority=k % 2)
            return carry
        lax.fori_loop(0, tm, issue, 0, unroll=4)

    @pl.when(i == 0)
    def _():
        gather(slot_ref, 0)

    @pl.when(i + 1 < pl.num_programs(0))
    def _():
        gather(slot_next_ref, 1 - cur)

    pe = _rms(jnp.dot(p_ref[...].astype(bf16), wproj_ref[...], preferred_element_type=f32), gpost_ref[...])

    def drain(j, carry):
        for k in range(TOP_K):
            row(slot_ref, cur, 0, k).wait()
        return carry

    lax.fori_loop(0, tm, drain, 0, unroll=4)

    gate = gate_ref[...]
    y = gate[:, 0:1] * _load_token_tiles(ybuf_ref.at[cur, 0], tm)
    for k in range(1, TOP_K):
        y = y + gate[:, k:k + 1] * _load_token_tiles(ybuf_ref.at[cur, k], tm)
    x2 = xmid_ref[...] + y
    gl = jnp.dot(_rms(x2, gple_ref[...]).astype(bf16), wgate_ref[...], preferred_element_type=f32)
    out_ref[...] = x2 + jax.nn.sigmoid(gl) * pe


def _combine(slot_flat, xmid, gate, p2, w_proj, g_post, g_ple, w_gate, yb):
    n = xmid.shape[0]
    tm = TM_CMB
    steps = n // tm
    slot3 = slot_flat.reshape(steps, 1, tm * TOP_K)
    row = lambda i: (i, 0)
    const = lambda i: (0, 0)
    return pl.pallas_call(
        functools.partial(_combine_kernel, tm=tm),
        grid=(steps,),
        in_specs=[
            pl.BlockSpec((1, 1, tm * TOP_K), lambda i: (i, 0, 0), memory_space=pltpu.SMEM),
            pl.BlockSpec((1, 1, tm * TOP_K), lambda i: (jnp.minimum(i + 1, steps - 1), 0, 0),
                         memory_space=pltpu.SMEM),
            pl.BlockSpec((tm, D_MODEL), row),
            pl.BlockSpec((tm, TOP_K), row),
            pl.BlockSpec((tm, PLE_DIM), row),
            pl.BlockSpec((PLE_DIM, D_MODEL), const),
            pl.BlockSpec((1, D_MODEL), const),
            pl.BlockSpec((1, D_MODEL), const),
            pl.BlockSpec((D_MODEL, D_MODEL), const),
            pl.BlockSpec(memory_space=pl.ANY),
        ],
        out_specs=pl.BlockSpec((tm, D_MODEL), row),
        out_shape=jax.ShapeDtypeStruct((n, D_MODEL), f32),
        scratch_shapes=[pltpu.VMEM((2, TOP_K, tm * TOKEN_TILE_ROWS, LANES), f32), pltpu.SemaphoreType.DMA((2,))],
        compiler_params=_cparams(("arbitrary",)),
        name="combine_ple",
    )(slot3, slot3, xmid, gate, p2, w_proj, g_post, g_ple, w_gate, yb)


def _wprep_kernel(w_ref, out_ref):
    grp = 2 * LANES
    src = lax.broadcasted_iota(i32, (grp, grp), 0)
    dst = lax.broadcasted_iota(i32, (grp, grp), 1)
    want = jnp.where(dst < LANES, 2 * dst, 2 * (dst - LANES) + 1)
    perm = (src == want).astype(bf16)
    for g in range(2 * D_FF // grp):
        t = jnp.dot(w_ref[:, g * grp:(g + 1) * grp].astype(bf16), perm, preferred_element_type=f32)
        out_ref[:, g * LANES:(g + 1) * LANES] = t[:, :LANES].astype(bf16)
        out_ref[:, D_FF + g * LANES:D_FF + (g + 1) * LANES] = t[:, LANES:].astype(bf16)


def _wprep(w_gate_up):
    ne = w_gate_up.shape[0]
    tr = TR_WPREP
    return pl.pallas_call(
        _wprep_kernel,
        grid=(ne, D_MODEL // tr),
        in_specs=[pl.BlockSpec((None, tr, 2 * D_FF), lambda e, r: (e, r, 0))],
        out_specs=pl.BlockSpec((None, tr, 2 * D_FF), lambda e, r: (e, r, 0)),
        out_shape=jax.ShapeDtypeStruct((ne, D_MODEL, 2 * D_FF), bf16),
        compiler_params=_cparams(("parallel", "parallel")),
        name="wprep",
    )(w_gate_up)


def _prep_weights(g_mix, w_in, b_gates, conv_w, g_m_head, g_q, g_k, rel_bias, g_a_out, w_out, g_ffn,
                  w_router, b_router, w_gate_up, b_gate_up, w_down, b_down, g_ple, w_ple_gate, w_ple_proj,
                  g_ple_post):
    g0 = 4 * M_WIDTH
    w = w_in[0]
    pw = {}
    pw["g_mix"] = g_mix[0][None, :]
    pw["w_main"] = jnp.concatenate([w[:, :g0], w[:, g0 + N_GATES:]], axis=1).astype(bf16)
    pw["w_gate"] = jnp.pad(w[:, g0:g0 + N_GATES], ((0, 0), (0, LANES - N_GATES))).astype(bf16)
    pw["b_gate"] = jnp.pad(b_gates[0], (0, LANES - N_GATES))[None, :]
    pw["conv_w"] = conv_w[0]
    pw["g_m"] = g_m_head[0].reshape(1, M_WIDTH)
    pw["g_q2"] = jnp.tile(g_q[0], 2)[None, :]
    pw["g_k2"] = jnp.tile(g_k[0], 2)[None, :]
    pw["bias_tab"] = _natten_bias_table(rel_bias[0])
    pw["g_a"] = g_a_out[0][None, :]
    pw["w_out"] = w_out[0].astype(bf16)
    pw["g_ffn"] = g_ffn[0][None, :]
    wr = jnp.pad(w_router[0], ((0, 0), (0, LANES - N_EXPERTS)))
    wr1 = wr.astype(bf16)
    pw["wr1"] = wr1
    pw["wr2"] = (wr - wr1.astype(f32)).astype(bf16)
    pw["b_r"] = jnp.pad(b_router[0], (0, LANES - N_EXPERTS), constant_values=NEG_INF)[None, :]
    pw["w_gu"] = _wprep(w_gate_up[0])
    bgu = b_gate_up[0]
    pw["b_gu"] = jnp.concatenate([bgu[:, 0::2], bgu[:, 1::2]], axis=-1)[:, None, :]
    pw["w_d"] = w_down[0].astype(bf16)
    pw["b_d"] = b_down[0][:, None, :]
    pw["g_ple"] = g_ple[0][None, :]
    pw["w_ple_gate"] = w_ple_gate[0].astype(bf16)
    pw["w_ple_proj"] = w_ple_proj[0].astype(bf16)
    pw["g_ple_post"] = g_ple_post[0][None, :]
    return pw


def _trunk(x, p, pw):
    batch, seq_len, _ = x.shape
    n = batch * seq_len
    x2 = x.reshape(n, D_MODEL)
    qk, v_m, o_m, gates, qa, ka, va = _inproj(x2, pw["g_mix"], pw["w_main"], pw["w_gate"], pw["b_gate"],
                                              pw["conv_w"], seq_len)
    hfb = _mlstm(qk, v_m, gates, batch, seq_len)
    ha = _natten(qa, ka, va, pw["bias_tab"], pw["g_q2"], pw["g_k2"], batch, seq_len)
    xmid, hn, ids, pos, gate, cnt = _outproj(x2, hfb, o_m, ha, pw["g_m"], pw["g_a"], pw["w_out"], pw["g_ffn"],
                                             pw["wr1"], pw["wr2"], pw["b_r"], batch, seq_len)

    tm_e = TM_EXP
    counts = cnt[0, :N_EXPERTS].astype(i32)
    padded = (counts + tm_e - 1) // tm_e * tm_e
    pend = jnp.cumsum(padded).astype(i32)
    pstart = pend - padded
    eid = jnp.arange(N_EXPERTS, dtype=i32)
    slot = (jnp.sum(jnp.where(ids[:, :, None] == eid, pstart, 0), axis=-1) + pos).reshape(-1)
    nk = n * TOP_K
    n_blocks = (nk + N_EXPERTS * (tm_e - 1) + tm_e - 1) // tm_e
    cap = n_blocks * tm_e
    block_start = jnp.arange(n_blocks, dtype=i32) * tm_e
    block_e = jnp.minimum(jnp.sum((pend[None, :] <= block_start[:, None]).astype(i32), axis=1), N_EXPERTS - 1)
    nvalid = (pend[-1:] // tm_e).astype(i32)

    xb = _dispatch(hn, slot, pend, padded, cap)
    yb = _experts(xb, block_e, nvalid, pw["w_gu"], pw["b_gu"], pw["w_d"], pw["b_d"])
    out = _combine(slot, xmid, gate, p.reshape(n, PLE_DIM), pw["w_ple_proj"], pw["g_ple_post"], pw["g_ple"],
                   pw["w_ple_gate"], yb)
    return out.reshape(batch, seq_len, D_MODEL)


def kernel(x_prompt, x_sample, p_prompt, p_sample, g_mix, w_in, b_gates, conv_w, g_m_head, g_q, g_k, rel_bias,
           g_a_out, w_out, g_ffn, w_router, b_router, w_gate_up, b_gate_up, w_down, b_down, g_ple, w_ple_gate,
           w_ple_proj, g_ple_post):
    assert w_in.shape[0] == 1, "single-layer trunk"
    pw = _prep_weights(g_mix, w_in, b_gates, conv_w, g_m_head, g_q, g_k, rel_bias, g_a_out, w_out, g_ffn,
                       w_router, b_router, w_gate_up, b_gate_up, w_down, b_down, g_ple, w_ple_gate, w_ple_proj,
                       g_ple_post)
    y_prompt = _trunk(x_prompt, p_prompt[0], pw)
    y_sample = _trunk(x_sample, p_sample[0], pw)
    return (y_prompt, y_sample)
```

```python
import functools

import jax
import jax.numpy as jnp
from jax import lax
from jax.experimental import pallas as pl
from jax.experimental.pallas import tpu as pltpu

f32 = jnp.float32
bf16 = jnp.bfloat16
i32 = jnp.int32

D_MODEL = 1024
HEAD_DIM = 64
M_HEADS = 8
A_HEADS = 8
M_WIDTH = M_HEADS * HEAD_DIM
A_WIDTH = A_HEADS * HEAD_DIM
N_GATES = 4 * M_HEADS
GRID_W = 64
WIN_H = 8
WIN_W = 16
N_EXPERTS = 32
TOP_K = 4
D_FF = 1024
SWIGLU_LIMIT = 7.0
SWIGLU_ALPHA = 1.702
PLE_DIM = 256
EPS = 1e-6

LANES = 128
SUBLANES = 8
MXU_TILE = 256
VMEM_LIMIT = 56 * 1024 * 1024

TM_IN = 512
MLSTM_CHUNK = 256
NAT_ROWS = 16
TM_OUT = 512
TM_EXP = 512
TM_DISPATCH = 256
TR_WPREP = 512
TM_CMB = 256
CMB_CHUNK = 32

NEG_INF = float("-inf")


def _cparams(sem):
    return pltpu.CompilerParams(dimension_semantics=sem, vmem_limit_bytes=VMEM_LIMIT)


def _rms(xv, g):
    return xv * lax.rsqrt(jnp.mean(xv * xv, axis=-1, keepdims=True) + EPS) * g


def _split2(a):
    hi = a.astype(bf16)
    lo = (a - hi.astype(f32)).astype(bf16)
    return hi, lo


def _split3(a):
    hi = a.astype(bf16)
    r = a - hi.astype(f32)
    mid = r.astype(bf16)
    lo = (r - mid.astype(f32)).astype(bf16)
    return hi, mid, lo


def _head_mean_sq(xv, width):
    grp = min(width, MXU_TILE)
    a = lax.broadcasted_iota(i32, (grp, grp), 0) // HEAD_DIM
    b = lax.broadcasted_iota(i32, (grp, grp), 1) // HEAD_DIM
    bd = jnp.where(a == b, 1.0 / HEAD_DIM, 0.0).astype(bf16)
    hi, lo = _split2(xv * xv)
    parts = []
    for g in range(width // grp):
        sl = slice(g * grp, (g + 1) * grp)
        parts.append(jnp.dot(hi[:, sl], bd, preferred_element_type=f32)
                     + jnp.dot(lo[:, sl], bd, preferred_element_type=f32))
    return parts[0] if len(parts) == 1 else jnp.concatenate(parts, axis=1)


TOKEN_TILE_ROWS = D_MODEL // LANES


def _store_token_tiles(ref, val):
    n = val.shape[0]
    for c in range(TOKEN_TILE_ROWS):
        ref[pl.ds(c, n, stride=TOKEN_TILE_ROWS), :] = val[:, c * LANES:(c + 1) * LANES]


def _token_tile(ref, t):
    return ref.at[pl.ds(pl.multiple_of(t * TOKEN_TILE_ROWS, TOKEN_TILE_ROWS), TOKEN_TILE_ROWS)]


def _load_token_tiles(ref, n, first=0):
    base = first * TOKEN_TILE_ROWS
    return jnp.concatenate([ref[pl.ds(base + c, n, stride=TOKEN_TILE_ROWS), :] for c in range(TOKEN_TILE_ROWS)],
                           axis=1)


def _log_sigmoid(x):
    return jnp.minimum(x, 0.0) - jnp.log1p(jnp.exp(-jnp.abs(x)))


def _inproj_kernel(x_ref, xp_ref, xn_ref, g_ref, wm_ref, wg_ref, bg_ref, cw_ref,
                   qk_ref, v_ref, o_ref, gates_ref, qa_ref, ka_ref, va_ref, *, tm, seq_len):
    i = pl.program_id(0)
    hf = _rms(jnp.concatenate([xp_ref[...], x_ref[...], xn_ref[...]], axis=0), g_ref[...])
    ext = tm + 2 * SUBLANES
    h = hf[SUBLANES:SUBLANES + tm, :].astype(bf16)
    zq = jnp.dot(hf.astype(bf16), wm_ref[:, :2 * M_WIDTH], preferred_element_type=f32)
    z = jnp.dot(h, wm_ref[:, 2 * M_WIDTH:], preferred_element_type=f32)
    zg = jnp.dot(h, wg_ref[...], preferred_element_type=f32) + bg_ref[...]
    gates_ref[...] = zg.T[:N_GATES, :]
    start = lax.rem(i * tm, seq_len)
    rid = lax.broadcasted_iota(i32, (tm, 1), 0)
    u = zq[SUBLANES:SUBLANES + tm, :]
    u_prev = pltpu.roll(zq, 1, 0)[SUBLANES:SUBLANES + tm, :]
    u_next = pltpu.roll(zq, ext - 1, 0)[SUBLANES:SUBLANES + tm, :]
    u_prev = jnp.where(jnp.logical_and(rid == 0, start == 0), 0.0, u_prev)
    u_next = jnp.where(jnp.logical_and(rid == tm - 1, start + tm == seq_len), 0.0, u_next)
    cw = cw_ref[...]
    c = u_prev * cw[0:1, :] + u * cw[1:2, :] + u_next * cw[2:3, :]
    qk_ref[...] = (c * jax.nn.sigmoid(c)).astype(bf16)
    v_ref[...] = z[:, 0:M_WIDTH].T.astype(bf16)
    o_ref[...] = z[:, M_WIDTH:2 * M_WIDTH]
    base = 2 * M_WIDTH
    qa_ref[...] = z[:, base:base + A_WIDTH]
    ka_ref[...] = z[:, base + A_WIDTH:base + 2 * A_WIDTH]
    va_ref[...] = z[:, base + 2 * A_WIDTH:base + 3 * A_WIDTH].astype(bf16)


def _inproj(x2, g_mix, w_main, w_gate, b_gate, conv_w, seq_len):
    n = x2.shape[0]
    tm = TM_IN
    nb8 = n // SUBLANES
    r8 = tm // SUBLANES
    wcols = w_main.shape[1]
    row = lambda i: (i, 0)
    const = lambda i: (0, 0)
    out_shapes = (
        jax.ShapeDtypeStruct((n, 2 * M_WIDTH), bf16),
        jax.ShapeDtypeStruct((M_WIDTH, n), bf16),
        jax.ShapeDtypeStruct((n, M_WIDTH), f32),
        jax.ShapeDtypeStruct((N_GATES, n), f32),
        jax.ShapeDtypeStruct((n, A_WIDTH), f32),
        jax.ShapeDtypeStruct((n, A_WIDTH), f32),
        jax.ShapeDtypeStruct((n, A_WIDTH), bf16),
    )
    return pl.pallas_call(
        functools.partial(_inproj_kernel, tm=tm, seq_len=seq_len),
        grid=(n // tm,),
        in_specs=[
            pl.BlockSpec((tm, D_MODEL), row),
            pl.BlockSpec((SUBLANES, D_MODEL), lambda i: (jnp.maximum(i * r8 - 1, 0), 0)),
            pl.BlockSpec((SUBLANES, D_MODEL), lambda i: (jnp.minimum((i + 1) * r8, nb8 - 1), 0)),
            pl.BlockSpec((1, D_MODEL), const),
            pl.BlockSpec((D_MODEL, wcols), const),
            pl.BlockSpec((D_MODEL, LANES), const),
            pl.BlockSpec((1, LANES), const),
            pl.BlockSpec((3, 2 * M_WIDTH), const),
        ],
        out_specs=[
            pl.BlockSpec((tm, 2 * M_WIDTH), row),
            pl.BlockSpec((M_WIDTH, tm), lambda i: (0, i)),
            pl.BlockSpec((tm, M_WIDTH), row),
            pl.BlockSpec((N_GATES, tm), lambda i: (0, i)),
            pl.BlockSpec((tm, A_WIDTH), row),
            pl.BlockSpec((tm, A_WIDTH), row),
            pl.BlockSpec((tm, A_WIDTH), row),
        ],
        out_shape=out_shapes,
        compiler_params=_cparams(("parallel",)),
        name="inproj",
    )(x2, x2, x2, g_mix, w_main, w_gate, b_gate, conv_w)


def _mlstm_kernel(q_ref, k_ref, vt_ref, g_ref, out_ref, c_ref, m_ref, *, L):
    d = pl.program_id(1)
    c = pl.program_id(2)

    @pl.when(c == 0)
    def _():
        c_ref[...] = jnp.zeros_like(c_ref)
        m_ref[...] = jnp.zeros_like(m_ref)

    fwd = d == 0
    sgn = jnp.where(fwd, 1, -1)
    row = lax.broadcasted_iota(i32, (L, L), 0)
    col = lax.broadcasted_iota(i32, (L, L), 1)
    vis = (col - row) * sgn >= 0
    t_row = vis.astype(bf16)

    g = g_ref[...]
    h8 = M_HEADS
    li = jnp.where(fwd, g[0:h8, :], g[2 * h8:3 * h8, :])
    lf = _log_sigmoid(jnp.where(fwd, g[h8:2 * h8, :], g[3 * h8:4 * h8, :]))
    b = sum(jnp.dot(part, t_row, preferred_element_type=f32) for part in _split3(lf))
    bl = jnp.sum(lf, axis=1, keepdims=True)
    r_row = li - b
    pos = lax.broadcasted_iota(i32, (1, L), 1)
    cm = r_row
    sh = 1
    while sh < L:
        from_before = jnp.where(pos >= sh, pltpu.roll(cm, sh, 1), NEG_INF)
        from_after = jnp.where(pos < L - sh, pltpu.roll(cm, L - sh, 1), NEG_INF)
        cm = jnp.maximum(cm, jnp.where(fwd, from_before, from_after))
        sh *= 2
    m_prev = m_ref[:, 0:1]
    mx_r = jnp.maximum(m_prev, cm)
    m_new = bl + jnp.maximum(m_prev, jnp.max(r_row, axis=1, keepdims=True))
    decay = jnp.exp(bl + m_prev - m_new)
    m_ref[...] = jnp.broadcast_to(m_new, (h8, LANES))
    w_int = jnp.exp(m_prev - mx_r)
    floor = jnp.exp(-(b + mx_r))
    wk = jnp.exp(bl + r_row - m_new)
    r_col = jnp.concatenate([r_row, jnp.zeros((LANES - h8, L), f32)], axis=0).T

    lane = lax.broadcasted_iota(i32, (1, LANES), 1)
    sub = lax.broadcasted_iota(i32, (LANES, 1), 0)
    lane_half = (lane < HEAD_DIM, lane >= HEAD_DIM)
    sub_half = (sub < HEAD_DIM, sub >= HEAD_DIM)
    nt = (((1,), (1,)), ((), ()))

    heads = range(M_HEADS)
    pair = lambda h: slice((h // 2) * LANES, (h // 2 + 1) * LANES)
    qz = [jnp.where(lane_half[h % 2], q_ref[:, pair(h)] * (HEAD_DIM ** -0.5), 0).astype(bf16) for h in heads]
    kz = [jnp.where(lane_half[h % 2], k_ref[:, pair(h)], 0).astype(bf16) for h in heads]
    va_t = [jnp.where(sub_half[h % 2], vt_ref[pair(h), :], 1).astype(bf16) for h in heads]
    kq = [lax.dot_general(kz[h], qz[h], nt, preferred_element_type=f32) for h in heads]
    s_t = [jnp.where(vis, kq[h] * jnp.exp(r_col[:, h:h + 1] - mx_r[h:h + 1, :]), 0.0).astype(bf16) for h in heads]
    cst = [c_ref[h] for h in heads]
    nums = [w_int[h:h + 1, :] * lax.dot_general(cst[h].astype(bf16), qz[h], nt, preferred_element_type=f32)
            + jnp.dot(va_t[h], s_t[h], preferred_element_type=f32) for h in heads]
    for h in heads:
        vw = (va_t[h].astype(f32) * wk[h:h + 1, :]).astype(bf16)
        c_ref[h] = decay[h:h + 1, :] * cst[h] + jnp.dot(vw, kz[h], preferred_element_type=f32)
    for p in range(M_HEADS // 2):
        ev, od = 2 * p, 2 * p + 1
        numer = jnp.where(sub_half[0], nums[ev], nums[od])
        den = jnp.where(sub_half[0], nums[ev][HEAD_DIM:HEAD_DIM + 1, :], nums[od][0:1, :])
        lim = jnp.where(sub_half[0], floor[ev:ev + 1, :], floor[od:od + 1, :])
        out_ref[:, pair(ev)] = (numer / jnp.maximum(jnp.abs(den), lim)).T


def _mlstm(qk, v_t, gates_t, batch, seq_len):
    L = MLSTM_CHUNK
    nc = seq_len // L
    qk3 = qk.reshape(batch, seq_len, 2 * M_WIDTH)
    cidx = lambda d, c: jnp.where(d == 0, c, nc - 1 - c)
    return pl.pallas_call(
        functools.partial(_mlstm_kernel, L=L),
        grid=(batch, 2, nc),
        in_specs=[
            pl.BlockSpec((None, L, M_WIDTH), lambda b, d, c: (b, cidx(d, c), 0)),
            pl.BlockSpec((None, L, M_WIDTH), lambda b, d, c: (b, cidx(d, c), 1)),
            pl.BlockSpec((M_WIDTH, L), lambda b, d, c: (0, b * nc + cidx(d, c))),
            pl.BlockSpec((N_GATES, L), lambda b, d, c: (0, b * nc + cidx(d, c))),
        ],
        out_specs=pl.BlockSpec((None, None, L, M_WIDTH), lambda b, d, c: (b, d, cidx(d, c), 0)),
        out_shape=jax.ShapeDtypeStruct((batch, 2, seq_len, M_WIDTH), f32),
        scratch_shapes=[pltpu.VMEM((M_HEADS, LANES, LANES), f32), pltpu.VMEM((M_HEADS, LANES), f32)],
        compiler_params=_cparams(("parallel", "parallel", "arbitrary")),
        name="mlstm",
    )(qk3, qk3, v_t, gates_t)


def _natten_kernel(q_ref, k_ref, v_ref, bias_ref, gq_ref, gk_ref, out_ref, kn_ref, *, seq_len, rb_rows):
    rb = pl.program_id(2)
    rows = seq_len // GRID_W
    norm_rows = 512

    @pl.when(rb == 0)
    def _():
        def body(i, carry):
            sl = pl.ds(pl.multiple_of(i * norm_rows, norm_rows), norm_rows)
            kv = k_ref[sl, :]
            kn_ref[sl, :] = (kv * lax.rsqrt(_head_mean_sq(kv, LANES) + EPS) * gk_ref[...]).astype(bf16)
            return carry
        lax.fori_loop(0, seq_len // norm_rows, body, 0)

    qv = q_ref[...]
    qn = (qv * lax.rsqrt(_head_mean_sq(qv, LANES) + EPS) * gq_ref[...] * (HEAD_DIM ** -0.5)).astype(bf16)
    lane = lax.broadcasted_iota(i32, (1, LANES), 1)
    lo = lane < HEAD_DIM
    nt = (((1,), (1,)), ((), ()))
    pair = 2 * GRID_W

    def window(j):
        r = rb * rb_rows + j
        r0 = jnp.clip(r - WIN_H // 2, 0, rows - WIN_H)
        return r - r0, pl.ds(pl.multiple_of(r0 * GRID_W, GRID_W), WIN_H * GRID_W)

    tiles = []
    for j in range(rb_rows):
        delta, ks = window(j)
        qj = qn[j * GRID_W:(j + 1) * GRID_W, :]
        q2 = jnp.concatenate([jnp.where(lo, qj, jnp.zeros_like(qj)), jnp.where(lo, jnp.zeros_like(qj), qj)], axis=0)
        tiles.append(lax.dot_general(q2, kn_ref[ks, :], nt, preferred_element_type=f32) + bias_ref[delta])
    s = jnp.concatenate(tiles, axis=0)
    p = jnp.exp(s - jnp.max(s, axis=-1, keepdims=True))
    inv = 1.0 / jnp.sum(p, axis=-1, keepdims=True)
    pb = p.astype(bf16)
    for j in range(rb_rows):
        _, ks = window(j)
        o = jnp.dot(pb[j * pair:(j + 1) * pair, :], v_ref[ks, :], preferred_element_type=f32)
        o = o * inv[j * pair:(j + 1) * pair, :]
        out_ref[j * GRID_W:(j + 1) * GRID_W, :] = jnp.where(lo, o[:GRID_W, :], o[GRID_W:, :])


def _natten_bias_table(rel_bias):
    cq = jnp.arange(GRID_W)[:, None]
    ck = jnp.arange(GRID_W)[None, :]
    c0 = jnp.clip(cq - WIN_W // 2, 0, GRID_W - WIN_W)
    col_in = (ck >= c0) & (ck < c0 + WIN_W)
    idx_c = jnp.clip(ck - cq, -(WIN_W - 1), WIN_W - 1) + (WIN_W - 1)
    pick = idx_c[:, :, None] == jnp.arange(2 * WIN_W - 1)
    tz = jnp.sum(jnp.where(pick[None, None], rel_bias.astype(f32)[:, :, None, None, :], 0.0), axis=-1)
    tz = jnp.where(col_in[None, None], tz, NEG_INF)
    tab = jnp.stack([tz[:, WIN_H - 1 - dl:2 * WIN_H - 1 - dl] for dl in range(WIN_H)], axis=1)
    tab = tab.transpose(0, 1, 3, 2, 4).reshape(A_HEADS // 2, 2, WIN_H, GRID_W, WIN_H * GRID_W)
    return tab.transpose(0, 2, 1, 3, 4).reshape(A_HEADS // 2, WIN_H, 2 * GRID_W, WIN_H * GRID_W)


def _natten(qa, ka, va, bias_tab, g_q2, g_k2, batch, seq_len):
    rows = seq_len // GRID_W
    rbr = NAT_ROWS
    tq = rbr * GRID_W
    q3 = qa.reshape(batch, seq_len, A_WIDTH)
    k3 = ka.reshape(batch, seq_len, A_WIDTH)
    v3 = va.reshape(batch, seq_len, A_WIDTH)
    return pl.pallas_call(
        functools.partial(_natten_kernel, seq_len=seq_len, rb_rows=rbr),
        grid=(batch, A_HEADS // 2, rows // rbr),
        in_specs=[
            pl.BlockSpec((None, tq, LANES), lambda b, hp, rb: (b, rb, hp)),
            pl.BlockSpec((None, seq_len, LANES), lambda b, hp, rb: (b, 0, hp)),
            pl.BlockSpec((None, seq_len, LANES), lambda b, hp, rb: (b, 0, hp)),
            pl.BlockSpec((None, WIN_H, 2 * GRID_W, WIN_H * GRID_W), lambda b, hp, rb: (hp, 0, 0, 0)),
            pl.BlockSpec((1, LANES), lambda b, hp, rb: (0, 0)),
            pl.BlockSpec((1, LANES), lambda b, hp, rb: (0, 0)),
        ],
        out_specs=pl.BlockSpec((None, tq, LANES), lambda b, hp, rb: (b, rb, hp)),
        out_shape=jax.ShapeDtypeStruct((batch, seq_len, A_WIDTH), f32),
        scratch_shapes=[pltpu.VMEM((seq_len, LANES), bf16)],
        compiler_params=_cparams(("parallel", "parallel", "arbitrary")),
        name="natten",
    )(q3, k3, v3, bias_tab, g_q2, g_k2)


def _outproj_kernel(x_ref, hf_ref, hb_ref, o_ref, ha_ref, gm_ref, ga_ref, wo_ref, gf_ref,
                    wr1_ref, wr2_ref, br_ref,
                    xmid_ref, hn_ref, ids_ref, pos_ref, gate_ref, cnt_ref, carry_ref, *, tm):
    i = pl.program_id(0)

    @pl.when(i == 0)
    def _():
        carry_ref[...] = jnp.zeros_like(carry_ref)

    hm = hf_ref[...] + hb_ref[...]
    hm = hm * lax.rsqrt(_head_mean_sq(hm, M_WIDTH) + EPS) * gm_ref[...] * jax.nn.sigmoid(o_ref[...])
    ha = _rms(ha_ref[...], ga_ref[...])
    mix = (jnp.dot(hm.astype(bf16), wo_ref[0:M_WIDTH, :], preferred_element_type=f32)
           + jnp.dot(ha.astype(bf16), wo_ref[M_WIDTH:M_WIDTH + A_WIDTH, :], preferred_element_type=f32))
    xm = x_ref[...] + mix
    xmid_ref[...] = xm
    hn = _rms(xm, gf_ref[...])
    _store_token_tiles(hn_ref, hn)

    h1, h2 = _split2(hn)
    logits = (jnp.dot(h1, wr1_ref[...], preferred_element_type=f32)
              + (jnp.dot(h1, wr2_ref[...], preferred_element_type=f32)
                 + jnp.dot(h2, wr1_ref[...], preferred_element_type=f32))) + br_ref[...]
    lane = lax.broadcasted_iota(i32, (tm, LANES), 1)
    work = logits
    vals, idxs, sels = [], [], []
    for _ in range(TOP_K):
        mx = jnp.max(work, axis=-1, keepdims=True)
        idx = jnp.min(jnp.where(work == mx, lane, LANES), axis=-1, keepdims=True)
        sel = lane == idx
        vals.append(mx)
        idxs.append(idx)
        sels.append(sel)
        work = jnp.where(sel, NEG_INF, work)
    es = [jnp.exp(v - vals[0]) for v in vals]
    tot = es[0] + es[1] + es[2] + es[3]

    onehot = jnp.where(sels[0] | sels[1] | sels[2] | sels[3], 1.0, 0.0)
    tri = (lax.broadcasted_iota(i32, (tm, tm), 0) > lax.broadcasted_iota(i32, (tm, tm), 1)).astype(bf16)
    base = jnp.dot(tri, onehot.astype(bf16), preferred_element_type=f32) + carry_ref[...]
    ids_out = jnp.zeros((tm, LANES), i32)
    pos_out = jnp.zeros((tm, LANES), i32)
    gate_out = jnp.zeros((tm, LANES), f32)
    for k in range(TOP_K):
        pk = jnp.sum(jnp.where(sels[k], base, 0.0), axis=-1, keepdims=True).astype(i32)
        ids_out = jnp.where(lane == k, idxs[k], ids_out)
        pos_out = jnp.where(lane == k, pk, pos_out)
        gate_out = jnp.where(lane == k, es[k] / tot, gate_out)
    ids_ref[...] = ids_out.T[:SUBLANES, :]
    pos_ref[...] = pos_out.T[:SUBLANES, :]
    gate_ref[...] = gate_out[:, :TOP_K]
    carry_ref[...] += jnp.sum(onehot, axis=0, keepdims=True)
    cnt_ref[...] = carry_ref[...]


def _outproj(x2, hfb, o_m, ha, g_m, g_a, w_out, g_ffn, wr1, wr2, b_r, batch, seq_len):
    n = x2.shape[0]
    tm = TM_OUT
    tpb = seq_len // tm
    row = lambda i: (i, 0)
    const = lambda i: (0, 0)
    ha2 = ha.reshape(n, A_WIDTH)
    return pl.pallas_call(
        functools.partial(_outproj_kernel, tm=tm),
        grid=(n // tm,),
        in_specs=[
            pl.BlockSpec((tm, D_MODEL), row),
            pl.BlockSpec((None, None, tm, M_WIDTH), lambda i: (i // tpb, 0, i % tpb, 0)),
            pl.BlockSpec((None, None, tm, M_WIDTH), lambda i: (i // tpb, 1, i % tpb, 0)),
            pl.BlockSpec((tm, M_WIDTH), row),
            pl.BlockSpec((tm, A_WIDTH), row),
            pl.BlockSpec((1, M_WIDTH), const),
            pl.BlockSpec((1, A_WIDTH), const),
            pl.BlockSpec((M_WIDTH + A_WIDTH, D_MODEL), const),
            pl.BlockSpec((1, D_MODEL), const),
            pl.BlockSpec((D_MODEL, LANES), const),
            pl.BlockSpec((D_MODEL, LANES), const),
            pl.BlockSpec((1, LANES), const),
        ],
        out_specs=[
            pl.BlockSpec((tm, D_MODEL), row),
            pl.BlockSpec((tm * TOKEN_TILE_ROWS, LANES), row),
            pl.BlockSpec((SUBLANES, tm), lambda i: (0, i)),
            pl.BlockSpec((SUBLANES, tm), lambda i: (0, i)),
            pl.BlockSpec((tm, TOP_K), row),
            pl.BlockSpec((1, LANES), const),
        ],
        out_shape=(
            jax.ShapeDtypeStruct((n, D_MODEL), f32),
            jax.ShapeDtypeStruct((n * TOKEN_TILE_ROWS, LANES), f32),
            jax.ShapeDtypeStruct((SUBLANES, n), i32),
            jax.ShapeDtypeStruct((SUBLANES, n), i32),
            jax.ShapeDtypeStruct((n, TOP_K), f32),
            jax.ShapeDtypeStruct((1, LANES), f32),
        ),
        scratch_shapes=[pltpu.VMEM((1, LANES), f32)],
        compiler_params=_cparams(("arbitrary",)),
        name="outproj_router",
    )(x2, hfb, hfb, o_m, ha2, g_m, g_a, w_out, g_ffn, wr1, wr2, b_r)


def _dispatch_kernel(pend_ref, padded_ref, slot_ref, hn_ref, xb_hbm, zero_ref, sem, *, tm, tm_e, n_blocks):
    i = pl.program_id(0)

    blk = tm_e * TOKEN_TILE_ROWS

    def zero_copy(e):
        start = (pend_ref[e] - tm_e) * TOKEN_TILE_ROWS
        return pltpu.make_async_copy(zero_ref, xb_hbm.at[pl.ds(pl.multiple_of(start, blk), blk)], sem)

    def tail_copy(b):
        return pltpu.make_async_copy(zero_ref, xb_hbm.at[pl.ds(pl.multiple_of(b * blk, blk), blk)], sem)

    @pl.when(i == 0)
    def _():
        zero_ref[...] = jnp.zeros_like(zero_ref)
        for e in range(N_EXPERTS):
            @pl.when(padded_ref[e] > 0)
            def _():
                zero_copy(e).start()
        used = pend_ref[N_EXPERTS - 1] // tm_e

        def tail_start(b, carry):
            tail_copy(b).start()
            return carry

        def tail_wait(b, carry):
            tail_copy(b).wait()
            return carry

        lax.fori_loop(used, n_blocks, tail_start, 0)
        for e in range(N_EXPERTS):
            @pl.when(padded_ref[e] > 0)
            def _():
                zero_copy(e).wait()
        lax.fori_loop(used, n_blocks, tail_wait, 0)

    def row(j, k):
        return pltpu.make_async_copy(_token_tile(hn_ref, j), _token_tile(xb_hbm, slot_ref[0, 0, j * TOP_K + k]), sem)

    def issue(j, carry):
        for k in range(TOP_K):
            row(j, k).start(priority=k % 2)
        return carry

    lax.fori_loop(0, tm, issue, 0, unroll=4)

    def drain(j, carry):
        for k in range(TOP_K):
            row(0, k).wait()
        return carry

    lax.fori_loop(0, tm, drain, 0, unroll=4)


def _dispatch(hn8, slot_flat, pend, padded, cap):
    n = hn8.shape[0] // TOKEN_TILE_ROWS
    tm = TM_DISPATCH
    slot3 = slot_flat.reshape(n // tm, 1, tm * TOP_K)
    grid_spec = pltpu.PrefetchScalarGridSpec(
        num_scalar_prefetch=2,
        grid=(n // tm,),
        in_specs=[
            pl.BlockSpec((1, 1, tm * TOP_K), lambda i, pe, pa: (i, 0, 0), memory_space=pltpu.SMEM),
            pl.BlockSpec((tm * TOKEN_TILE_ROWS, LANES), lambda i, pe, pa: (i, 0)),
        ],
        out_specs=pl.BlockSpec(memory_space=pl.ANY),
        scratch_shapes=[pltpu.VMEM((TM_EXP * TOKEN_TILE_ROWS, LANES), f32), pltpu.SemaphoreType.DMA(())],
    )
    return pl.pallas_call(
        functools.partial(_dispatch_kernel, tm=tm, tm_e=TM_EXP, n_blocks=cap // TM_EXP),
        grid_spec=grid_spec,
        out_shape=jax.ShapeDtypeStruct((cap * TOKEN_TILE_ROWS, LANES), f32),
        compiler_params=_cparams(("arbitrary",)),
        name="dispatch",
    )(pend, padded, slot3, hn8)


def _experts_kernel(be_ref, nv_ref, xb_ref, wgu_ref, bgu_ref, wd_ref, bd_ref, yb_ref, *, tm):
    j = pl.program_id(0)

    @pl.when(j < nv_ref[0])
    def _():
        xv = _load_token_tiles(xb_ref, tm).astype(bf16)
        h = jnp.dot(xv, wgu_ref[...], preferred_element_type=f32) + bgu_ref[...]
        gt = jnp.minimum(h[:, :D_FF], SWIGLU_LIMIT)
        up = jnp.clip(h[:, D_FF:], -SWIGLU_LIMIT, SWIGLU_LIMIT)
        act = (up + 1.0) * (gt * jax.nn.sigmoid(SWIGLU_ALPHA * gt))
        _store_token_tiles(yb_ref, jnp.dot(act.astype(bf16), wd_ref[...], preferred_element_type=f32) + bd_ref[...])

    @pl.when(j >= nv_ref[0])
    def _():
        yb_ref[...] = jnp.zeros_like(yb_ref)


def _experts(xb, block_e, nvalid, w_gu, b_gu, w_d, b_d):
    cap = xb.shape[0] // TOKEN_TILE_ROWS
    tm = TM_EXP
    grid_spec = pltpu.PrefetchScalarGridSpec(
        num_scalar_prefetch=2,
        grid=(cap // tm,),
        in_specs=[
            pl.BlockSpec((tm * TOKEN_TILE_ROWS, LANES), lambda j, be, nv: (jnp.minimum(j, nv[0] - 1), 0)),
            pl.BlockSpec((None, D_MODEL, 2 * D_FF), lambda j, be, nv: (be[j], 0, 0)),
            pl.BlockSpec((None, 1, 2 * D_FF), lambda j, be, nv: (be[j], 0, 0)),
            pl.BlockSpec((None, D_FF, D_MODEL), lambda j, be, nv: (be[j], 0, 0)),
            pl.BlockSpec((None, 1, D_MODEL), lambda j, be, nv: (be[j], 0, 0)),
        ],
        out_specs=pl.BlockSpec((tm * TOKEN_TILE_ROWS, LANES), lambda j, be, nv: (j, 0)),
    )
    return pl.pallas_call(
        functools.partial(_experts_kernel, tm=tm),
        grid_spec=grid_spec,
        out_shape=jax.ShapeDtypeStruct((cap * TOKEN_TILE_ROWS, LANES), f32),
        compiler_params=_cparams(("arbitrary",)),
        name="experts",
    )(block_e, nvalid, xb, w_gu, b_gu, w_d, b_d)


def _combine_kernel(slot_ref, slot_next_ref, xmid_ref, gate_ref, p_ref, wproj_ref, gpost_ref, gple_ref, wgate_ref,
                    yb_hbm, out_ref, ybuf_ref, y_ref, sem, *, tm):
    i = pl.program_id(0)
    cur = lax.rem(i, 2)
    nxt = 1 - cur

    def row(slots, buf, j, k):
        return pltpu.make_async_copy(_token_tile(yb_hbm, slots[0, 0, j * TOP_K + k]),
                                     _token_tile(ybuf_ref.at[buf, k], j), sem.at[buf])

    def drain(buf):
        def wait(j, carry):
            for k in range(TOP_K):
                row(slot_ref, buf, 0, k).wait()
            return carry
        lax.fori_loop(0, tm, wait, 0, unroll=4)

    @pl.when(i == 0)
    def _():
        def issue(j, carry):
            for k in range(TOP_K):
                row(slot_ref, 0, j, k).start(priority=k % 2)
            return carry
        lax.fori_loop(0, tm, issue, 0, unroll=4)

    pe = _rms(jnp.dot(p_ref[...].astype(bf16), wproj_ref[...], preferred_element_type=f32), gpost_ref[...])
    drain(cur)

    def chunk_body(cb, carry):
        j0 = pl.multiple_of(cb * CMB_CHUNK, CMB_CHUNK)
        for jj in range(CMB_CHUNK):
            for k in range(TOP_K):
                row(slot_next_ref, nxt, j0 + jj, k).start(priority=k % 2)
        gate = gate_ref[pl.ds(j0, CMB_CHUNK), :]
        acc = gate[:, 0:1] * _load_token_tiles(ybuf_ref.at[cur, 0], CMB_CHUNK, j0)
        for k in range(1, TOP_K):
            acc = acc + gate[:, k:k + 1] * _load_token_tiles(ybuf_ref.at[cur, k], CMB_CHUNK, j0)
        y_ref[pl.ds(j0, CMB_CHUNK), :] = acc
        return carry

    lax.fori_loop(0, tm // CMB_CHUNK, chunk_body, 0)

    @pl.when(i + 1 == pl.num_programs(0))
    def _():
        drain(nxt)

    x2 = xmid_ref[...] + y_ref[...]
    gl = jnp.dot(_rms(x2, gple_ref[...]).astype(bf16), wgate_ref[...], preferred_element_type=f32)
    out_ref[...] = x2 + jax.nn.sigmoid(gl) * pe


def _combine(slot_flat, xmid, gate, p2, w_proj, g_post, g_ple, w_gate, yb):
    n = xmid.shape[0]
    tm = TM_CMB
    steps = n // tm
    slot3 = slot_flat.reshape(steps, 1, tm * TOP_K)
    row = lambda i: (i, 0)
    const = lambda i: (0, 0)
    return pl.pallas_call(
        functools.partial(_combine_kernel, tm=tm),
        grid=(steps,),
        in_specs=[
            pl.BlockSpec((1, 1, tm * TOP_K), lambda i: (i, 0, 0), memory_space=pltpu.SMEM),
            pl.BlockSpec((1, 1, tm * TOP_K), lambda i: (jnp.minimum(i + 1, steps - 1), 0, 0),
                         memory_space=pltpu.SMEM),
            pl.BlockSpec((tm, D_MODEL), row),
            pl.BlockSpec((tm, TOP_K), row),
            pl.BlockSpec((tm, PLE_DIM), row),
            pl.BlockSpec((PLE_DIM, D_MODEL), const),
            pl.BlockSpec((1, D_MODEL), const),
            pl.BlockSpec((1, D_MODEL), const),
            pl.BlockSpec((D_MODEL, D_MODEL), const),
            pl.BlockSpec(memory_space=pl.ANY),
        ],
        out_specs=pl.BlockSpec((tm, D_MODEL), row),
        out_shape=jax.ShapeDtypeStruct((n, D_MODEL), f32),
        scratch_shapes=[pltpu.VMEM((2, TOP_K, tm * TOKEN_TILE_ROWS, LANES), f32), pltpu.VMEM((tm, D_MODEL), f32),
                        pltpu.SemaphoreType.DMA((2,))],
        compiler_params=_cparams(("arbitrary",)),
        name="combine_ple",
    )(slot3, slot3, xmid, gate, p2, w_proj, g_post, g_ple, w_gate, yb)


def _wprep_kernel(w_ref, out_ref):
    grp = 2 * LANES
    src = lax.broadcasted_iota(i32, (grp, grp), 0)
    dst = lax.broadcasted_iota(i32, (grp, grp), 1)
    want = jnp.where(dst < LANES, 2 * dst, 2 * (dst - LANES) + 1)
    perm = (src == want).astype(bf16)
    for g in range(2 * D_FF // grp):
        t = jnp.dot(w_ref[:, g * grp:(g + 1) * grp].astype(bf16), perm, preferred_element_type=f32)
        out_ref[:, g * LANES:(g + 1) * LANES] = t[:, :LANES].astype(bf16)
        out_ref[:, D_FF + g * LANES:D_FF + (g + 1) * LANES] = t[:, LANES:].astype(bf16)


def _wprep(w_gate_up):
    ne = w_gate_up.shape[0]
    tr = TR_WPREP
    return pl.pallas_call(
        _wprep_kernel,
        grid=(ne, D_MODEL // tr),
        in_specs=[pl.BlockSpec((None, tr, 2 * D_FF), lambda e, r: (e, r, 0))],
        out_specs=pl.BlockSpec((None, tr, 2 * D_FF), lambda e, r: (e, r, 0)),
        out_shape=jax.ShapeDtypeStruct((ne, D_MODEL, 2 * D_FF), bf16),
        compiler_params=_cparams(("parallel", "parallel")),
        name="wprep",
    )(w_gate_up)


def _prep_weights(g_mix, w_in, b_gates, conv_w, g_m_head, g_q, g_k, rel_bias, g_a_out, w_out, g_ffn,
                  w_router, b_router, w_gate_up, b_gate_up, w_down, b_down, g_ple, w_ple_gate, w_ple_proj,
                  g_ple_post):
    g0 = 4 * M_WIDTH
    w = w_in[0]
    pw = {}
    pw["g_mix"] = g_mix[0][None, :]
    pw["w_main"] = jnp.concatenate([w[:, :g0], w[:, g0 + N_GATES:]], axis=1).astype(bf16)
    pw["w_gate"] = jnp.pad(w[:, g0:g0 + N_GATES], ((0, 0), (0, LANES - N_GATES))).astype(bf16)
    pw["b_gate"] = jnp.pad(b_gates[0], (0, LANES - N_GATES))[None, :]
    pw["conv_w"] = conv_w[0]
    pw["g_m"] = g_m_head[0].reshape(1, M_WIDTH)
    pw["g_q2"] = jnp.tile(g_q[0], 2)[None, :]
    pw["g_k2"] = jnp.tile(g_k[0], 2)[None, :]
    pw["bias_tab"] = _natten_bias_table(rel_bias[0])
    pw["g_a"] = g_a_out[0][None, :]
    pw["w_out"] = w_out[0].astype(bf16)
    pw["g_ffn"] = g_ffn[0][None, :]
    wr = jnp.pad(w_router[0], ((0, 0), (0, LANES - N_EXPERTS)))
    wr1 = wr.astype(bf16)
    pw["wr1"] = wr1
    pw["wr2"] = (wr - wr1.astype(f32)).astype(bf16)
    pw["b_r"] = jnp.pad(b_router[0], (0, LANES - N_EXPERTS), constant_values=NEG_INF)[None, :]
    pw["w_gu"] = _wprep(w_gate_up[0])
    bgu = b_gate_up[0]
    pw["b_gu"] = jnp.concatenate([bgu[:, 0::2], bgu[:, 1::2]], axis=-1)[:, None, :]
    pw["w_d"] = w_down[0].astype(bf16)
    pw["b_d"] = b_down[0][:, None, :]
    pw["g_ple"] = g_ple[0][None, :]
    pw["w_ple_gate"] = w_ple_gate[0].astype(bf16)
    pw["w_ple_proj"] = w_ple_proj[0].astype(bf16)
    pw["g_ple_post"] = g_ple_post[0][None, :]
    return pw


def _trunk(x, p, pw):
    batch, seq_len, _ = x.shape
    n = batch * seq_len
    x2 = x.reshape(n, D_MODEL)
    qk, v_m, o_m, gates, qa, ka, va = _inproj(x2, pw["g_mix"], pw["w_main"], pw["w_gate"], pw["b_gate"],
                                              pw["conv_w"], seq_len)
    hfb = _mlstm(qk, v_m, gates, batch, seq_len)
    ha = _natten(qa, ka, va, pw["bias_tab"], pw["g_q2"], pw["g_k2"], batch, seq_len)
    xmid, hn, ids, pos, gate, cnt = _outproj(x2, hfb, o_m, ha, pw["g_m"], pw["g_a"], pw["w_out"], pw["g_ffn"],
                                             pw["wr1"], pw["wr2"], pw["b_r"], batch, seq_len)

    tm_e = TM_EXP
    counts = cnt[0, :N_EXPERTS].astype(i32)
    padded = (counts + tm_e - 1) // tm_e * tm_e
    pend = jnp.cumsum(padded).astype(i32)
    pstart = pend - padded
    ids_t = ids[:TOP_K, :]
    first = jnp.zeros_like(ids_t)
    for e in range(N_EXPERTS):
        first = jnp.where(ids_t == e, pstart[e], first)
    slot = (first + pos[:TOP_K, :]).T.reshape(-1)
    nk = n * TOP_K
    n_blocks = (nk + N_EXPERTS * (tm_e - 1) + tm_e - 1) // tm_e
    cap = n_blocks * tm_e
    block_start = jnp.arange(n_blocks, dtype=i32) * tm_e
    block_e = jnp.minimum(jnp.sum((pend[None, :] <= block_start[:, None]).astype(i32), axis=1), N_EXPERTS - 1)
    nvalid = (pend[-1:] // tm_e).astype(i32)

    xb = _dispatch(hn, slot, pend, padded, cap)
    yb = _experts(xb, block_e, nvalid, pw["w_gu"], pw["b_gu"], pw["w_d"], pw["b_d"])
    out = _combine(slot, xmid, gate, p.reshape(n, PLE_DIM), pw["w_ple_proj"], pw["g_ple_post"], pw["g_ple"],
                   pw["w_ple_gate"], yb)
    return out.reshape(batch, seq_len, D_MODEL)


def kernel(x_prompt, x_sample, p_prompt, p_sample, g_mix, w_in, b_gates, conv_w, g_m_head, g_q, g_k, rel_bias,
           g_a_out, w_out, g_ffn, w_router, b_router, w_gate_up, b_gate_up, w_down, b_down, g_ple, w_ple_gate,
           w_ple_proj, g_ple_post):
    assert w_in.shape[0] == 1, "single-layer trunk"
    pw = _prep_weights(g_mix, w_in, b_gates, conv_w, g_m_head, g_q, g_k, rel_bias, g_a_out, w_out, g_ffn,
                       w_router, b_router, w_gate_up, b_gate_up, w_down, b_down, g_ple, w_ple_gate, w_ple_proj,
                       g_ple_post)
    y_prompt = _trunk(x_prompt, p_prompt[0], pw)
    y_sample = _trunk(x_sample, p_sample[0], pw)
    return (y_prompt, y_sample)
```

```python
import functools

import jax
import jax.numpy as jnp
from jax import lax
from jax.experimental import pallas as pl
from jax.experimental.pallas import tpu as pltpu

f32 = jnp.float32
bf16 = jnp.bfloat16
i32 = jnp.int32

D_MODEL = 1024
HEAD_DIM = 64
M_HEADS = 8
A_HEADS = 8
M_WIDTH = M_HEADS * HEAD_DIM
A_WIDTH = A_HEADS * HEAD_DIM
N_GATES = 4 * M_HEADS
GRID_W = 64
WIN_H = 8
WIN_W = 16
N_EXPERTS = 32
TOP_K = 4
D_FF = 1024
SWIGLU_LIMIT = 7.0
SWIGLU_ALPHA = 1.702
PLE_DIM = 256
EPS = 1e-6

LANES = 128
SUBLANES = 8
MXU_TILE = 256
VMEM_LIMIT = 56 * 1024 * 1024

TM_IN = 512
MLSTM_CHUNK = 256
NAT_ROWS = 16
TM_OUT = 512
TM_EXP = 512
TM_DISPATCH = 256
TM_CMB = 256
CMB_CHUNK = 32

NEG_INF = float("-inf")


def _cparams(sem):
    return pltpu.CompilerParams(dimension_semantics=sem, vmem_limit_bytes=VMEM_LIMIT)


def _rms(xv, g):
    return xv * lax.rsqrt(jnp.mean(xv * xv, axis=-1, keepdims=True) + EPS) * g


def _split2(a):
    hi = a.astype(bf16)
    lo = (a - hi.astype(f32)).astype(bf16)
    return hi, lo


def _split3(a):
    hi = a.astype(bf16)
    r = a - hi.astype(f32)
    mid = r.astype(bf16)
    lo = (r - mid.astype(f32)).astype(bf16)
    return hi, mid, lo


def _head_mean_sq(xv, width):
    grp = min(width, MXU_TILE)
    a = lax.broadcasted_iota(i32, (grp, grp), 0) // HEAD_DIM
    b = lax.broadcasted_iota(i32, (grp, grp), 1) // HEAD_DIM
    bd = jnp.where(a == b, 1.0 / HEAD_DIM, 0.0).astype(bf16)
    hi, lo = _split2(xv * xv)
    parts = []
    for g in range(width // grp):
        sl = slice(g * grp, (g + 1) * grp)
        parts.append(jnp.dot(hi[:, sl], bd, preferred_element_type=f32)
                     + jnp.dot(lo[:, sl], bd, preferred_element_type=f32))
    return parts[0] if len(parts) == 1 else jnp.concatenate(parts, axis=1)


TOKEN_TILE_ROWS = D_MODEL // LANES


def _store_token_tiles(ref, val):
    n = val.shape[0]
    for c in range(TOKEN_TILE_ROWS):
        ref[pl.ds(c, n, stride=TOKEN_TILE_ROWS), :] = val[:, c * LANES:(c + 1) * LANES]


def _token_tile(ref, t):
    return ref.at[pl.ds(pl.multiple_of(t * TOKEN_TILE_ROWS, TOKEN_TILE_ROWS), TOKEN_TILE_ROWS)]


def _load_token_tiles(ref, n, first=0):
    base = first * TOKEN_TILE_ROWS
    return jnp.concatenate([ref[pl.ds(base + c, n, stride=TOKEN_TILE_ROWS), :] for c in range(TOKEN_TILE_ROWS)],
                           axis=1)


def _log_sigmoid(x):
    return jnp.minimum(x, 0.0) - jnp.log1p(jnp.exp(-jnp.abs(x)))


def _inproj_kernel(x_ref, xp_ref, xn_ref, g_ref, wm_ref, wg_ref, bg_ref, cw_ref,
                   qk_ref, v_ref, o_ref, gates_ref, qa_ref, ka_ref, va_ref, *, tm, seq_len):
    i = pl.program_id(0)
    hf = _rms(jnp.concatenate([xp_ref[...], x_ref[...], xn_ref[...]], axis=0), g_ref[...])
    ext = tm + 2 * SUBLANES
    h = hf[SUBLANES:SUBLANES + tm, :].astype(bf16)
    zq = jnp.dot(hf.astype(bf16), wm_ref[:, :2 * M_WIDTH], preferred_element_type=f32)
    z = jnp.dot(h, wm_ref[:, 2 * M_WIDTH:], preferred_element_type=f32)
    zg = jnp.dot(h, wg_ref[...], preferred_element_type=f32) + bg_ref[...]
    gates_ref[...] = zg.T[:N_GATES, :]
    start = lax.rem(i * tm, seq_len)
    rid = lax.broadcasted_iota(i32, (tm, 1), 0)
    u = zq[SUBLANES:SUBLANES + tm, :]
    u_prev = pltpu.roll(zq, 1, 0)[SUBLANES:SUBLANES + tm, :]
    u_next = pltpu.roll(zq, ext - 1, 0)[SUBLANES:SUBLANES + tm, :]
    u_prev = jnp.where(jnp.logical_and(rid == 0, start == 0), 0.0, u_prev)
    u_next = jnp.where(jnp.logical_and(rid == tm - 1, start + tm == seq_len), 0.0, u_next)
    cw = cw_ref[...]
    c = u_prev * cw[0:1, :] + u * cw[1:2, :] + u_next * cw[2:3, :]
    qk_ref[...] = (c * jax.nn.sigmoid(c)).astype(bf16)
    v_ref[...] = z[:, 0:M_WIDTH].T.astype(bf16)
    o_ref[...] = z[:, M_WIDTH:2 * M_WIDTH]
    base = 2 * M_WIDTH
    qa_ref[...] = z[:, base:base + A_WIDTH]
    ka_ref[...] = z[:, base + A_WIDTH:base + 2 * A_WIDTH]
    va_ref[...] = z[:, base + 2 * A_WIDTH:base + 3 * A_WIDTH].astype(bf16)


def _inproj(x2, g_mix, w_main, w_gate, b_gate, conv_w, seq_len):
    n = x2.shape[0]
    tm = TM_IN
    nb8 = n // SUBLANES
    r8 = tm // SUBLANES
    wcols = w_main.shape[1]
    row = lambda i: (i, 0)
    const = lambda i: (0, 0)
    out_shapes = (
        jax.ShapeDtypeStruct((n, 2 * M_WIDTH), bf16),
        jax.ShapeDtypeStruct((M_WIDTH, n), bf16),
        jax.ShapeDtypeStruct((n, M_WIDTH), f32),
        jax.ShapeDtypeStruct((N_GATES, n), f32),
        jax.ShapeDtypeStruct((n, A_WIDTH), f32),
        jax.ShapeDtypeStruct((n, A_WIDTH), f32),
        jax.ShapeDtypeStruct((n, A_WIDTH), bf16),
    )
    return pl.pallas_call(
        functools.partial(_inproj_kernel, tm=tm, seq_len=seq_len),
        grid=(n // tm,),
        in_specs=[
            pl.BlockSpec((tm, D_MODEL), row),
            pl.BlockSpec((SUBLANES, D_MODEL), lambda i: (jnp.maximum(i * r8 - 1, 0), 0)),
            pl.BlockSpec((SUBLANES, D_MODEL), lambda i: (jnp.minimum((i + 1) * r8, nb8 - 1), 0)),
            pl.BlockSpec((1, D_MODEL), const),
            pl.BlockSpec((D_MODEL, wcols), const),
            pl.BlockSpec((D_MODEL, LANES), const),
            pl.BlockSpec((1, LANES), const),
            pl.BlockSpec((3, 2 * M_WIDTH), const),
        ],
        out_specs=[
            pl.BlockSpec((tm, 2 * M_WIDTH), row),
            pl.BlockSpec((M_WIDTH, tm), lambda i: (0, i)),
            pl.BlockSpec((tm, M_WIDTH), row),
            pl.BlockSpec((N_GATES, tm), lambda i: (0, i)),
            pl.BlockSpec((tm, A_WIDTH), row),
            pl.BlockSpec((tm, A_WIDTH), row),
            pl.BlockSpec((tm, A_WIDTH), row),
        ],
        out_shape=out_shapes,
        compiler_params=_cparams(("parallel",)),
        name="inproj",
    )(x2, x2, x2, g_mix, w_main, w_gate, b_gate, conv_w)


def _mlstm_kernel(q_ref, k_ref, vt_ref, g_ref, out_ref, c_ref, m_ref, *, L):
    d = pl.program_id(1)
    c = pl.program_id(2)

    @pl.when(c == 0)
    def _():
        c_ref[...] = jnp.zeros_like(c_ref)
        m_ref[...] = jnp.zeros_like(m_ref)

    fwd = d == 0
    sgn = jnp.where(fwd, 1, -1)
    row = lax.broadcasted_iota(i32, (L, L), 0)
    col = lax.broadcasted_iota(i32, (L, L), 1)
    vis = (col - row) * sgn >= 0
    t_row = vis.astype(bf16)

    g = g_ref[...]
    h8 = M_HEADS
    li = jnp.where(fwd, g[0:h8, :], g[2 * h8:3 * h8, :])
    lf = _log_sigmoid(jnp.where(fwd, g[h8:2 * h8, :], g[3 * h8:4 * h8, :]))
    b = sum(jnp.dot(part, t_row, preferred_element_type=f32) for part in _split3(lf))
    bl = jnp.sum(lf, axis=1, keepdims=True)
    r_row = li - b
    pos = lax.broadcasted_iota(i32, (1, L), 1)
    cm = r_row
    sh = 1
    while sh < L:
        from_before = jnp.where(pos >= sh, pltpu.roll(cm, sh, 1), NEG_INF)
        from_after = jnp.where(pos < L - sh, pltpu.roll(cm, L - sh, 1), NEG_INF)
        cm = jnp.maximum(cm, jnp.where(fwd, from_before, from_after))
        sh *= 2
    m_prev = m_ref[:, 0:1]
    mx_r = jnp.maximum(m_prev, cm)
    m_new = bl + jnp.maximum(m_prev, jnp.max(r_row, axis=1, keepdims=True))
    decay = jnp.exp(bl + m_prev - m_new)
    m_ref[...] = jnp.broadcast_to(m_new, (h8, LANES))
    w_int = jnp.exp(m_prev - mx_r)
    floor = jnp.exp(-(b + mx_r))
    wk = jnp.exp(bl + r_row - m_new)
    r_col = jnp.concatenate([r_row, jnp.zeros((LANES - h8, L), f32)], axis=0).T

    lane = lax.broadcasted_iota(i32, (1, LANES), 1)
    sub = lax.broadcasted_iota(i32, (LANES, 1), 0)
    lane_half = (lane < HEAD_DIM, lane >= HEAD_DIM)
    sub_half = (sub < HEAD_DIM, sub >= HEAD_DIM)
    nt = (((1,), (1,)), ((), ()))

    heads = range(M_HEADS)
    pair = lambda h: slice((h // 2) * LANES, (h // 2 + 1) * LANES)
    qz = [jnp.where(lane_half[h % 2], q_ref[:, pair(h)] * (HEAD_DIM ** -0.5), 0).astype(bf16) for h in heads]
    kz = [jnp.where(lane_half[h % 2], k_ref[:, pair(h)], 0).astype(bf16) for h in heads]
    va_t = [jnp.where(sub_half[h % 2], vt_ref[pair(h), :], 1).astype(bf16) for h in heads]
    kq = [lax.dot_general(kz[h], qz[h], nt, preferred_element_type=f32) for h in heads]
    s_t = [jnp.where(vis, kq[h] * jnp.exp(r_col[:, h:h + 1] - mx_r[h:h + 1, :]), 0.0).astype(bf16) for h in heads]
    cst = [c_ref[h] for h in heads]
    nums = [w_int[h:h + 1, :] * lax.dot_general(cst[h].astype(bf16), qz[h], nt, preferred_element_type=f32)
            + jnp.dot(va_t[h], s_t[h], preferred_element_type=f32) for h in heads]
    for h in heads:
        vw = (va_t[h].astype(f32) * wk[h:h + 1, :]).astype(bf16)
        c_ref[h] = decay[h:h + 1, :] * cst[h] + jnp.dot(vw, kz[h], preferred_element_type=f32)
    for p in range(M_HEADS // 2):
        ev, od = 2 * p, 2 * p + 1
        numer = jnp.where(sub_half[0], nums[ev], nums[od])
        den = jnp.where(sub_half[0], nums[ev][HEAD_DIM:HEAD_DIM + 1, :], nums[od][0:1, :])
        lim = jnp.where(sub_half[0], floor[ev:ev + 1, :], floor[od:od + 1, :])
        out_ref[:, pair(ev)] = (numer / jnp.maximum(jnp.abs(den), lim)).T


def _mlstm(qk, v_t, gates_t, batch, seq_len):
    L = MLSTM_CHUNK
    nc = seq_len // L
    qk3 = qk.reshape(batch, seq_len, 2 * M_WIDTH)
    cidx = lambda d, c: jnp.where(d == 0, c, nc - 1 - c)
    return pl.pallas_call(
        functools.partial(_mlstm_kernel, L=L),
        grid=(batch, 2, nc),
        in_specs=[
            pl.BlockSpec((None, L, M_WIDTH), lambda b, d, c: (b, cidx(d, c), 0)),
            pl.BlockSpec((None, L, M_WIDTH), lambda b, d, c: (b, cidx(d, c), 1)),
            pl.BlockSpec((M_WIDTH, L), lambda b, d, c: (0, b * nc + cidx(d, c))),
            pl.BlockSpec((N_GATES, L), lambda b, d, c: (0, b * nc + cidx(d, c))),
        ],
        out_specs=pl.BlockSpec((None, None, L, M_WIDTH), lambda b, d, c: (b, d, cidx(d, c), 0)),
        out_shape=jax.ShapeDtypeStruct((batch, 2, seq_len, M_WIDTH), f32),
        scratch_shapes=[pltpu.VMEM((M_HEADS, LANES, LANES), f32), pltpu.VMEM((M_HEADS, LANES), f32)],
        compiler_params=_cparams(("parallel", "parallel", "arbitrary")),
        name="mlstm",
    )(qk3, qk3, v_t, gates_t)


def _natten_kernel(q_ref, k_ref, v_ref, bias_ref, gq_ref, gk_ref, out_ref, kn_ref, *, seq_len, rb_rows):
    rb = pl.program_id(2)
    rows = seq_len // GRID_W
    norm_rows = 512

    @pl.when(rb == 0)
    def _():
        def body(i, carry):
            sl = pl.ds(pl.multiple_of(i * norm_rows, norm_rows), norm_rows)
            kv = k_ref[sl, :]
            kn_ref[sl, :] = (kv * lax.rsqrt(_head_mean_sq(kv, LANES) + EPS) * gk_ref[...]).astype(bf16)
            return carry
        lax.fori_loop(0, seq_len // norm_rows, body, 0)

    qv = q_ref[...]
    qn = (qv * lax.rsqrt(_head_mean_sq(qv, LANES) + EPS) * gq_ref[...] * (HEAD_DIM ** -0.5)).astype(bf16)
    lane = lax.broadcasted_iota(i32, (1, LANES), 1)
    lo = lane < HEAD_DIM
    nt = (((1,), (1,)), ((), ()))
    pair = 2 * GRID_W

    def window(j):
        r = rb * rb_rows + j
        r0 = jnp.clip(r - WIN_H // 2, 0, rows - WIN_H)
        return r - r0, pl.ds(pl.multiple_of(r0 * GRID_W, GRID_W), WIN_H * GRID_W)

    tiles = []
    for j in range(rb_rows):
        delta, ks = window(j)
        qj = qn[j * GRID_W:(j + 1) * GRID_W, :]
        q2 = jnp.concatenate([jnp.where(lo, qj, jnp.zeros_like(qj)), jnp.where(lo, jnp.zeros_like(qj), qj)], axis=0)
        tiles.append(lax.dot_general(q2, kn_ref[ks, :], nt, preferred_element_type=f32) + bias_ref[delta])
    s = jnp.concatenate(tiles, axis=0)
    p = jnp.exp(s - jnp.max(s, axis=-1, keepdims=True))
    inv = 1.0 / jnp.sum(p, axis=-1, keepdims=True)
    pb = p.astype(bf16)
    for j in range(rb_rows):
        _, ks = window(j)
        o = jnp.dot(pb[j * pair:(j + 1) * pair, :], v_ref[ks, :], preferred_element_type=f32)
        o = o * inv[j * pair:(j + 1) * pair, :]
        out_ref[j * GRID_W:(j + 1) * GRID_W, :] = jnp.where(lo, o[:GRID_W, :], o[GRID_W:, :])


def _natten_bias_table(rel_bias):
    cq = jnp.arange(GRID_W)[:, None]
    ck = jnp.arange(GRID_W)[None, :]
    c0 = jnp.clip(cq - WIN_W // 2, 0, GRID_W - WIN_W)
    col_in = (ck >= c0) & (ck < c0 + WIN_W)
    idx_c = jnp.clip(ck - cq, -(WIN_W - 1), WIN_W - 1) + (WIN_W - 1)
    pick = idx_c[:, :, None] == jnp.arange(2 * WIN_W - 1)
    tz = jnp.sum(jnp.where(pick[None, None], rel_bias.astype(f32)[:, :, None, None, :], 0.0), axis=-1)
    tz = jnp.where(col_in[None, None], tz, NEG_INF)
    tab = jnp.stack([tz[:, WIN_H - 1 - dl:2 * WIN_H - 1 - dl] for dl in range(WIN_H)], axis=1)
    tab = tab.transpose(0, 1, 3, 2, 4).reshape(A_HEADS // 2, 2, WIN_H, GRID_W, WIN_H * GRID_W)
    return tab.transpose(0, 2, 1, 3, 4).reshape(A_HEADS // 2, WIN_H, 2 * GRID_W, WIN_H * GRID_W)


def _natten(qa, ka, va, bias_tab, g_q2, g_k2, batch, seq_len):
    rows = seq_len // GRID_W
    rbr = NAT_ROWS
    tq = rbr * GRID_W
    q3 = qa.reshape(batch, seq_len, A_WIDTH)
    k3 = ka.reshape(batch, seq_len, A_WIDTH)
    v3 = va.reshape(batch, seq_len, A_WIDTH)
    return pl.pallas_call(
        functools.partial(_natten_kernel, seq_len=seq_len, rb_rows=rbr),
        grid=(batch, A_HEADS // 2, rows // rbr),
        in_specs=[
            pl.BlockSpec((None, tq, LANES), lambda b, hp, rb: (b, rb, hp)),
            pl.BlockSpec((None, seq_len, LANES), lambda b, hp, rb: (b, 0, hp)),
            pl.BlockSpec((None, seq_len, LANES), lambda b, hp, rb: (b, 0, hp)),
            pl.BlockSpec((None, WIN_H, 2 * GRID_W, WIN_H * GRID_W), lambda b, hp, rb: (hp, 0, 0, 0)),
            pl.BlockSpec((1, LANES), lambda b, hp, rb: (0, 0)),
            pl.BlockSpec((1, LANES), lambda b, hp, rb: (0, 0)),
        ],
        out_specs=pl.BlockSpec((None, tq, LANES), lambda b, hp, rb: (b, rb, hp)),
        out_shape=jax.ShapeDtypeStruct((batch, seq_len, A_WIDTH), f32),
        scratch_shapes=[pltpu.VMEM((seq_len, LANES), bf16)],
        compiler_params=_cparams(("parallel", "parallel", "arbitrary")),
        name="natten",
    )(q3, k3, v3, bias_tab, g_q2, g_k2)


def _outproj_kernel(x_ref, hf_ref, hb_ref, o_ref, ha_ref, gm_ref, ga_ref, wo_ref, gf_ref,
                    wr1_ref, wr2_ref, br_ref,
                    xmid_ref, hn_ref, ids_ref, pos_ref, gate_ref, cnt_ref, carry_ref, *, tm):
    i = pl.program_id(0)

    @pl.when(i == 0)
    def _():
        carry_ref[...] = jnp.zeros_like(carry_ref)

    hm = hf_ref[...] + hb_ref[...]
    hm = hm * lax.rsqrt(_head_mean_sq(hm, M_WIDTH) + EPS) * gm_ref[...] * jax.nn.sigmoid(o_ref[...])
    ha = _rms(ha_ref[...], ga_ref[...])
    mix = (jnp.dot(hm.astype(bf16), wo_ref[0:M_WIDTH, :], preferred_element_type=f32)
           + jnp.dot(ha.astype(bf16), wo_ref[M_WIDTH:M_WIDTH + A_WIDTH, :], preferred_element_type=f32))
    xm = x_ref[...] + mix
    xmid_ref[...] = xm
    hn = _rms(xm, gf_ref[...])
    _store_token_tiles(hn_ref, hn)

    h1, h2 = _split2(hn)
    logits = (jnp.dot(h1, wr1_ref[...], preferred_element_type=f32)
              + (jnp.dot(h1, wr2_ref[...], preferred_element_type=f32)
                 + jnp.dot(h2, wr1_ref[...], preferred_element_type=f32))) + br_ref[...]
    lane = lax.broadcasted_iota(i32, (tm, LANES), 1)
    work = logits
    vals, idxs, sels = [], [], []
    for _ in range(TOP_K):
        mx = jnp.max(work, axis=-1, keepdims=True)
        idx = jnp.min(jnp.where(work == mx, lane, LANES), axis=-1, keepdims=True)
        sel = lane == idx
        vals.append(mx)
        idxs.append(idx)
        sels.append(sel)
        work = jnp.where(sel, NEG_INF, work)
    es = [jnp.exp(v - vals[0]) for v in vals]
    tot = es[0] + es[1] + es[2] + es[3]

    onehot = jnp.where(sels[0] | sels[1] | sels[2] | sels[3], 1.0, 0.0)
    tri = (lax.broadcasted_iota(i32, (tm, tm), 0) > lax.broadcasted_iota(i32, (tm, tm), 1)).astype(bf16)
    base = jnp.dot(tri, onehot.astype(bf16), preferred_element_type=f32) + carry_ref[...]
    ids_out = jnp.zeros((tm, LANES), i32)
    pos_out = jnp.zeros((tm, LANES), i32)
    gate_out = jnp.zeros((tm, LANES), f32)
    for k in range(TOP_K):
        pk = jnp.sum(jnp.where(sels[k], base, 0.0), axis=-1, keepdims=True).astype(i32)
        ids_out = jnp.where(lane == k, idxs[k], ids_out)
        pos_out = jnp.where(lane == k, pk, pos_out)
        gate_out = jnp.where(lane == k, es[k] / tot, gate_out)
    ids_ref[...] = ids_out.T[:SUBLANES, :]
    pos_ref[...] = pos_out.T[:SUBLANES, :]
    gate_ref[...] = gate_out[:, :TOP_K]
    carry_ref[...] += jnp.sum(onehot, axis=0, keepdims=True)
    cnt_ref[...] = carry_ref[...]


def _outproj(x2, hfb, o_m, ha, g_m, g_a, w_out, g_ffn, wr1, wr2, b_r, batch, seq_len):
    n = x2.shape[0]
    tm = TM_OUT
    tpb = seq_len // tm
    row = lambda i: (i, 0)
    const = lambda i: (0, 0)
    ha2 = ha.reshape(n, A_WIDTH)
    return pl.pallas_call(
        functools.partial(_outproj_kernel, tm=tm),
        grid=(n // tm,),
        in_specs=[
            pl.BlockSpec((tm, D_MODEL), row),
            pl.BlockSpec((None, None, tm, M_WIDTH), lambda i: (i // tpb, 0, i % tpb, 0)),
            pl.BlockSpec((None, None, tm, M_WIDTH), lambda i: (i // tpb, 1, i % tpb, 0)),
            pl.BlockSpec((tm, M_WIDTH), row),
            pl.BlockSpec((tm, A_WIDTH), row),
            pl.BlockSpec((1, M_WIDTH), const),
            pl.BlockSpec((1, A_WIDTH), const),
            pl.BlockSpec((M_WIDTH + A_WIDTH, D_MODEL), const),
            pl.BlockSpec((1, D_MODEL), const),
            pl.BlockSpec((D_MODEL, LANES), const),
            pl.BlockSpec((D_MODEL, LANES), const),
            pl.BlockSpec((1, LANES), const),
        ],
        out_specs=[
            pl.BlockSpec((tm, D_MODEL), row),
            pl.BlockSpec((tm * TOKEN_TILE_ROWS, LANES), row),
            pl.BlockSpec((SUBLANES, tm), lambda i: (0, i)),
            pl.BlockSpec((SUBLANES, tm), lambda i: (0, i)),
            pl.BlockSpec((tm, TOP_K), row),
            pl.BlockSpec((1, LANES), const),
        ],
        out_shape=(
            jax.ShapeDtypeStruct((n, D_MODEL), f32),
            jax.ShapeDtypeStruct((n * TOKEN_TILE_ROWS, LANES), f32),
            jax.ShapeDtypeStruct((SUBLANES, n), i32),
            jax.ShapeDtypeStruct((SUBLANES, n), i32),
            jax.ShapeDtypeStruct((n, TOP_K), f32),
            jax.ShapeDtypeStruct((1, LANES), f32),
        ),
        scratch_shapes=[pltpu.VMEM((1, LANES), f32)],
        compiler_params=_cparams(("arbitrary",)),
        name="outproj_router",
    )(x2, hfb, hfb, o_m, ha2, g_m, g_a, w_out, g_ffn, wr1, wr2, b_r)


def _dispatch_kernel(pend_ref, padded_ref, slot_ref, hn_ref, *rest, tm, tm_e, n_blocks, with_wprep):
    if with_wprep:
        w_ref, xb_hbm, wout_ref, zero_ref, sem = rest
    else:
        xb_hbm, zero_ref, sem = rest
    i = pl.program_id(0)

    blk = tm_e * TOKEN_TILE_ROWS

    def zero_copy(e):
        start = (pend_ref[e] - tm_e) * TOKEN_TILE_ROWS
        return pltpu.make_async_copy(zero_ref, xb_hbm.at[pl.ds(pl.multiple_of(start, blk), blk)], sem)

    def tail_copy(b):
        return pltpu.make_async_copy(zero_ref, xb_hbm.at[pl.ds(pl.multiple_of(b * blk, blk), blk)], sem)

    @pl.when(i == 0)
    def _():
        zero_ref[...] = jnp.zeros_like(zero_ref)
        for e in range(N_EXPERTS):
            @pl.when(padded_ref[e] > 0)
            def _():
                zero_copy(e).start()
        used = pend_ref[N_EXPERTS - 1] // tm_e

        def tail_start(b, carry):
            tail_copy(b).start()
            return carry

        def tail_wait(b, carry):
            tail_copy(b).wait()
            return carry

        lax.fori_loop(used, n_blocks, tail_start, 0)
        for e in range(N_EXPERTS):
            @pl.when(padded_ref[e] > 0)
            def _():
                zero_copy(e).wait()
        lax.fori_loop(used, n_blocks, tail_wait, 0)

    def row(j, k):
        return pltpu.make_async_copy(_token_tile(hn_ref, j), _token_tile(xb_hbm, slot_ref[0, 0, j * TOP_K + k]), sem)

    def issue(j, carry):
        for k in range(TOP_K):
            row(j, k).start(priority=k % 2)
        return carry

    lax.fori_loop(0, tm, issue, 0, unroll=4)

    if with_wprep:
        _wprep_kernel(w_ref, wout_ref)

    def drain(j, carry):
        for k in range(TOP_K):
            row(0, k).wait()
        return carry

    lax.fori_loop(0, tm, drain, 0, unroll=4)


def _dispatch(hn8, slot_flat, pend, padded, cap, w_gate_up=None):
    n = hn8.shape[0] // TOKEN_TILE_ROWS
    tm = TM_DISPATCH
    steps = n // tm
    slot3 = slot_flat.reshape(steps, 1, tm * TOP_K)
    with_wprep = w_gate_up is not None
    in_specs = [
        pl.BlockSpec((1, 1, tm * TOP_K), lambda i, pe, pa: (i, 0, 0), memory_space=pltpu.SMEM),
        pl.BlockSpec((tm * TOKEN_TILE_ROWS, LANES), lambda i, pe, pa: (i, 0)),
    ]
    out_specs = [pl.BlockSpec(memory_space=pl.ANY)]
    out_shape = [jax.ShapeDtypeStruct((cap * TOKEN_TILE_ROWS, LANES), f32)]
    operands = [pend, padded, slot3, hn8]
    if with_wprep:
        ne = w_gate_up.shape[0]
        halves = steps // ne
        assert halves * ne == steps and D_MODEL % halves == 0, "one weight slab per grid step"
        tr = D_MODEL // halves
        wspec = pl.BlockSpec((None, tr, 2 * D_FF), lambda i, pe, pa: (i // halves, i % halves, 0))
        in_specs.append(wspec)
        out_specs.append(wspec)
        out_shape.append(jax.ShapeDtypeStruct((ne, D_MODEL, 2 * D_FF), bf16))
        operands.append(w_gate_up)
    grid_spec = pltpu.PrefetchScalarGridSpec(
        num_scalar_prefetch=2,
        grid=(steps,),
        in_specs=in_specs,
        out_specs=out_specs,
        scratch_shapes=[pltpu.VMEM((TM_EXP * TOKEN_TILE_ROWS, LANES), f32), pltpu.SemaphoreType.DMA(())],
    )
    outs = pl.pallas_call(
        functools.partial(_dispatch_kernel, tm=tm, tm_e=TM_EXP, n_blocks=cap // TM_EXP, with_wprep=with_wprep),
        grid_spec=grid_spec,
        out_shape=out_shape,
        compiler_params=_cparams(("arbitrary",)),
        name="dispatch_wprep" if with_wprep else "dispatch",
    )(*operands)
    return outs if with_wprep else outs[0]


def _experts_kernel(be_ref, nv_ref, xb_ref, wgu_ref, bgu_ref, wd_ref, bd_ref, yb_ref, *, tm):
    j = pl.program_id(0)

    @pl.when(j < nv_ref[0])
    def _():
        xv = _load_token_tiles(xb_ref, tm).astype(bf16)
        h = jnp.dot(xv, wgu_ref[...], preferred_element_type=f32) + bgu_ref[...]
        gt = jnp.minimum(h[:, :D_FF], SWIGLU_LIMIT)
        up = jnp.clip(h[:, D_FF:], -SWIGLU_LIMIT, SWIGLU_LIMIT)
        act = (up + 1.0) * (gt * jax.nn.sigmoid(SWIGLU_ALPHA * gt))
        _store_token_tiles(yb_ref, jnp.dot(act.astype(bf16), wd_ref[...], preferred_element_type=f32) + bd_ref[...])

    @pl.when(j >= nv_ref[0])
    def _():
        yb_ref[...] = jnp.zeros_like(yb_ref)


def _experts(xb, block_e, nvalid, w_gu, b_gu, w_d, b_d):
    cap = xb.shape[0] // TOKEN_TILE_ROWS
    tm = TM_EXP
    grid_spec = pltpu.PrefetchScalarGridSpec(
        num_scalar_prefetch=2,
        grid=(cap // tm,),
        in_specs=[
            pl.BlockSpec((tm * TOKEN_TILE_ROWS, LANES), lambda j, be, nv: (jnp.minimum(j, nv[0] - 1), 0)),
            pl.BlockSpec((None, D_MODEL, 2 * D_FF), lambda j, be, nv: (be[j], 0, 0)),
            pl.BlockSpec((None, 1, 2 * D_FF), lambda j, be, nv: (be[j], 0, 0)),
            pl.BlockSpec((None, D_FF, D_MODEL), lambda j, be, nv: (be[j], 0, 0)),
            pl.BlockSpec((None, 1, D_MODEL), lambda j, be, nv: (be[j], 0, 0)),
        ],
        out_specs=pl.BlockSpec((tm * TOKEN_TILE_ROWS, LANES), lambda j, be, nv: (j, 0)),
    )
    return pl.pallas_call(
        functools.partial(_experts_kernel, tm=tm),
        grid_spec=grid_spec,
        out_shape=jax.ShapeDtypeStruct((cap * TOKEN_TILE_ROWS, LANES), f32),
        compiler_params=_cparams(("arbitrary",)),
        name="experts",
    )(block_e, nvalid, xb, w_gu, b_gu, w_d, b_d)


def _combine_kernel(slot_ref, slot_next_ref, xmid_ref, gate_ref, p_ref, wproj_ref, gpost_ref, gple_ref, wgate_ref,
                    yb_hbm, out_ref, ybuf_ref, y_ref, sem, *, tm):
    i = pl.program_id(0)
    cur = lax.rem(i, 2)
    nxt = 1 - cur

    def row(slots, buf, j, k):
        return pltpu.make_async_copy(_token_tile(yb_hbm, slots[0, 0, j * TOP_K + k]),
                                     _token_tile(ybuf_ref.at[buf, k], j), sem.at[buf])

    def drain(buf):
        def wait(j, carry):
            for k in range(TOP_K):
                row(slot_ref, buf, 0, k).wait()
            return carry
        lax.fori_loop(0, tm, wait, 0, unroll=4)

    @pl.when(i == 0)
    def _():
        def issue(j, carry):
            for k in range(TOP_K):
                row(slot_ref, 0, j, k).start(priority=k % 2)
            return carry
        lax.fori_loop(0, tm, issue, 0, unroll=4)

    pe = _rms(jnp.dot(p_ref[...].astype(bf16), wproj_ref[...], preferred_element_type=f32), gpost_ref[...])
    drain(cur)

    half_chunk = CMB_CHUNK // 2

    def chunk_body(cb, carry):
        j0 = pl.multiple_of(cb * CMB_CHUNK, CMB_CHUNK)
        for jj in range(half_chunk):
            for k in range(TOP_K):
                row(slot_next_ref, nxt, cb * half_chunk + jj, k).start(priority=k % 2)
        gate = gate_ref[pl.ds(j0, CMB_CHUNK), :]
        acc = gate[:, 0:1] * _load_token_tiles(ybuf_ref.at[cur, 0], CMB_CHUNK, j0)
        for k in range(1, TOP_K):
            acc = acc + gate[:, k:k + 1] * _load_token_tiles(ybuf_ref.at[cur, k], CMB_CHUNK, j0)
        y_ref[pl.ds(j0, CMB_CHUNK), :] = acc
        return carry

    lax.fori_loop(0, tm // CMB_CHUNK, chunk_body, 0)
    for j in range(tm // 2, tm):
        for k in range(TOP_K):
            row(slot_next_ref, nxt, j, k).start(priority=k % 2)

    x2 = xmid_ref[...] + y_ref[...]
    gl = jnp.dot(_rms(x2, gple_ref[...]).astype(bf16), wgate_ref[...], preferred_element_type=f32)
    out_ref[...] = x2 + jax.nn.sigmoid(gl) * pe

    @pl.when(i + 1 == pl.num_programs(0))
    def _():
        drain(nxt)


def _combine(slot_flat, xmid, gate, p2, w_proj, g_post, g_ple, w_gate, yb):
    n = xmid.shape[0]
    tm = TM_CMB
    steps = n // tm
    slot3 = slot_flat.reshape(steps, 1, tm * TOP_K)
    row = lambda i: (i, 0)
    const = lambda i: (0, 0)
    return pl.pallas_call(
        functools.partial(_combine_kernel, tm=tm),
        grid=(steps,),
        in_specs=[
            pl.BlockSpec((1, 1, tm * TOP_K), lambda i: (i, 0, 0), memory_space=pltpu.SMEM),
            pl.BlockSpec((1, 1, tm * TOP_K), lambda i: (jnp.minimum(i + 1, steps - 1), 0, 0),
                         memory_space=pltpu.SMEM),
            pl.BlockSpec((tm, D_MODEL), row),
            pl.BlockSpec((tm, TOP_K), row),
            pl.BlockSpec((tm, PLE_DIM), row),
            pl.BlockSpec((PLE_DIM, D_MODEL), const),
            pl.BlockSpec((1, D_MODEL), const),
            pl.BlockSpec((1, D_MODEL), const),
            pl.BlockSpec((D_MODEL, D_MODEL), const),
            pl.BlockSpec(memory_space=pl.ANY),
        ],
        out_specs=pl.BlockSpec((tm, D_MODEL), row),
        out_shape=jax.ShapeDtypeStruct((n, D_MODEL), f32),
        scratch_shapes=[pltpu.VMEM((2, TOP_K, tm * TOKEN_TILE_ROWS, LANES), f32), pltpu.VMEM((tm, D_MODEL), f32),
                        pltpu.SemaphoreType.DMA((2,))],
        compiler_params=_cparams(("arbitrary",)),
        name="combine_ple",
    )(slot3, slot3, xmid, gate, p2, w_proj, g_post, g_ple, w_gate, yb)


def _wprep_kernel(w_ref, out_ref):
    grp = 2 * LANES
    src = lax.broadcasted_iota(i32, (grp, grp), 0)
    dst = lax.broadcasted_iota(i32, (grp, grp), 1)
    want = jnp.where(dst < LANES, 2 * dst, 2 * (dst - LANES) + 1)
    perm = (src == want).astype(bf16)
    for g in range(2 * D_FF // grp):
        t = jnp.dot(w_ref[:, g * grp:(g + 1) * grp].astype(bf16), perm, preferred_element_type=f32)
        out_ref[:, g * LANES:(g + 1) * LANES] = t[:, :LANES].astype(bf16)
        out_ref[:, D_FF + g * LANES:D_FF + (g + 1) * LANES] = t[:, LANES:].astype(bf16)


def _prep_weights(g_mix, w_in, b_gates, conv_w, g_m_head, g_q, g_k, rel_bias, g_a_out, w_out, g_ffn,
                  w_router, b_router, w_gate_up, b_gate_up, w_down, b_down, g_ple, w_ple_gate, w_ple_proj,
                  g_ple_post):
    g0 = 4 * M_WIDTH
    w = w_in[0]
    pw = {}
    pw["g_mix"] = g_mix[0][None, :]
    pw["w_main"] = jnp.concatenate([w[:, :g0], w[:, g0 + N_GATES:]], axis=1).astype(bf16)
    pw["w_gate"] = jnp.pad(w[:, g0:g0 + N_GATES], ((0, 0), (0, LANES - N_GATES))).astype(bf16)
    pw["b_gate"] = jnp.pad(b_gates[0], (0, LANES - N_GATES))[None, :]
    pw["conv_w"] = conv_w[0]
    pw["g_m"] = g_m_head[0].reshape(1, M_WIDTH)
    pw["g_q2"] = jnp.tile(g_q[0], 2)[None, :]
    pw["g_k2"] = jnp.tile(g_k[0], 2)[None, :]
    pw["bias_tab"] = _natten_bias_table(rel_bias[0])
    pw["g_a"] = g_a_out[0][None, :]
    pw["w_out"] = w_out[0].astype(bf16)
    pw["g_ffn"] = g_ffn[0][None, :]
    wr = jnp.pad(w_router[0], ((0, 0), (0, LANES - N_EXPERTS)))
    wr1 = wr.astype(bf16)
    pw["wr1"] = wr1
    pw["wr2"] = (wr - wr1.astype(f32)).astype(bf16)
    pw["b_r"] = jnp.pad(b_router[0], (0, LANES - N_EXPERTS), constant_values=NEG_INF)[None, :]
    pw["w_gate_up"] = w_gate_up[0]
    bgu = b_gate_up[0]
    pw["b_gu"] = jnp.concatenate([bgu[:, 0::2], bgu[:, 1::2]], axis=-1)[:, None, :]
    pw["w_d"] = w_down[0].astype(bf16)
    pw["b_d"] = b_down[0][:, None, :]
    pw["g_ple"] = g_ple[0][None, :]
    pw["w_ple_gate"] = w_ple_gate[0].astype(bf16)
    pw["w_ple_proj"] = w_ple_proj[0].astype(bf16)
    pw["g_ple_post"] = g_ple_post[0][None, :]
    return pw


def _trunk(x, p, pw, w_gu=None):
    batch, seq_len, _ = x.shape
    n = batch * seq_len
    x2 = x.reshape(n, D_MODEL)
    qk, v_m, o_m, gates, qa, ka, va = _inproj(x2, pw["g_mix"], pw["w_main"], pw["w_gate"], pw["b_gate"],
                                              pw["conv_w"], seq_len)
    hfb = _mlstm(qk, v_m, gates, batch, seq_len)
    ha = _natten(qa, ka, va, pw["bias_tab"], pw["g_q2"], pw["g_k2"], batch, seq_len)
    xmid, hn, ids, pos, gate, cnt = _outproj(x2, hfb, o_m, ha, pw["g_m"], pw["g_a"], pw["w_out"], pw["g_ffn"],
                                             pw["wr1"], pw["wr2"], pw["b_r"], batch, seq_len)

    tm_e = TM_EXP
    counts = cnt[0, :N_EXPERTS].astype(i32)
    padded = (counts + tm_e - 1) // tm_e * tm_e
    pend = jnp.cumsum(padded).astype(i32)
    pstart = pend - padded
    ids_t = ids[:TOP_K, :]
    first = jnp.zeros_like(ids_t)
    for e in range(N_EXPERTS):
        first = jnp.where(ids_t == e, pstart[e], first)
    slot = (first + pos[:TOP_K, :]).T.reshape(-1)
    nk = n * TOP_K
    n_blocks = (nk + N_EXPERTS * (tm_e - 1) + tm_e - 1) // tm_e
    cap = n_blocks * tm_e
    block_start = jnp.arange(n_blocks, dtype=i32) * tm_e
    block_e = jnp.minimum(jnp.sum((pend[None, :] <= block_start[:, None]).astype(i32), axis=1), N_EXPERTS - 1)
    nvalid = (pend[-1:] // tm_e).astype(i32)

    if w_gu is None:
        xb, w_gu = _dispatch(hn, slot, pend, padded, cap, pw["w_gate_up"])
    else:
        xb = _dispatch(hn, slot, pend, padded, cap)
    yb = _experts(xb, block_e, nvalid, w_gu, pw["b_gu"], pw["w_d"], pw["b_d"])
    out = _combine(slot, xmid, gate, p.reshape(n, PLE_DIM), pw["w_ple_proj"], pw["g_ple_post"], pw["g_ple"],
                   pw["w_ple_gate"], yb)
    return out.reshape(batch, seq_len, D_MODEL), w_gu


def kernel(x_prompt, x_sample, p_prompt, p_sample, g_mix, w_in, b_gates, conv_w, g_m_head, g_q, g_k, rel_bias,
           g_a_out, w_out, g_ffn, w_router, b_router, w_gate_up, b_gate_up, w_down, b_down, g_ple, w_ple_gate,
           w_ple_proj, g_ple_post):
    assert w_in.shape[0] == 1, "single-layer trunk"
    pw = _prep_weights(g_mix, w_in, b_gates, conv_w, g_m_head, g_q, g_k, rel_bias, g_a_out, w_out, g_ffn,
                       w_router, b_router, w_gate_up, b_gate_up, w_down, b_down, g_ple, w_ple_gate, w_ple_proj,
                       g_ple_post)
    y_prompt, w_gu = _trunk(x_prompt, p_prompt[0], pw)
    y_sample, _ = _trunk(x_sample, p_sample[0], pw, w_gu)
    return (y_prompt, y_sample)
```

```python
import functools

import jax
import jax.numpy as jnp
from jax import lax
from jax.experimental import pallas as pl
from jax.experimental.pallas import tpu as pltpu

f32 = jnp.float32
bf16 = jnp.bfloat16
i32 = jnp.int32

D_MODEL = 1024
HEAD_DIM = 64
M_HEADS = 8
A_HEADS = 8
M_WIDTH = M_HEADS * HEAD_DIM
A_WIDTH = A_HEADS * HEAD_DIM
N_GATES = 4 * M_HEADS
GRID_W = 64
WIN_H = 8
WIN_W = 16
N_EXPERTS = 32
TOP_K = 4
D_FF = 1024
SWIGLU_LIMIT = 7.0
SWIGLU_ALPHA = 1.702
PLE_DIM = 256
EPS = 1e-6

LANES = 128
SUBLANES = 8
MXU_TILE = 256
VMEM_LIMIT = 56 * 1024 * 1024

TM_IN = 512
MLSTM_CHUNK = 256
NAT_ROWS = 16
TM_OUT = 512
TM_EXP = 512
TM_DISPATCH = 256
TM_CMB = 256
CMB_CHUNK = 32

NEG_INF = float("-inf")


def _cparams(sem):
    return pltpu.CompilerParams(dimension_semantics=sem, vmem_limit_bytes=VMEM_LIMIT)


def _rms(xv, g):
    return xv * lax.rsqrt(jnp.mean(xv * xv, axis=-1, keepdims=True) + EPS) * g


def _split2(a):
    hi = a.astype(bf16)
    lo = (a - hi.astype(f32)).astype(bf16)
    return hi, lo


def _split3(a):
    hi = a.astype(bf16)
    r = a - hi.astype(f32)
    mid = r.astype(bf16)
    lo = (r - mid.astype(f32)).astype(bf16)
    return hi, mid, lo


def _head_mean_sq(xv, width):
    grp = min(width, MXU_TILE)
    a = lax.broadcasted_iota(i32, (grp, grp), 0) // HEAD_DIM
    b = lax.broadcasted_iota(i32, (grp, grp), 1) // HEAD_DIM
    bd = jnp.where(a == b, 1.0 / HEAD_DIM, 0.0).astype(bf16)
    hi, lo = _split2(xv * xv)
    parts = []
    for g in range(width // grp):
        sl = slice(g * grp, (g + 1) * grp)
        parts.append(jnp.dot(hi[:, sl], bd, preferred_element_type=f32)
                     + jnp.dot(lo[:, sl], bd, preferred_element_type=f32))
    return parts[0] if len(parts) == 1 else jnp.concatenate(parts, axis=1)


TOKEN_TILE_ROWS = D_MODEL // LANES


def _store_token_tiles(ref, val):
    n = val.shape[0]
    for c in range(TOKEN_TILE_ROWS):
        ref[pl.ds(c, n, stride=TOKEN_TILE_ROWS), :] = val[:, c * LANES:(c + 1) * LANES]


def _token_tile(ref, t):
    return ref.at[pl.ds(pl.multiple_of(t * TOKEN_TILE_ROWS, TOKEN_TILE_ROWS), TOKEN_TILE_ROWS)]


def _load_token_tiles(ref, n, first=0):
    base = first * TOKEN_TILE_ROWS
    return jnp.concatenate([ref[pl.ds(base + c, n, stride=TOKEN_TILE_ROWS), :] for c in range(TOKEN_TILE_ROWS)],
                           axis=1)


def _log_sigmoid(x):
    return jnp.minimum(x, 0.0) - jnp.log1p(jnp.exp(-jnp.abs(x)))


def _inproj_kernel(x_ref, xp_ref, xn_ref, g_ref, wm_ref, wg_ref, bg_ref, cw_ref,
                   qk_ref, v_ref, o_ref, gates_ref, qa_ref, ka_ref, va_ref, *, tm, seq_len):
    i = pl.program_id(0)
    hf = _rms(jnp.concatenate([xp_ref[...], x_ref[...], xn_ref[...]], axis=0), g_ref[...])
    ext = tm + 2 * SUBLANES
    h = hf[SUBLANES:SUBLANES + tm, :].astype(bf16)
    zq = jnp.dot(hf.astype(bf16), wm_ref[:, :2 * M_WIDTH], preferred_element_type=f32)
    z = jnp.dot(h, wm_ref[:, 2 * M_WIDTH:], preferred_element_type=f32)
    zg = jnp.dot(h, wg_ref[...], preferred_element_type=f32) + bg_ref[...]
    gates_ref[...] = zg.T[:N_GATES, :]
    start = lax.rem(i * tm, seq_len)
    rid = lax.broadcasted_iota(i32, (tm, 1), 0)
    u = zq[SUBLANES:SUBLANES + tm, :]
    u_prev = pltpu.roll(zq, 1, 0)[SUBLANES:SUBLANES + tm, :]
    u_next = pltpu.roll(zq, ext - 1, 0)[SUBLANES:SUBLANES + tm, :]
    u_prev = jnp.where(jnp.logical_and(rid == 0, start == 0), 0.0, u_prev)
    u_next = jnp.where(jnp.logical_and(rid == tm - 1, start + tm == seq_len), 0.0, u_next)
    cw = cw_ref[...]
    c = u_prev * cw[0:1, :] + u * cw[1:2, :] + u_next * cw[2:3, :]
    qk_ref[...] = (c * jax.nn.sigmoid(c)).astype(bf16)
    v_ref[...] = z[:, 0:M_WIDTH].T.astype(bf16)
    o_ref[...] = z[:, M_WIDTH:2 * M_WIDTH]
    base = 2 * M_WIDTH
    qa_ref[...] = z[:, base:base + A_WIDTH]
    ka_ref[...] = z[:, base + A_WIDTH:base + 2 * A_WIDTH]
    va_ref[...] = z[:, base + 2 * A_WIDTH:base + 3 * A_WIDTH].astype(bf16)


def _inproj(x2, g_mix, w_main, w_gate, b_gate, conv_w, seq_len):
    n = x2.shape[0]
    tm = TM_IN
    nb8 = n // SUBLANES
    r8 = tm // SUBLANES
    wcols = w_main.shape[1]
    row = lambda i: (i, 0)
    const = lambda i: (0, 0)
    out_shapes = (
        jax.ShapeDtypeStruct((n, 2 * M_WIDTH), bf16),
        jax.ShapeDtypeStruct((M_WIDTH, n), bf16),
        jax.ShapeDtypeStruct((n, M_WIDTH), f32),
        jax.ShapeDtypeStruct((N_GATES, n), f32),
        jax.ShapeDtypeStruct((n, A_WIDTH), f32),
        jax.ShapeDtypeStruct((n, A_WIDTH), f32),
        jax.ShapeDtypeStruct((n, A_WIDTH), bf16),
    )
    return pl.pallas_call(
        functools.partial(_inproj_kernel, tm=tm, seq_len=seq_len),
        grid=(n // tm,),
        in_specs=[
            pl.BlockSpec((tm, D_MODEL), row),
            pl.BlockSpec((SUBLANES, D_MODEL), lambda i: (jnp.maximum(i * r8 - 1, 0), 0)),
            pl.BlockSpec((SUBLANES, D_MODEL), lambda i: (jnp.minimum((i + 1) * r8, nb8 - 1), 0)),
            pl.BlockSpec((1, D_MODEL), const),
            pl.BlockSpec((D_MODEL, wcols), const),
            pl.BlockSpec((D_MODEL, LANES), const),
            pl.BlockSpec((1, LANES), const),
            pl.BlockSpec((3, 2 * M_WIDTH), const),
        ],
        out_specs=[
            pl.BlockSpec((tm, 2 * M_WIDTH), row),
            pl.BlockSpec((M_WIDTH, tm), lambda i: (0, i)),
            pl.BlockSpec((tm, M_WIDTH), row),
            pl.BlockSpec((N_GATES, tm), lambda i: (0, i)),
            pl.BlockSpec((tm, A_WIDTH), row),
            pl.BlockSpec((tm, A_WIDTH), row),
            pl.BlockSpec((tm, A_WIDTH), row),
        ],
        out_shape=out_shapes,
        compiler_params=_cparams(("parallel",)),
        name="inproj",
    )(x2, x2, x2, g_mix, w_main, w_gate, b_gate, conv_w)


def _mlstm_kernel(q_ref, k_ref, vt_ref, g_ref, out_ref, c_ref, m_ref, *, L):
    d = pl.program_id(1)
    c = pl.program_id(2)

    @pl.when(c == 0)
    def _():
        c_ref[...] = jnp.zeros_like(c_ref)
        m_ref[...] = jnp.zeros_like(m_ref)

    fwd = d == 0
    sgn = jnp.where(fwd, 1, -1)
    row = lax.broadcasted_iota(i32, (L, L), 0)
    col = lax.broadcasted_iota(i32, (L, L), 1)
    vis = (col - row) * sgn >= 0
    t_row = vis.astype(bf16)

    g = g_ref[...]
    h8 = M_HEADS
    li = jnp.where(fwd, g[0:h8, :], g[2 * h8:3 * h8, :])
    lf = _log_sigmoid(jnp.where(fwd, g[h8:2 * h8, :], g[3 * h8:4 * h8, :]))
    b = sum(jnp.dot(part, t_row, preferred_element_type=f32) for part in _split3(lf))
    bl = jnp.sum(lf, axis=1, keepdims=True)
    r_row = li - b
    pos = lax.broadcasted_iota(i32, (1, L), 1)
    cm = r_row
    sh = 1
    while sh < L:
        from_before = jnp.where(pos >= sh, pltpu.roll(cm, sh, 1), NEG_INF)
        from_after = jnp.where(pos < L - sh, pltpu.roll(cm, L - sh, 1), NEG_INF)
        cm = jnp.maximum(cm, jnp.where(fwd, from_before, from_after))
        sh *= 2
    m_prev = m_ref[:, 0:1]
    mx_r = jnp.maximum(m_prev, cm)
    m_new = bl + jnp.maximum(m_prev, jnp.max(r_row, axis=1, keepdims=True))
    decay = jnp.exp(bl + m_prev - m_new)
    m_ref[...] = jnp.broadcast_to(m_new, (h8, LANES))
    w_int = jnp.exp(m_prev - mx_r)
    floor = jnp.exp(-(b + mx_r))
    wk = jnp.exp(bl + r_row - m_new)
    r_col = jnp.concatenate([r_row, jnp.zeros((LANES - h8, L), f32)], axis=0).T

    lane = lax.broadcasted_iota(i32, (1, LANES), 1)
    sub = lax.broadcasted_iota(i32, (LANES, 1), 0)
    lane_half = (lane < HEAD_DIM, lane >= HEAD_DIM)
    sub_half = (sub < HEAD_DIM, sub >= HEAD_DIM)
    nt = (((1,), (1,)), ((), ()))

    heads = range(M_HEADS)
    pair = lambda h: slice((h // 2) * LANES, (h // 2 + 1) * LANES)
    qz = [jnp.where(lane_half[h % 2], q_ref[:, pair(h)] * (HEAD_DIM ** -0.5), 0).astype(bf16) for h in heads]
    kz = [jnp.where(lane_half[h % 2], k_ref[:, pair(h)], 0).astype(bf16) for h in heads]
    va_t = [jnp.where(sub_half[h % 2], vt_ref[pair(h), :], 1).astype(bf16) for h in heads]
    kq = [lax.dot_general(kz[h], qz[h], nt, preferred_element_type=f32) for h in heads]
    s_t = [jnp.where(vis, kq[h] * jnp.exp(r_col[:, h:h + 1] - mx_r[h:h + 1, :]), 0.0).astype(bf16) for h in heads]
    cst = [c_ref[h] for h in heads]
    nums = [w_int[h:h + 1, :] * lax.dot_general(cst[h].astype(bf16), qz[h], nt, preferred_element_type=f32)
            + jnp.dot(va_t[h], s_t[h], preferred_element_type=f32) for h in heads]
    for h in heads:
        vw = (va_t[h].astype(f32) * wk[h:h + 1, :]).astype(bf16)
        c_ref[h] = decay[h:h + 1, :] * cst[h] + jnp.dot(vw, kz[h], preferred_element_type=f32)
    for p in range(M_HEADS // 2):
        ev, od = 2 * p, 2 * p + 1
        numer = jnp.where(sub_half[0], nums[ev], nums[od])
        den = jnp.where(sub_half[0], nums[ev][HEAD_DIM:HEAD_DIM + 1, :], nums[od][0:1, :])
        lim = jnp.where(sub_half[0], floor[ev:ev + 1, :], floor[od:od + 1, :])
        out_ref[:, pair(ev)] = (numer / jnp.maximum(jnp.abs(den), lim)).T


def _mlstm(qk, v_t, gates_t, batch, seq_len):
    L = MLSTM_CHUNK
    nc = seq_len // L
    qk3 = qk.reshape(batch, seq_len, 2 * M_WIDTH)
    cidx = lambda d, c: jnp.where(d == 0, c, nc - 1 - c)
    return pl.pallas_call(
        functools.partial(_mlstm_kernel, L=L),
        grid=(batch, 2, nc),
        in_specs=[
            pl.BlockSpec((None, L, M_WIDTH), lambda b, d, c: (b, cidx(d, c), 0)),
            pl.BlockSpec((None, L, M_WIDTH), lambda b, d, c: (b, cidx(d, c), 1)),
            pl.BlockSpec((M_WIDTH, L), lambda b, d, c: (0, b * nc + cidx(d, c))),
            pl.BlockSpec((N_GATES, L), lambda b, d, c: (0, b * nc + cidx(d, c))),
        ],
        out_specs=pl.BlockSpec((None, None, L, M_WIDTH), lambda b, d, c: (b, d, cidx(d, c), 0)),
        out_shape=jax.ShapeDtypeStruct((batch, 2, seq_len, M_WIDTH), f32),
        scratch_shapes=[pltpu.VMEM((M_HEADS, LANES, LANES), f32), pltpu.VMEM((M_HEADS, LANES), f32)],
        compiler_params=_cparams(("parallel", "parallel", "arbitrary")),
        name="mlstm",
    )(qk3, qk3, v_t, gates_t)


def _natten_kernel(q_ref, k_ref, v_ref, bias_ref, gq_ref, gk_ref, out_ref, kn_ref, *, seq_len, rb_rows):
    rb = pl.program_id(2)
    rows = seq_len // GRID_W
    norm_rows = 512

    @pl.when(rb == 0)
    def _():
        def body(i, carry):
            sl = pl.ds(pl.multiple_of(i * norm_rows, norm_rows), norm_rows)
            kv = k_ref[sl, :]
            kn_ref[sl, :] = (kv * lax.rsqrt(_head_mean_sq(kv, LANES) + EPS) * gk_ref[...]).astype(bf16)
            return carry
        lax.fori_loop(0, seq_len // norm_rows, body, 0)

    qv = q_ref[...]
    qn = (qv * lax.rsqrt(_head_mean_sq(qv, LANES) + EPS) * gq_ref[...] * (HEAD_DIM ** -0.5)).astype(bf16)
    lane = lax.broadcasted_iota(i32, (1, LANES), 1)
    lo = lane < HEAD_DIM
    nt = (((1,), (1,)), ((), ()))
    pair = 2 * GRID_W

    def window(j):
        r = rb * rb_rows + j
        r0 = jnp.clip(r - WIN_H // 2, 0, rows - WIN_H)
        return r - r0, pl.ds(pl.multiple_of(r0 * GRID_W, GRID_W), WIN_H * GRID_W)

    tiles = []
    for j in range(rb_rows):
        delta, ks = window(j)
        qj = qn[j * GRID_W:(j + 1) * GRID_W, :]
        q2 = jnp.concatenate([jnp.where(lo, qj, jnp.zeros_like(qj)), jnp.where(lo, jnp.zeros_like(qj), qj)], axis=0)
        tiles.append(lax.dot_general(q2, kn_ref[ks, :], nt, preferred_element_type=f32) + bias_ref[delta])
    s = jnp.concatenate(tiles, axis=0)
    p = jnp.exp(s - jnp.max(s, axis=-1, keepdims=True))
    inv = 1.0 / jnp.sum(p, axis=-1, keepdims=True)
    pb = p.astype(bf16)
    for j in range(rb_rows):
        _, ks = window(j)
        o = jnp.dot(pb[j * pair:(j + 1) * pair, :], v_ref[ks, :], preferred_element_type=f32)
        o = o * inv[j * pair:(j + 1) * pair, :]
        out_ref[j * GRID_W:(j + 1) * GRID_W, :] = jnp.where(lo, o[:GRID_W, :], o[GRID_W:, :])


def _natten_bias_table(rel_bias):
    cq = jnp.arange(GRID_W)[:, None]
    ck = jnp.arange(GRID_W)[None, :]
    c0 = jnp.clip(cq - WIN_W // 2, 0, GRID_W - WIN_W)
    col_in = (ck >= c0) & (ck < c0 + WIN_W)
    idx_c = jnp.clip(ck - cq, -(WIN_W - 1), WIN_W - 1) + (WIN_W - 1)
    pick = idx_c[:, :, None] == jnp.arange(2 * WIN_W - 1)
    tz = jnp.sum(jnp.where(pick[None, None], rel_bias.astype(f32)[:, :, None, None, :], 0.0), axis=-1)
    tz = jnp.where(col_in[None, None], tz, NEG_INF)
    tab = jnp.stack([tz[:, WIN_H - 1 - dl:2 * WIN_H - 1 - dl] for dl in range(WIN_H)], axis=1)
    tab = tab.transpose(0, 1, 3, 2, 4).reshape(A_HEADS // 2, 2, WIN_H, GRID_W, WIN_H * GRID_W)
    return tab.transpose(0, 2, 1, 3, 4).reshape(A_HEADS // 2, WIN_H, 2 * GRID_W, WIN_H * GRID_W)


def _natten(qa, ka, va, bias_tab, g_q2, g_k2, batch, seq_len):
    rows = seq_len // GRID_W
    rbr = NAT_ROWS
    tq = rbr * GRID_W
    q3 = qa.reshape(batch, seq_len, A_WIDTH)
    k3 = ka.reshape(batch, seq_len, A_WIDTH)
    v3 = va.reshape(batch, seq_len, A_WIDTH)
    return pl.pallas_call(
        functools.partial(_natten_kernel, seq_len=seq_len, rb_rows=rbr),
        grid=(batch, A_HEADS // 2, rows // rbr),
        in_specs=[
            pl.BlockSpec((None, tq, LANES), lambda b, hp, rb: (b, rb, hp)),
            pl.BlockSpec((None, seq_len, LANES), lambda b, hp, rb: (b, 0, hp)),
            pl.BlockSpec((None, seq_len, LANES), lambda b, hp, rb: (b, 0, hp)),
            pl.BlockSpec((None, WIN_H, 2 * GRID_W, WIN_H * GRID_W), lambda b, hp, rb: (hp, 0, 0, 0)),
            pl.BlockSpec((1, LANES), lambda b, hp, rb: (0, 0)),
            pl.BlockSpec((1, LANES), lambda b, hp, rb: (0, 0)),
        ],
        out_specs=pl.BlockSpec((None, tq, LANES), lambda b, hp, rb: (b, rb, hp)),
        out_shape=jax.ShapeDtypeStruct((batch, seq_len, A_WIDTH), f32),
        scratch_shapes=[pltpu.VMEM((seq_len, LANES), bf16)],
        compiler_params=_cparams(("parallel", "parallel", "arbitrary")),
        name="natten",
    )(q3, k3, v3, bias_tab, g_q2, g_k2)


def _outproj_kernel(x_ref, hf_ref, hb_ref, o_ref, ha_ref, gm_ref, ga_ref, wo_ref, gf_ref,
                    wr1_ref, wr2_ref, br_ref,
                    xmid_ref, hn_ref, ids_ref, pos_ref, gate_ref, cnt_ref, carry_ref, *, tm):
    i = pl.program_id(0)

    @pl.when(i == 0)
    def _():
        carry_ref[...] = jnp.zeros_like(carry_ref)

    hm = hf_ref[...] + hb_ref[...]
    hm = hm * lax.rsqrt(_head_mean_sq(hm, M_WIDTH) + EPS) * gm_ref[...] * jax.nn.sigmoid(o_ref[...])
    ha = _rms(ha_ref[...], ga_ref[...])
    mix = (jnp.dot(hm.astype(bf16), wo_ref[0:M_WIDTH, :], preferred_element_type=f32)
           + jnp.dot(ha.astype(bf16), wo_ref[M_WIDTH:M_WIDTH + A_WIDTH, :], preferred_element_type=f32))
    xm = x_ref[...] + mix
    xmid_ref[...] = xm
    hn = _rms(xm, gf_ref[...])
    _store_token_tiles(hn_ref, hn)

    h1, h2 = _split2(hn)
    logits = (jnp.dot(h1, wr1_ref[...], preferred_element_type=f32)
              + (jnp.dot(h1, wr2_ref[...], preferred_element_type=f32)
                 + jnp.dot(h2, wr1_ref[...], preferred_element_type=f32))) + br_ref[...]
    lane = lax.broadcasted_iota(i32, (tm, LANES), 1)
    work = logits
    vals, idxs, sels = [], [], []
    for _ in range(TOP_K):
        mx = jnp.max(work, axis=-1, keepdims=True)
        idx = jnp.min(jnp.where(work == mx, lane, LANES), axis=-1, keepdims=True)
        sel = lane == idx
        vals.append(mx)
        idxs.append(idx)
        sels.append(sel)
        work = jnp.where(sel, NEG_INF, work)
    es = [jnp.exp(v - vals[0]) for v in vals]
    tot = es[0] + es[1] + es[2] + es[3]

    onehot = jnp.where(sels[0] | sels[1] | sels[2] | sels[3], 1.0, 0.0)
    tri = (lax.broadcasted_iota(i32, (tm, tm), 0) > lax.broadcasted_iota(i32, (tm, tm), 1)).astype(bf16)
    base = jnp.dot(tri, onehot.astype(bf16), preferred_element_type=f32) + carry_ref[...]
    ids_out = jnp.zeros((tm, LANES), i32)
    pos_out = jnp.zeros((tm, LANES), i32)
    gate_out = jnp.zeros((tm, LANES), f32)
    for k in range(TOP_K):
        pk = jnp.sum(jnp.where(sels[k], base, 0.0), axis=-1, keepdims=True).astype(i32)
        ids_out = jnp.where(lane == k, idxs[k], ids_out)
        pos_out = jnp.where(lane == k, pk, pos_out)
        gate_out = jnp.where(lane == k, es[k] / tot, gate_out)
    ids_ref[...] = ids_out.T[:SUBLANES, :]
    pos_ref[...] = pos_out.T[:SUBLANES, :]
    gate_ref[...] = gate_out[:, :TOP_K]
    carry_ref[...] += jnp.sum(onehot, axis=0, keepdims=True)
    cnt_ref[...] = carry_ref[...]


def _outproj(x2, hfb, o_m, ha, g_m, g_a, w_out, g_ffn, wr1, wr2, b_r, batch, seq_len):
    n = x2.shape[0]
    tm = TM_OUT
    tpb = seq_len // tm
    row = lambda i: (i, 0)
    const = lambda i: (0, 0)
    ha2 = ha.reshape(n, A_WIDTH)
    return pl.pallas_call(
        functools.partial(_outproj_kernel, tm=tm),
        grid=(n // tm,),
        in_specs=[
            pl.BlockSpec((tm, D_MODEL), row),
            pl.BlockSpec((None, None, tm, M_WIDTH), lambda i: (i // tpb, 0, i % tpb, 0)),
            pl.BlockSpec((None, None, tm, M_WIDTH), lambda i: (i // tpb, 1, i % tpb, 0)),
            pl.BlockSpec((tm, M_WIDTH), row),
            pl.BlockSpec((tm, A_WIDTH), row),
            pl.BlockSpec((1, M_WIDTH), const),
            pl.BlockSpec((1, A_WIDTH), const),
            pl.BlockSpec((M_WIDTH + A_WIDTH, D_MODEL), const),
            pl.BlockSpec((1, D_MODEL), const),
            pl.BlockSpec((D_MODEL, LANES), const),
            pl.BlockSpec((D_MODEL, LANES), const),
            pl.BlockSpec((1, LANES), const),
        ],
        out_specs=[
            pl.BlockSpec((tm, D_MODEL), row),
            pl.BlockSpec((tm * TOKEN_TILE_ROWS, LANES), row),
            pl.BlockSpec((SUBLANES, tm), lambda i: (0, i)),
            pl.BlockSpec((SUBLANES, tm), lambda i: (0, i)),
            pl.BlockSpec((tm, TOP_K), row),
            pl.BlockSpec((1, LANES), const),
        ],
        out_shape=(
            jax.ShapeDtypeStruct((n, D_MODEL), f32),
            jax.ShapeDtypeStruct((n * TOKEN_TILE_ROWS, LANES), f32),
            jax.ShapeDtypeStruct((SUBLANES, n), i32),
            jax.ShapeDtypeStruct((SUBLANES, n), i32),
            jax.ShapeDtypeStruct((n, TOP_K), f32),
            jax.ShapeDtypeStruct((1, LANES), f32),
        ),
        scratch_shapes=[pltpu.VMEM((1, LANES), f32)],
        compiler_params=_cparams(("arbitrary",)),
        name="outproj_router",
    )(x2, hfb, hfb, o_m, ha2, g_m, g_a, w_out, g_ffn, wr1, wr2, b_r)


def _scatter_clear(pend_ref, padded_ref, xb_hbm, zero_ref, sem, tm_e, n_blocks):
    blk = tm_e * TOKEN_TILE_ROWS

    def zero_copy(e):
        start = (pend_ref[e] - tm_e) * TOKEN_TILE_ROWS
        return pltpu.make_async_copy(zero_ref, xb_hbm.at[pl.ds(pl.multiple_of(start, blk), blk)], sem)

    def tail_copy(b):
        return pltpu.make_async_copy(zero_ref, xb_hbm.at[pl.ds(pl.multiple_of(b * blk, blk), blk)], sem)

    zero_ref[...] = jnp.zeros_like(zero_ref)
    for e in range(N_EXPERTS):
        @pl.when(padded_ref[e] > 0)
        def _():
            zero_copy(e).start()
    used = pend_ref[N_EXPERTS - 1] // tm_e

    def tail_start(b, carry):
        tail_copy(b).start()
        return carry

    def tail_wait(b, carry):
        tail_copy(b).wait()
        return carry

    lax.fori_loop(used, n_blocks, tail_start, 0)
    for e in range(N_EXPERTS):
        @pl.when(padded_ref[e] > 0)
        def _():
            zero_copy(e).wait()
    lax.fori_loop(used, n_blocks, tail_wait, 0)


def _scatter_rows(slot_ref, hn_ref, xb_hbm, sem, tm):
    def row(j, k):
        return pltpu.make_async_copy(_token_tile(hn_ref, j), _token_tile(xb_hbm, slot_ref[0, 0, j * TOP_K + k]), sem)

    def issue(j, carry):
        for k in range(TOP_K):
            row(j, k).start(priority=k % 2)
        return carry

    def drain(j, carry):
        for k in range(TOP_K):
            row(0, k).wait()
        return carry

    start = lambda: lax.fori_loop(0, tm, issue, 0, unroll=4)
    wait = lambda: lax.fori_loop(0, tm, drain, 0, unroll=4)
    return start, wait


def _dispatch_kernel(pend_ref, padded_ref, slot_ref, hn_ref, w_ref, xb_hbm, wout_ref, zero_ref, sem, *,
                     tm, tm_e, n_blocks):
    @pl.when(pl.program_id(0) == 0)
    def _():
        _scatter_clear(pend_ref, padded_ref, xb_hbm, zero_ref, sem, tm_e, n_blocks)

    start, wait = _scatter_rows(slot_ref, hn_ref, xb_hbm, sem, tm)
    start()
    _wprep_kernel(w_ref, wout_ref)
    wait()


def _dispatch(hn8, slot_flat, pend, padded, cap, w_gate_up):
    n = hn8.shape[0] // TOKEN_TILE_ROWS
    tm = TM_DISPATCH
    steps = n // tm
    ne = w_gate_up.shape[0]
    halves = steps // ne
    assert halves * ne == steps and D_MODEL % halves == 0, "one weight slab per grid step"
    wspec = pl.BlockSpec((None, D_MODEL // halves, 2 * D_FF), lambda i, pe, pa: (i // halves, i % halves, 0))
    grid_spec = pltpu.PrefetchScalarGridSpec(
        num_scalar_prefetch=2,
        grid=(steps,),
        in_specs=[
            pl.BlockSpec((1, 1, tm * TOP_K), lambda i, pe, pa: (i, 0, 0), memory_space=pltpu.SMEM),
            pl.BlockSpec((tm * TOKEN_TILE_ROWS, LANES), lambda i, pe, pa: (i, 0)),
            wspec,
        ],
        out_specs=[pl.BlockSpec(memory_space=pl.ANY), wspec],
        scratch_shapes=[pltpu.VMEM((TM_EXP * TOKEN_TILE_ROWS, LANES), f32), pltpu.SemaphoreType.DMA(())],
    )
    return pl.pallas_call(
        functools.partial(_dispatch_kernel, tm=tm, tm_e=TM_EXP, n_blocks=cap // TM_EXP),
        grid_spec=grid_spec,
        out_shape=[jax.ShapeDtypeStruct((cap * TOKEN_TILE_ROWS, LANES), f32),
                   jax.ShapeDtypeStruct((ne, D_MODEL, 2 * D_FF), bf16)],
        compiler_params=_cparams(("arbitrary",)),
        name="dispatch_wprep",
    )(pend, padded, slot_flat.reshape(steps, 1, tm * TOP_K), hn8, w_gate_up)


def _experts_kernel(be_ref, nv_ref, *rest, tm, n_blocks, scatter_steps):
    if scatter_steps:
        (pend2_ref, padded2_ref, xb_ref, wgu_ref, bgu_ref, wd_ref, bd_ref, slot2_ref, hn2_ref,
         yb_ref, xb2_hbm, zero_ref, sem) = rest
        start, wait = _scatter_rows(slot2_ref, hn2_ref, xb2_hbm, sem, tm // TOP_K)
    else:
        xb_ref, wgu_ref, bgu_ref, wd_ref, bd_ref, yb_ref = rest
    j = pl.program_id(0)

    if scatter_steps:
        @pl.when(j == 0)
        def _():
            _scatter_clear(pend2_ref, padded2_ref, xb2_hbm, zero_ref, sem, tm, n_blocks)

        @pl.when(j < scatter_steps)
        def _():
            start()

    @pl.when(j < nv_ref[0])
    def _():
        xv = _load_token_tiles(xb_ref, tm).astype(bf16)
        h = jnp.dot(xv, wgu_ref[...], preferred_element_type=f32) + bgu_ref[...]
        gt = jnp.minimum(h[:, :D_FF], SWIGLU_LIMIT)
        up = jnp.clip(h[:, D_FF:], -SWIGLU_LIMIT, SWIGLU_LIMIT)
        act = (up + 1.0) * (gt * jax.nn.sigmoid(SWIGLU_ALPHA * gt))
        _store_token_tiles(yb_ref, jnp.dot(act.astype(bf16), wd_ref[...], preferred_element_type=f32) + bd_ref[...])

    @pl.when(j >= nv_ref[0])
    def _():
        yb_ref[...] = jnp.zeros_like(yb_ref)

    if scatter_steps:
        @pl.when(j < scatter_steps)
        def _():
            wait()


def _experts(xb, block_e, nvalid, w_gu, b_gu, w_d, b_d, scatter=None):
    cap = xb.shape[0] // TOKEN_TILE_ROWS
    tm = TM_EXP
    n_blocks = cap // tm
    nsp = 4 if scatter is not None else 2
    blk = lambda f: (lambda j, *pref: f(j, *pref[:2]))
    in_specs = [
        pl.BlockSpec((tm * TOKEN_TILE_ROWS, LANES), blk(lambda j, be, nv: (jnp.minimum(j, nv[0] - 1), 0))),
        pl.BlockSpec((None, D_MODEL, 2 * D_FF), blk(lambda j, be, nv: (be[j], 0, 0))),
        pl.BlockSpec((None, 1, 2 * D_FF), blk(lambda j, be, nv: (be[j], 0, 0))),
        pl.BlockSpec((None, D_FF, D_MODEL), blk(lambda j, be, nv: (be[j], 0, 0))),
        pl.BlockSpec((None, 1, D_MODEL), blk(lambda j, be, nv: (be[j], 0, 0))),
    ]
    out_specs = [pl.BlockSpec((tm * TOKEN_TILE_ROWS, LANES), blk(lambda j, be, nv: (j, 0)))]
    out_shape = [jax.ShapeDtypeStruct((cap * TOKEN_TILE_ROWS, LANES), f32)]
    prefetch = [block_e, nvalid]
    operands = [xb, w_gu, b_gu, w_d, b_d]
    scratch = []
    steps = 0
    if scatter is not None:
        hn8, slot_flat, pend2, padded2 = scatter
        tms = tm // TOP_K
        steps = hn8.shape[0] // TOKEN_TILE_ROWS // tms
        assert steps * tms * TOKEN_TILE_ROWS == hn8.shape[0] and steps <= n_blocks
        last = steps - 1
        prefetch += [pend2, padded2]
        in_specs += [
            pl.BlockSpec((1, 1, tm), blk(lambda j, be, nv: (jnp.minimum(j, last), 0, 0)), memory_space=pltpu.SMEM),
            pl.BlockSpec((tms * TOKEN_TILE_ROWS, LANES), blk(lambda j, be, nv: (jnp.minimum(j, last), 0))),
        ]
        operands += [slot_flat.reshape(steps, 1, tm), hn8]
        out_specs.append(pl.BlockSpec(memory_space=pl.ANY))
        out_shape.append(jax.ShapeDtypeStruct((cap * TOKEN_TILE_ROWS, LANES), f32))
        scratch = [pltpu.VMEM((tm * TOKEN_TILE_ROWS, LANES), f32), pltpu.SemaphoreType.DMA(())]
    grid_spec = pltpu.PrefetchScalarGridSpec(
        num_scalar_prefetch=nsp, grid=(n_blocks,), in_specs=in_specs, out_specs=out_specs, scratch_shapes=scratch)
    outs = pl.pallas_call(
        functools.partial(_experts_kernel, tm=tm, n_blocks=n_blocks, scatter_steps=steps),
        grid_spec=grid_spec,
        out_shape=out_shape,
        compiler_params=_cparams(("arbitrary",)),
        name="experts_scatter" if scatter is not None else "experts",
    )(*prefetch, *operands)
    return outs if scatter is not None else outs[0]


def _combine_kernel(slot_ref, slot_next_ref, xmid_ref, gate_ref, p_ref, wproj_ref, gpost_ref, gple_ref, wgate_ref,
                    yb_hbm, out_ref, ybuf_ref, y_ref, sem, *, tm):
    i = pl.program_id(0)
    cur = lax.rem(i, 2)
    nxt = 1 - cur

    def row(slots, buf, j, k):
        return pltpu.make_async_copy(_token_tile(yb_hbm, slots[0, 0, j * TOP_K + k]),
                                     _token_tile(ybuf_ref.at[buf, k], j), sem.at[buf])

    def drain(buf):
        def wait(j, carry):
            for k in range(TOP_K):
                row(slot_ref, buf, 0, k).wait()
            return carry
        lax.fori_loop(0, tm, wait, 0, unroll=4)

    @pl.when(i == 0)
    def _():
        def issue(j, carry):
            for k in range(TOP_K):
                row(slot_ref, 0, j, k).start(priority=k % 2)
            return carry
        lax.fori_loop(0, tm, issue, 0, unroll=4)

    pe = _rms(jnp.dot(p_ref[...].astype(bf16), wproj_ref[...], preferred_element_type=f32), gpost_ref[...])
    drain(cur)

    half_chunk = CMB_CHUNK // 2

    def chunk_body(cb, carry):
        j0 = pl.multiple_of(cb * CMB_CHUNK, CMB_CHUNK)
        for jj in range(half_chunk):
            for k in range(TOP_K):
                row(slot_next_ref, nxt, cb * half_chunk + jj, k).start(priority=k % 2)
        gate = gate_ref[pl.ds(j0, CMB_CHUNK), :]
        acc = gate[:, 0:1] * _load_token_tiles(ybuf_ref.at[cur, 0], CMB_CHUNK, j0)
        for k in range(1, TOP_K):
            acc = acc + gate[:, k:k + 1] * _load_token_tiles(ybuf_ref.at[cur, k], CMB_CHUNK, j0)
        y_ref[pl.ds(j0, CMB_CHUNK), :] = acc
        return carry

    lax.fori_loop(0, tm // CMB_CHUNK, chunk_body, 0)
    for j in range(tm // 2, tm):
        for k in range(TOP_K):
            row(slot_next_ref, nxt, j, k).start(priority=k % 2)

    x2 = xmid_ref[...] + y_ref[...]
    gl = jnp.dot(_rms(x2, gple_ref[...]).astype(bf16), wgate_ref[...], preferred_element_type=f32)
    out_ref[...] = x2 + jax.nn.sigmoid(gl) * pe

    @pl.when(i + 1 == pl.num_programs(0))
    def _():
        drain(nxt)


def _combine(slot_flat, xmid, gate, p2, w_proj, g_post, g_ple, w_gate, yb):
    n = xmid.shape[0]
    tm = TM_CMB
    steps = n // tm
    slot3 = slot_flat.reshape(steps, 1, tm * TOP_K)
    row = lambda i: (i, 0)
    const = lambda i: (0, 0)
    return pl.pallas_call(
        functools.partial(_combine_kernel, tm=tm),
        grid=(steps,),
        in_specs=[
            pl.BlockSpec((1, 1, tm * TOP_K), lambda i: (i, 0, 0), memory_space=pltpu.SMEM),
            pl.BlockSpec((1, 1, tm * TOP_K), lambda i: (jnp.minimum(i + 1, steps - 1), 0, 0),
                         memory_space=pltpu.SMEM),
            pl.BlockSpec((tm, D_MODEL), row),
            pl.BlockSpec((tm, TOP_K), row),
            pl.BlockSpec((tm, PLE_DIM), row),
            pl.BlockSpec((PLE_DIM, D_MODEL), const),
            pl.BlockSpec((1, D_MODEL), const),
            pl.BlockSpec((1, D_MODEL), const),
            pl.BlockSpec((D_MODEL, D_MODEL), const),
            pl.BlockSpec(memory_space=pl.ANY),
        ],
        out_specs=pl.BlockSpec((tm, D_MODEL), row),
        out_shape=jax.ShapeDtypeStruct((n, D_MODEL), f32),
        scratch_shapes=[pltpu.VMEM((2, TOP_K, tm * TOKEN_TILE_ROWS, LANES), f32), pltpu.VMEM((tm, D_MODEL), f32),
                        pltpu.SemaphoreType.DMA((2,))],
        compiler_params=_cparams(("arbitrary",)),
        name="combine_ple",
    )(slot3, slot3, xmid, gate, p2, w_proj, g_post, g_ple, w_gate, yb)


def _wprep_kernel(w_ref, out_ref):
    grp = 2 * LANES
    src = lax.broadcasted_iota(i32, (grp, grp), 0)
    dst = lax.broadcasted_iota(i32, (grp, grp), 1)
    want = jnp.where(dst < LANES, 2 * dst, 2 * (dst - LANES) + 1)
    perm = (src == want).astype(bf16)
    for g in range(2 * D_FF // grp):
        t = jnp.dot(w_ref[:, g * grp:(g + 1) * grp].astype(bf16), perm, preferred_element_type=f32)
        out_ref[:, g * LANES:(g + 1) * LANES] = t[:, :LANES].astype(bf16)
        out_ref[:, D_FF + g * LANES:D_FF + (g + 1) * LANES] = t[:, LANES:].astype(bf16)


def _prep_weights(g_mix, w_in, b_gates, conv_w, g_m_head, g_q, g_k, rel_bias, g_a_out, w_out, g_ffn,
                  w_router, b_router, w_gate_up, b_gate_up, w_down, b_down, g_ple, w_ple_gate, w_ple_proj,
                  g_ple_post):
    g0 = 4 * M_WIDTH
    w = w_in[0]
    pw = {}
    pw["g_mix"] = g_mix[0][None, :]
    pw["w_main"] = jnp.concatenate([w[:, :g0], w[:, g0 + N_GATES:]], axis=1).astype(bf16)
    pw["w_gate"] = jnp.pad(w[:, g0:g0 + N_GATES], ((0, 0), (0, LANES - N_GATES))).astype(bf16)
    pw["b_gate"] = jnp.pad(b_gates[0], (0, LANES - N_GATES))[None, :]
    pw["conv_w"] = conv_w[0]
    pw["g_m"] = g_m_head[0].reshape(1, M_WIDTH)
    pw["g_q2"] = jnp.tile(g_q[0], 2)[None, :]
    pw["g_k2"] = jnp.tile(g_k[0], 2)[None, :]
    pw["bias_tab"] = _natten_bias_table(rel_bias[0])
    pw["g_a"] = g_a_out[0][None, :]
    pw["w_out"] = w_out[0].astype(bf16)
    pw["g_ffn"] = g_ffn[0][None, :]
    wr = jnp.pad(w_router[0], ((0, 0), (0, LANES - N_EXPERTS)))
    wr1 = wr.astype(bf16)
    pw["wr1"] = wr1
    pw["wr2"] = (wr - wr1.astype(f32)).astype(bf16)
    pw["b_r"] = jnp.pad(b_router[0], (0, LANES - N_EXPERTS), constant_values=NEG_INF)[None, :]
    pw["w_gate_up"] = w_gate_up[0]
    bgu = b_gate_up[0]
    pw["b_gu"] = jnp.concatenate([bgu[:, 0::2], bgu[:, 1::2]], axis=-1)[:, None, :]
    pw["w_d"] = w_down[0].astype(bf16)
    pw["b_d"] = b_down[0][:, None, :]
    pw["g_ple"] = g_ple[0][None, :]
    pw["w_ple_gate"] = w_ple_gate[0].astype(bf16)
    pw["w_ple_proj"] = w_ple_proj[0].astype(bf16)
    pw["g_ple_post"] = g_ple_post[0][None, :]
    return pw


def _mix_and_route(x, pw):
    batch, seq_len, _ = x.shape
    n = batch * seq_len
    x2 = x.reshape(n, D_MODEL)
    qk, v_m, o_m, gates, qa, ka, va = _inproj(x2, pw["g_mix"], pw["w_main"], pw["w_gate"], pw["b_gate"],
                                              pw["conv_w"], seq_len)
    hfb = _mlstm(qk, v_m, gates, batch, seq_len)
    ha = _natten(qa, ka, va, pw["bias_tab"], pw["g_q2"], pw["g_k2"], batch, seq_len)
    xmid, hn, ids, pos, gate, cnt = _outproj(x2, hfb, o_m, ha, pw["g_m"], pw["g_a"], pw["w_out"], pw["g_ffn"],
                                             pw["wr1"], pw["wr2"], pw["b_r"], batch, seq_len)

    tm_e = TM_EXP
    counts = cnt[0, :N_EXPERTS].astype(i32)
    padded = (counts + tm_e - 1) // tm_e * tm_e
    pend = jnp.cumsum(padded).astype(i32)
    pstart = pend - padded
    ids_t = ids[:TOP_K, :]
    first = jnp.zeros_like(ids_t)
    for e in range(N_EXPERTS):
        first = jnp.where(ids_t == e, pstart[e], first)
    slot = (first + pos[:TOP_K, :]).T.reshape(-1)
    nk = n * TOP_K
    n_blocks = (nk + N_EXPERTS * (tm_e - 1) + tm_e - 1) // tm_e
    cap = n_blocks * tm_e
    block_start = jnp.arange(n_blocks, dtype=i32) * tm_e
    block_e = jnp.minimum(jnp.sum((pend[None, :] <= block_start[:, None]).astype(i32), axis=1), N_EXPERTS - 1)
    nvalid = (pend[-1:] // tm_e).astype(i32)

    return dict(xmid=xmid, hn=hn, gate=gate, slot=slot, pend=pend, padded=padded, block_e=block_e, nvalid=nvalid,
                cap=cap)


def _finish(r, yb, p, pw, shape):
    n = r["xmid"].shape[0]
    out = _combine(r["slot"], r["xmid"], r["gate"], p.reshape(n, PLE_DIM), pw["w_ple_proj"], pw["g_ple_post"],
                   pw["g_ple"], pw["w_ple_gate"], yb)
    return out.reshape(shape)


def _layer(x_a, p_a, x_b, p_b, pw):
    ra = _mix_and_route(x_a, pw)
    xb_a, w_gu = _dispatch(ra["hn"], ra["slot"], ra["pend"], ra["padded"], ra["cap"], pw["w_gate_up"])
    rb = _mix_and_route(x_b, pw)
    assert rb["cap"] == ra["cap"], "the fused scatter writes a buffer of trunk a's size"
    yb_a, xb_b = _experts(xb_a, ra["block_e"], ra["nvalid"], w_gu, pw["b_gu"], pw["w_d"], pw["b_d"],
                          scatter=(rb["hn"], rb["slot"], rb["pend"], rb["padded"]))
    yb_b = _experts(xb_b, rb["block_e"], rb["nvalid"], w_gu, pw["b_gu"], pw["w_d"], pw["b_d"])
    return _finish(ra, yb_a, p_a, pw, x_a.shape), _finish(rb, yb_b, p_b, pw, x_b.shape)


def kernel(x_prompt, x_sample, p_prompt, p_sample, g_mix, w_in, b_gates, conv_w, g_m_head, g_q, g_k, rel_bias,
           g_a_out, w_out, g_ffn, w_router, b_router, w_gate_up, b_gate_up, w_down, b_down, g_ple, w_ple_gate,
           w_ple_proj, g_ple_post):
    assert w_in.shape[0] == 1, "single-layer trunk"
    pw = _prep_weights(g_mix, w_in, b_gates, conv_w, g_m_head, g_q, g_k, rel_bias, g_a_out, w_out, g_ffn,
                       w_router, b_router, w_gate_up, b_gate_up, w_down, b_down, g_ple, w_ple_gate, w_ple_proj,
                       g_ple_post)
    return _layer(x_prompt, p_prompt[0], x_sample, p_sample[0], pw)
```

```python
import functools

import jax
import jax.numpy as jnp
from jax import lax
from jax.experimental import pallas as pl
from jax.experimental.pallas import tpu as pltpu

f32 = jnp.float32
bf16 = jnp.bfloat16
i32 = jnp.int32

D_MODEL = 1024
HEAD_DIM = 64
M_HEADS = 8
A_HEADS = 8
M_WIDTH = M_HEADS * HEAD_DIM
A_WIDTH = A_HEADS * HEAD_DIM
N_GATES = 4 * M_HEADS
GRID_W = 64
WIN_H = 8
WIN_W = 16
N_EXPERTS = 32
TOP_K = 4
D_FF = 1024
SWIGLU_LIMIT = 7.0
SWIGLU_ALPHA = 1.702
PLE_DIM = 256
EPS = 1e-6

LANES = 128
SUBLANES = 8
MXU_TILE = 256
VMEM_LIMIT = 56 * 1024 * 1024

TM_IN = 512
MLSTM_CHUNK = 256
NAT_ROWS = 16
TM_OUT = 512
TM_EXP = 512
TM_DISPATCH = 256
TM_CMB = 256
CMB_CHUNK = 32

NEG_INF = float("-inf")


def _cparams(sem):
    return pltpu.CompilerParams(dimension_semantics=sem, vmem_limit_bytes=VMEM_LIMIT)


def _rms(xv, g):
    return xv * lax.rsqrt(jnp.mean(xv * xv, axis=-1, keepdims=True) + EPS) * g


def _split2(a):
    hi = a.astype(bf16)
    lo = (a - hi.astype(f32)).astype(bf16)
    return hi, lo


def _split3(a):
    hi = a.astype(bf16)
    r = a - hi.astype(f32)
    mid = r.astype(bf16)
    lo = (r - mid.astype(f32)).astype(bf16)
    return hi, mid, lo


def _head_mean_sq(xv, width):
    grp = min(width, MXU_TILE)
    a = lax.broadcasted_iota(i32, (grp, grp), 0) // HEAD_DIM
    b = lax.broadcasted_iota(i32, (grp, grp), 1) // HEAD_DIM
    bd = jnp.where(a == b, 1.0 / HEAD_DIM, 0.0).astype(bf16)
    hi, lo = _split2(xv * xv)
    parts = []
    for g in range(width // grp):
        sl = slice(g * grp, (g + 1) * grp)
        parts.append(jnp.dot(hi[:, sl], bd, preferred_element_type=f32)
                     + jnp.dot(lo[:, sl], bd, preferred_element_type=f32))
    return parts[0] if len(parts) == 1 else jnp.concatenate(parts, axis=1)


TOKEN_TILE_ROWS = D_MODEL // LANES


def _store_token_tiles(ref, val):
    n = val.shape[0]
    for c in range(TOKEN_TILE_ROWS):
        ref[pl.ds(c, n, stride=TOKEN_TILE_ROWS), :] = val[:, c * LANES:(c + 1) * LANES]


def _token_tile(ref, t):
    return ref.at[pl.ds(pl.multiple_of(t * TOKEN_TILE_ROWS, TOKEN_TILE_ROWS), TOKEN_TILE_ROWS)]


def _load_token_tiles(ref, n, first=0):
    base = first * TOKEN_TILE_ROWS
    return jnp.concatenate([ref[pl.ds(base + c, n, stride=TOKEN_TILE_ROWS), :] for c in range(TOKEN_TILE_ROWS)],
                           axis=1)


def _log_sigmoid(x):
    return jnp.minimum(x, 0.0) - jnp.log1p(jnp.exp(-jnp.abs(x)))


def _inproj_kernel(x_ref, xp_ref, xn_ref, g_ref, wm_ref, wg_ref, bg_ref, cw_ref,
                   qk_ref, v_ref, o_ref, gates_ref, qa_ref, ka_ref, va_ref, *, tm, seq_len):
    i = pl.program_id(0)
    hf = _rms(jnp.concatenate([xp_ref[...], x_ref[...], xn_ref[...]], axis=0), g_ref[...])
    ext = tm + 2 * SUBLANES
    h = hf[SUBLANES:SUBLANES + tm, :].astype(bf16)
    zq = jnp.dot(hf.astype(bf16), wm_ref[:, :2 * M_WIDTH], preferred_element_type=f32)
    z = jnp.dot(h, wm_ref[:, 2 * M_WIDTH:], preferred_element_type=f32)
    zg = jnp.dot(h, wg_ref[...], preferred_element_type=f32) + bg_ref[...]
    gates_ref[...] = zg.T[:N_GATES, :]
    start = lax.rem(i * tm, seq_len)
    rid = lax.broadcasted_iota(i32, (tm, 1), 0)
    u = zq[SUBLANES:SUBLANES + tm, :]
    u_prev = pltpu.roll(zq, 1, 0)[SUBLANES:SUBLANES + tm, :]
    u_next = pltpu.roll(zq, ext - 1, 0)[SUBLANES:SUBLANES + tm, :]
    u_prev = jnp.where(jnp.logical_and(rid == 0, start == 0), 0.0, u_prev)
    u_next = jnp.where(jnp.logical_and(rid == tm - 1, start + tm == seq_len), 0.0, u_next)
    cw = cw_ref[...]
    c = u_prev * cw[0:1, :] + u * cw[1:2, :] + u_next * cw[2:3, :]
    qk_ref[...] = (c * jax.nn.sigmoid(c)).astype(bf16)
    v_ref[...] = z[:, 0:M_WIDTH].T.astype(bf16)
    o_ref[...] = z[:, M_WIDTH:2 * M_WIDTH]
    base = 2 * M_WIDTH
    qa_ref[...] = z[:, base:base + A_WIDTH]
    ka_ref[...] = z[:, base + A_WIDTH:base + 2 * A_WIDTH]
    va_ref[...] = z[:, base + 2 * A_WIDTH:base + 3 * A_WIDTH].astype(bf16)


def _inproj(x2, g_mix, w_main, w_gate, b_gate, conv_w, seq_len):
    n = x2.shape[0]
    tm = TM_IN
    nb8 = n // SUBLANES
    r8 = tm // SUBLANES
    wcols = w_main.shape[1]
    row = lambda i: (i, 0)
    const = lambda i: (0, 0)
    out_shapes = (
        jax.ShapeDtypeStruct((n, 2 * M_WIDTH), bf16),
        jax.ShapeDtypeStruct((M_WIDTH, n), bf16),
        jax.ShapeDtypeStruct((n, M_WIDTH), f32),
        jax.ShapeDtypeStruct((N_GATES, n), f32),
        jax.ShapeDtypeStruct((n, A_WIDTH), f32),
        jax.ShapeDtypeStruct((n, A_WIDTH), f32),
        jax.ShapeDtypeStruct((n, A_WIDTH), bf16),
    )
    return pl.pallas_call(
        functools.partial(_inproj_kernel, tm=tm, seq_len=seq_len),
        grid=(n // tm,),
        in_specs=[
            pl.BlockSpec((tm, D_MODEL), row),
            pl.BlockSpec((SUBLANES, D_MODEL), lambda i: (jnp.maximum(i * r8 - 1, 0), 0)),
            pl.BlockSpec((SUBLANES, D_MODEL), lambda i: (jnp.minimum((i + 1) * r8, nb8 - 1), 0)),
            pl.BlockSpec((1, D_MODEL), const),
            pl.BlockSpec((D_MODEL, wcols), const),
            pl.BlockSpec((D_MODEL, LANES), const),
            pl.BlockSpec((1, LANES), const),
            pl.BlockSpec((3, 2 * M_WIDTH), const),
        ],
        out_specs=[
            pl.BlockSpec((tm, 2 * M_WIDTH), row),
            pl.BlockSpec((M_WIDTH, tm), lambda i: (0, i)),
            pl.BlockSpec((tm, M_WIDTH), row),
            pl.BlockSpec((N_GATES, tm), lambda i: (0, i)),
            pl.BlockSpec((tm, A_WIDTH), row),
            pl.BlockSpec((tm, A_WIDTH), row),
            pl.BlockSpec((tm, A_WIDTH), row),
        ],
        out_shape=out_shapes,
        compiler_params=_cparams(("parallel",)),
        name="inproj",
    )(x2, x2, x2, g_mix, w_main, w_gate, b_gate, conv_w)


def _mlstm_kernel(q_ref, k_ref, vt_ref, g_ref, out_ref, c_ref, m_ref, *, L):
    d = pl.program_id(1)
    c = pl.program_id(2)

    @pl.when(c == 0)
    def _():
        c_ref[...] = jnp.zeros_like(c_ref)
        m_ref[...] = jnp.zeros_like(m_ref)

    fwd = d == 0
    sgn = jnp.where(fwd, 1, -1)
    row = lax.broadcasted_iota(i32, (L, L), 0)
    col = lax.broadcasted_iota(i32, (L, L), 1)
    vis = (col - row) * sgn >= 0
    t_row = vis.astype(bf16)

    g = g_ref[...]
    h8 = M_HEADS
    li = jnp.where(fwd, g[0:h8, :], g[2 * h8:3 * h8, :])
    lf = _log_sigmoid(jnp.where(fwd, g[h8:2 * h8, :], g[3 * h8:4 * h8, :]))
    b = sum(jnp.dot(part, t_row, preferred_element_type=f32) for part in _split3(lf))
    bl = jnp.sum(lf, axis=1, keepdims=True)
    r_row = li - b
    pos = lax.broadcasted_iota(i32, (1, L), 1)
    cm = r_row
    sh = 1
    while sh < L:
        from_before = jnp.where(pos >= sh, pltpu.roll(cm, sh, 1), NEG_INF)
        from_after = jnp.where(pos < L - sh, pltpu.roll(cm, L - sh, 1), NEG_INF)
        cm = jnp.maximum(cm, jnp.where(fwd, from_before, from_after))
        sh *= 2
    m_prev = m_ref[:, 0:1]
    mx_r = jnp.maximum(m_prev, cm)
    m_new = bl + jnp.maximum(m_prev, jnp.max(r_row, axis=1, keepdims=True))
    decay = jnp.exp(bl + m_prev - m_new)
    m_ref[...] = jnp.broadcast_to(m_new, (h8, LANES))
    w_int = jnp.exp(m_prev - mx_r)
    floor = jnp.exp(-(b + mx_r))
    wk = jnp.exp(bl + r_row - m_new)
    r_col = jnp.concatenate([r_row, jnp.zeros((LANES - h8, L), f32)], axis=0).T

    lane = lax.broadcasted_iota(i32, (1, LANES), 1)
    sub = lax.broadcasted_iota(i32, (LANES, 1), 0)
    lane_half = (lane < HEAD_DIM, lane >= HEAD_DIM)
    sub_half = (sub < HEAD_DIM, sub >= HEAD_DIM)
    nt = (((1,), (1,)), ((), ()))

    heads = range(M_HEADS)
    pair = lambda h: slice((h // 2) * LANES, (h // 2 + 1) * LANES)
    qz = [jnp.where(lane_half[h % 2], q_ref[:, pair(h)] * (HEAD_DIM ** -0.5), 0).astype(bf16) for h in heads]
    kz = [jnp.where(lane_half[h % 2], k_ref[:, pair(h)], 0).astype(bf16) for h in heads]
    va_t = [jnp.where(sub_half[h % 2], vt_ref[pair(h), :], 1).astype(bf16) for h in heads]
    kq = [lax.dot_general(kz[h], qz[h], nt, preferred_element_type=f32) for h in heads]
    s_t = [jnp.where(vis, kq[h] * jnp.exp(r_col[:, h:h + 1] - mx_r[h:h + 1, :]), 0.0).astype(bf16) for h in heads]
    cst = [c_ref[h] for h in heads]
    nums = [w_int[h:h + 1, :] * lax.dot_general(cst[h].astype(bf16), qz[h], nt, preferred_element_type=f32)
            + jnp.dot(va_t[h], s_t[h], preferred_element_type=f32) for h in heads]
    for h in heads:
        vw = (va_t[h].astype(f32) * wk[h:h + 1, :]).astype(bf16)
        c_ref[h] = decay[h:h + 1, :] * cst[h] + jnp.dot(vw, kz[h], preferred_element_type=f32)
    for p in range(M_HEADS // 2):
        ev, od = 2 * p, 2 * p + 1
        numer = jnp.where(sub_half[0], nums[ev], nums[od])
        den = jnp.where(sub_half[0], nums[ev][HEAD_DIM:HEAD_DIM + 1, :], nums[od][0:1, :])
        lim = jnp.where(sub_half[0], floor[ev:ev + 1, :], floor[od:od + 1, :])
        out_ref[:, pair(ev)] = (numer / jnp.maximum(jnp.abs(den), lim)).T


def _mlstm(qk, v_t, gates_t, batch, seq_len):
    L = MLSTM_CHUNK
    nc = seq_len // L
    qk3 = qk.reshape(batch, seq_len, 2 * M_WIDTH)
    cidx = lambda d, c: jnp.where(d == 0, c, nc - 1 - c)
    return pl.pallas_call(
        functools.partial(_mlstm_kernel, L=L),
        grid=(batch, 2, nc),
        in_specs=[
            pl.BlockSpec((None, L, M_WIDTH), lambda b, d, c: (b, cidx(d, c), 0)),
            pl.BlockSpec((None, L, M_WIDTH), lambda b, d, c: (b, cidx(d, c), 1)),
            pl.BlockSpec((M_WIDTH, L), lambda b, d, c: (0, b * nc + cidx(d, c))),
            pl.BlockSpec((N_GATES, L), lambda b, d, c: (0, b * nc + cidx(d, c))),
        ],
        out_specs=pl.BlockSpec((None, None, L, M_WIDTH), lambda b, d, c: (b, d, cidx(d, c), 0)),
        out_shape=jax.ShapeDtypeStruct((batch, 2, seq_len, M_WIDTH), f32),
        scratch_shapes=[pltpu.VMEM((M_HEADS, LANES, LANES), f32), pltpu.VMEM((M_HEADS, LANES), f32)],
        compiler_params=_cparams(("parallel", "parallel", "arbitrary")),
        name="mlstm",
    )(qk3, qk3, v_t, gates_t)


def _natten_kernel(q_ref, k_ref, v_ref, bias_ref, gq_ref, gk_ref, out_ref, kn_ref, *, seq_len, rb_rows):
    rb = pl.program_id(2)
    rows = seq_len // GRID_W
    norm_rows = 512

    @pl.when(rb == 0)
    def _():
        def body(i, carry):
            sl = pl.ds(pl.multiple_of(i * norm_rows, norm_rows), norm_rows)
            kv = k_ref[sl, :]
            kn_ref[sl, :] = (kv * lax.rsqrt(_head_mean_sq(kv, LANES) + EPS) * gk_ref[...]).astype(bf16)
            return carry
        lax.fori_loop(0, seq_len // norm_rows, body, 0)

    qv = q_ref[...]
    qn = (qv * lax.rsqrt(_head_mean_sq(qv, LANES) + EPS) * gq_ref[...] * (HEAD_DIM ** -0.5)).astype(bf16)
    lane = lax.broadcasted_iota(i32, (1, LANES), 1)
    lo = lane < HEAD_DIM
    nt = (((1,), (1,)), ((), ()))
    pair = 2 * GRID_W

    def window(j):
        r = rb * rb_rows + j
        r0 = jnp.clip(r - WIN_H // 2, 0, rows - WIN_H)
        return r - r0, pl.ds(pl.multiple_of(r0 * GRID_W, GRID_W), WIN_H * GRID_W)

    tiles = []
    for j in range(rb_rows):
        delta, ks = window(j)
        qj = qn[j * GRID_W:(j + 1) * GRID_W, :]
        q2 = jnp.concatenate([jnp.where(lo, qj, jnp.zeros_like(qj)), jnp.where(lo, jnp.zeros_like(qj), qj)], axis=0)
        tiles.append(lax.dot_general(q2, kn_ref[ks, :], nt, preferred_element_type=f32) + bias_ref[delta])
    s = jnp.concatenate(tiles, axis=0)
    p = jnp.exp(s - jnp.max(s, axis=-1, keepdims=True))
    inv = 1.0 / jnp.sum(p, axis=-1, keepdims=True)
    pb = p.astype(bf16)
    for j in range(rb_rows):
        _, ks = window(j)
        o = jnp.dot(pb[j * pair:(j + 1) * pair, :], v_ref[ks, :], preferred_element_type=f32)
        o = o * inv[j * pair:(j + 1) * pair, :]
        out_ref[j * GRID_W:(j + 1) * GRID_W, :] = jnp.where(lo, o[:GRID_W, :], o[GRID_W:, :])


def _natten_bias_table(rel_bias):
    cq = jnp.arange(GRID_W)[:, None]
    ck = jnp.arange(GRID_W)[None, :]
    c0 = jnp.clip(cq - WIN_W // 2, 0, GRID_W - WIN_W)
    col_in = (ck >= c0) & (ck < c0 + WIN_W)
    idx_c = jnp.clip(ck - cq, -(WIN_W - 1), WIN_W - 1) + (WIN_W - 1)
    pick = idx_c[:, :, None] == jnp.arange(2 * WIN_W - 1)
    tz = jnp.sum(jnp.where(pick[None, None], rel_bias.astype(f32)[:, :, None, None, :], 0.0), axis=-1)
    tz = jnp.where(col_in[None, None], tz, NEG_INF)
    tab = jnp.stack([tz[:, WIN_H - 1 - dl:2 * WIN_H - 1 - dl] for dl in range(WIN_H)], axis=1)
    tab = tab.transpose(0, 1, 3, 2, 4).reshape(A_HEADS // 2, 2, WIN_H, GRID_W, WIN_H * GRID_W)
    return tab.transpose(0, 2, 1, 3, 4).reshape(A_HEADS // 2, WIN_H, 2 * GRID_W, WIN_H * GRID_W)


def _natten(qa, ka, va, bias_tab, g_q2, g_k2, batch, seq_len):
    rows = seq_len // GRID_W
    rbr = NAT_ROWS
    tq = rbr * GRID_W
    q3 = qa.reshape(batch, seq_len, A_WIDTH)
    k3 = ka.reshape(batch, seq_len, A_WIDTH)
    v3 = va.reshape(batch, seq_len, A_WIDTH)
    return pl.pallas_call(
        functools.partial(_natten_kernel, seq_len=seq_len, rb_rows=rbr),
        grid=(batch, A_HEADS // 2, rows // rbr),
        in_specs=[
            pl.BlockSpec((None, tq, LANES), lambda b, hp, rb: (b, rb, hp)),
            pl.BlockSpec((None, seq_len, LANES), lambda b, hp, rb: (b, 0, hp)),
            pl.BlockSpec((None, seq_len, LANES), lambda b, hp, rb: (b, 0, hp)),
            pl.BlockSpec((None, WIN_H, 2 * GRID_W, WIN_H * GRID_W), lambda b, hp, rb: (hp, 0, 0, 0)),
            pl.BlockSpec((1, LANES), lambda b, hp, rb: (0, 0)),
            pl.BlockSpec((1, LANES), lambda b, hp, rb: (0, 0)),
        ],
        out_specs=pl.BlockSpec((None, tq, LANES), lambda b, hp, rb: (b, rb, hp)),
        out_shape=jax.ShapeDtypeStruct((batch, seq_len, A_WIDTH), f32),
        scratch_shapes=[pltpu.VMEM((seq_len, LANES), bf16)],
        compiler_params=_cparams(("parallel", "parallel", "arbitrary")),
        name="natten",
    )(q3, k3, v3, bias_tab, g_q2, g_k2)


def _outproj_kernel(x_ref, hf_ref, hb_ref, o_ref, ha_ref, gm_ref, ga_ref, wo_ref, gf_ref,
                    wr1_ref, wr2_ref, br_ref,
                    xmid_ref, hn_ref, ids_ref, pos_ref, gate_ref, cnt_ref, carry_ref, *, tm):
    i = pl.program_id(0)

    @pl.when(i == 0)
    def _():
        carry_ref[...] = jnp.zeros_like(carry_ref)

    hm = hf_ref[...] + hb_ref[...]
    hm = hm * lax.rsqrt(_head_mean_sq(hm, M_WIDTH) + EPS) * gm_ref[...] * jax.nn.sigmoid(o_ref[...])
    ha = _rms(ha_ref[...], ga_ref[...])
    mix = (jnp.dot(hm.astype(bf16), wo_ref[0:M_WIDTH, :], preferred_element_type=f32)
           + jnp.dot(ha.astype(bf16), wo_ref[M_WIDTH:M_WIDTH + A_WIDTH, :], preferred_element_type=f32))
    xm = x_ref[...] + mix
    xmid_ref[...] = xm
    hn = _rms(xm, gf_ref[...])
    _store_token_tiles(hn_ref, hn)

    h1, h2 = _split2(hn)
    logits = (jnp.dot(h1, wr1_ref[...], preferred_element_type=f32)
              + (jnp.dot(h1, wr2_ref[...], preferred_element_type=f32)
                 + jnp.dot(h2, wr1_ref[...], preferred_element_type=f32))) + br_ref[...]
    lane = lax.broadcasted_iota(i32, (tm, LANES), 1)
    work = logits
    vals, idxs, sels = [], [], []
    for _ in range(TOP_K):
        mx = jnp.max(work, axis=-1, keepdims=True)
        idx = jnp.min(jnp.where(work == mx, lane, LANES), axis=-1, keepdims=True)
        sel = lane == idx
        vals.append(mx)
        idxs.append(idx)
        sels.append(sel)
        work = jnp.where(sel, NEG_INF, work)
    es = [jnp.exp(v - vals[0]) for v in vals]
    tot = es[0] + es[1] + es[2] + es[3]

    onehot = jnp.where(sels[0] | sels[1] | sels[2] | sels[3], 1.0, 0.0)
    tri = (lax.broadcasted_iota(i32, (tm, tm), 0) > lax.broadcasted_iota(i32, (tm, tm), 1)).astype(bf16)
    base = jnp.dot(tri, onehot.astype(bf16), preferred_element_type=f32) + carry_ref[...]
    ids_out = jnp.zeros((tm, LANES), i32)
    pos_out = jnp.zeros((tm, LANES), i32)
    gate_out = jnp.zeros((tm, LANES), f32)
    for k in range(TOP_K):
        pk = jnp.sum(jnp.where(sels[k], base, 0.0), axis=-1, keepdims=True).astype(i32)
        ids_out = jnp.where(lane == k, idxs[k], ids_out)
        pos_out = jnp.where(lane == k, pk, pos_out)
        gate_out = jnp.where(lane == k, es[k] / tot, gate_out)
    ids_ref[...] = ids_out.T[:SUBLANES, :]
    pos_ref[...] = pos_out.T[:SUBLANES, :]
    gate_ref[...] = gate_out[:, :TOP_K]
    carry_ref[...] += jnp.sum(onehot, axis=0, keepdims=True)
    cnt_ref[...] = carry_ref[...]


def _outproj(x2, hfb, o_m, ha, g_m, g_a, w_out, g_ffn, wr1, wr2, b_r, batch, seq_len):
    n = x2.shape[0]
    tm = TM_OUT
    tpb = seq_len // tm
    row = lambda i: (i, 0)
    const = lambda i: (0, 0)
    ha2 = ha.reshape(n, A_WIDTH)
    return pl.pallas_call(
        functools.partial(_outproj_kernel, tm=tm),
        grid=(n // tm,),
        in_specs=[
            pl.BlockSpec((tm, D_MODEL), row),
            pl.BlockSpec((None, None, tm, M_WIDTH), lambda i: (i // tpb, 0, i % tpb, 0)),
            pl.BlockSpec((None, None, tm, M_WIDTH), lambda i: (i // tpb, 1, i % tpb, 0)),
            pl.BlockSpec((tm, M_WIDTH), row),
            pl.BlockSpec((tm, A_WIDTH), row),
            pl.BlockSpec((1, M_WIDTH), const),
            pl.BlockSpec((1, A_WIDTH), const),
            pl.BlockSpec((M_WIDTH + A_WIDTH, D_MODEL), const),
            pl.BlockSpec((1, D_MODEL), const),
            pl.BlockSpec((D_MODEL, LANES), const),
            pl.BlockSpec((D_MODEL, LANES), const),
            pl.BlockSpec((1, LANES), const),
        ],
        out_specs=[
            pl.BlockSpec((tm, D_MODEL), row),
            pl.BlockSpec((tm * TOKEN_TILE_ROWS, LANES), row),
            pl.BlockSpec((SUBLANES, tm), lambda i: (0, i)),
            pl.BlockSpec((SUBLANES, tm), lambda i: (0, i)),
            pl.BlockSpec((tm, TOP_K), row),
            pl.BlockSpec((1, LANES), const),
        ],
        out_shape=(
            jax.ShapeDtypeStruct((n, D_MODEL), f32),
            jax.ShapeDtypeStruct((n * TOKEN_TILE_ROWS, LANES), f32),
            jax.ShapeDtypeStruct((SUBLANES, n), i32),
            jax.ShapeDtypeStruct((SUBLANES, n), i32),
            jax.ShapeDtypeStruct((n, TOP_K), f32),
            jax.ShapeDtypeStruct((1, LANES), f32),
        ),
        scratch_shapes=[pltpu.VMEM((1, LANES), f32)],
        compiler_params=_cparams(("arbitrary",)),
        name="outproj_router",
    )(x2, hfb, hfb, o_m, ha2, g_m, g_a, w_out, g_ffn, wr1, wr2, b_r)


def _scatter_clear(pend_ref, padded_ref, xb_hbm, zero_ref, sem, tm_e, n_blocks):
    blk = tm_e * TOKEN_TILE_ROWS

    def zero_copy(e):
        start = (pend_ref[e] - tm_e) * TOKEN_TILE_ROWS
        return pltpu.make_async_copy(zero_ref, xb_hbm.at[pl.ds(pl.multiple_of(start, blk), blk)], sem)

    def tail_copy(b):
        return pltpu.make_async_copy(zero_ref, xb_hbm.at[pl.ds(pl.multiple_of(b * blk, blk), blk)], sem)

    zero_ref[...] = jnp.zeros_like(zero_ref)
    for e in range(N_EXPERTS):
        @pl.when(padded_ref[e] > 0)
        def _():
            zero_copy(e).start()
    used = pend_ref[N_EXPERTS - 1] // tm_e

    def tail_start(b, carry):
        tail_copy(b).start()
        return carry

    def tail_wait(b, carry):
        tail_copy(b).wait()
        return carry

    lax.fori_loop(used, n_blocks, tail_start, 0)
    for e in range(N_EXPERTS):
        @pl.when(padded_ref[e] > 0)
        def _():
            zero_copy(e).wait()
    lax.fori_loop(used, n_blocks, tail_wait, 0)


def _scatter_rows(slot_ref, hn_ref, xb_hbm, sem, tm):
    def row(j, k):
        return pltpu.make_async_copy(_token_tile(hn_ref, j), _token_tile(xb_hbm, slot_ref[0, 0, j * TOP_K + k]), sem)

    def issue(j, carry):
        for k in range(TOP_K):
            row(j, k).start(priority=k % 2)
        return carry

    def drain(j, carry):
        for k in range(TOP_K):
            row(0, k).wait()
        return carry

    def start(inline=False):
        if inline:
            for j in range(tm):
                issue(j, 0)
        else:
            lax.fori_loop(0, tm, issue, 0, unroll=4)

    wait = lambda: lax.fori_loop(0, tm, drain, 0, unroll=4)
    return start, wait


def _dispatch_kernel(pend_ref, padded_ref, slot_ref, hn_ref, w_ref, xb_hbm, wout_ref, zero_ref, sem, *,
                     tm, tm_e, n_blocks):
    @pl.when(pl.program_id(0) == 0)
    def _():
        _scatter_clear(pend_ref, padded_ref, xb_hbm, zero_ref, sem, tm_e, n_blocks)

    start, wait = _scatter_rows(slot_ref, hn_ref, xb_hbm, sem, tm)
    start()
    _wprep_kernel(w_ref, wout_ref)
    wait()


def _dispatch(hn8, slot_flat, pend, padded, cap, w_gate_up):
    n = hn8.shape[0] // TOKEN_TILE_ROWS
    tm = TM_DISPATCH
    steps = n // tm
    ne = w_gate_up.shape[0]
    halves = steps // ne
    assert halves * ne == steps and D_MODEL % halves == 0, "one weight slab per grid step"
    wspec = pl.BlockSpec((None, D_MODEL // halves, 2 * D_FF), lambda i, pe, pa: (i // halves, i % halves, 0))
    grid_spec = pltpu.PrefetchScalarGridSpec(
        num_scalar_prefetch=2,
        grid=(steps,),
        in_specs=[
            pl.BlockSpec((1, 1, tm * TOP_K), lambda i, pe, pa: (i, 0, 0), memory_space=pltpu.SMEM),
            pl.BlockSpec((tm * TOKEN_TILE_ROWS, LANES), lambda i, pe, pa: (i, 0)),
            wspec,
        ],
        out_specs=[pl.BlockSpec(memory_space=pl.ANY), wspec],
        scratch_shapes=[pltpu.VMEM((TM_EXP * TOKEN_TILE_ROWS, LANES), f32), pltpu.SemaphoreType.DMA(())],
    )
    return pl.pallas_call(
        functools.partial(_dispatch_kernel, tm=tm, tm_e=TM_EXP, n_blocks=cap // TM_EXP),
        grid_spec=grid_spec,
        out_shape=[jax.ShapeDtypeStruct((cap * TOKEN_TILE_ROWS, LANES), f32),
                   jax.ShapeDtypeStruct((ne, D_MODEL, 2 * D_FF), bf16)],
        compiler_params=_cparams(("arbitrary",)),
        name="dispatch_wprep",
    )(pend, padded, slot_flat.reshape(steps, 1, tm * TOP_K), hn8, w_gate_up)


def _experts_kernel(be_ref, nv_ref, *rest, tm, n_blocks, scatter_steps):
    if scatter_steps:
        (pend2_ref, padded2_ref, xb_ref, wgu_ref, bgu_ref, wd_ref, bd_ref, slot2_ref, hn2_ref,
         yb_ref, xb2_hbm, zero_ref, sem) = rest
        start, wait = _scatter_rows(slot2_ref, hn2_ref, xb2_hbm, sem, tm // TOP_K)
    else:
        xb_ref, wgu_ref, bgu_ref, wd_ref, bd_ref, yb_ref = rest
    j = pl.program_id(0)

    if scatter_steps:
        @pl.when(j == 0)
        def _():
            _scatter_clear(pend2_ref, padded2_ref, xb2_hbm, zero_ref, sem, tm, n_blocks)

    @pl.when(j < nv_ref[0])
    def _():
        if scatter_steps:
            start(inline=True)
        xv = _load_token_tiles(xb_ref, tm).astype(bf16)
        h = jnp.dot(xv, wgu_ref[...], preferred_element_type=f32) + bgu_ref[...]
        gt = jnp.minimum(h[:, :D_FF], SWIGLU_LIMIT)
        up = jnp.clip(h[:, D_FF:], -SWIGLU_LIMIT, SWIGLU_LIMIT)
        act = (up + 1.0) * (gt * jax.nn.sigmoid(SWIGLU_ALPHA * gt))
        _store_token_tiles(yb_ref, jnp.dot(act.astype(bf16), wd_ref[...], preferred_element_type=f32) + bd_ref[...])
        if scatter_steps:
            wait()

    @pl.when(j >= nv_ref[0])
    def _():
        yb_ref[...] = jnp.zeros_like(yb_ref)


def _experts(xb, block_e, nvalid, w_gu, b_gu, w_d, b_d, scatter=None):
    cap = xb.shape[0] // TOKEN_TILE_ROWS
    tm = TM_EXP
    n_blocks = cap // tm
    nsp = 4 if scatter is not None else 2
    blk = lambda f: (lambda j, *pref: f(j, *pref[:2]))
    in_specs = [
        pl.BlockSpec((tm * TOKEN_TILE_ROWS, LANES), blk(lambda j, be, nv: (jnp.minimum(j, nv[0] - 1), 0))),
        pl.BlockSpec((None, D_MODEL, 2 * D_FF), blk(lambda j, be, nv: (be[j], 0, 0))),
        pl.BlockSpec((None, 1, 2 * D_FF), blk(lambda j, be, nv: (be[j], 0, 0))),
        pl.BlockSpec((None, D_FF, D_MODEL), blk(lambda j, be, nv: (be[j], 0, 0))),
        pl.BlockSpec((None, 1, D_MODEL), blk(lambda j, be, nv: (be[j], 0, 0))),
    ]
    out_specs = [pl.BlockSpec((tm * TOKEN_TILE_ROWS, LANES), blk(lambda j, be, nv: (j, 0)))]
    out_shape = [jax.ShapeDtypeStruct((cap * TOKEN_TILE_ROWS, LANES), f32)]
    prefetch = [block_e, nvalid]
    operands = [xb, w_gu, b_gu, w_d, b_d]
    scratch = []
    steps = 0
    if scatter is not None:
        hn8, slot_flat, pend2, padded2 = scatter
        tms = tm // TOP_K
        steps = hn8.shape[0] // TOKEN_TILE_ROWS // tms
        assert steps * tms * TOKEN_TILE_ROWS == hn8.shape[0] and steps <= n_blocks
        last = steps - 1
        prefetch += [pend2, padded2]
        in_specs += [
            pl.BlockSpec((1, 1, tm), blk(lambda j, be, nv: (jnp.minimum(j, last), 0, 0)), memory_space=pltpu.SMEM),
            pl.BlockSpec((tms * TOKEN_TILE_ROWS, LANES), blk(lambda j, be, nv: (jnp.minimum(j, last), 0))),
        ]
        operands += [slot_flat.reshape(steps, 1, tm), hn8]
        out_specs.append(pl.BlockSpec(memory_space=pl.ANY))
        out_shape.append(jax.ShapeDtypeStruct((cap * TOKEN_TILE_ROWS, LANES), f32))
        scratch = [pltpu.VMEM((tm * TOKEN_TILE_ROWS, LANES), f32), pltpu.SemaphoreType.DMA(())]
    grid_spec = pltpu.PrefetchScalarGridSpec(
        num_scalar_prefetch=nsp, grid=(n_blocks,), in_specs=in_specs, out_specs=out_specs, scratch_shapes=scratch)
    outs = pl.pallas_call(
        functools.partial(_experts_kernel, tm=tm, n_blocks=n_blocks, scatter_steps=steps),
        grid_spec=grid_spec,
        out_shape=out_shape,
        compiler_params=_cparams(("arbitrary",)),
        name="experts_scatter" if scatter is not None else "experts",
    )(*prefetch, *operands)
    return outs if scatter is not None else outs[0]


def _combine_kernel(slot_ref, slot_next_ref, xmid_ref, gate_ref, p_ref, wproj_ref, gpost_ref, gple_ref, wgate_ref,
                    yb_hbm, out_ref, ybuf_ref, y_ref, sem, *, tm):
    i = pl.program_id(0)
    cur = lax.rem(i, 2)
    nxt = 1 - cur

    def row(slots, buf, j, k):
        return pltpu.make_async_copy(_token_tile(yb_hbm, slots[0, 0, j * TOP_K + k]),
                                     _token_tile(ybuf_ref.at[buf, k], j), sem.at[buf])

    def drain(buf):
        def wait(j, carry):
            for k in range(TOP_K):
                row(slot_ref, buf, 0, k).wait()
            return carry
        lax.fori_loop(0, tm, wait, 0, unroll=4)

    @pl.when(i == 0)
    def _():
        def issue(j, carry):
            for k in range(TOP_K):
                row(slot_ref, 0, j, k).start(priority=k % 2)
            return carry
        lax.fori_loop(0, tm, issue, 0, unroll=4)

    pe = _rms(jnp.dot(p_ref[...].astype(bf16), wproj_ref[...], preferred_element_type=f32), gpost_ref[...])
    drain(cur)

    half_chunk = CMB_CHUNK // 2

    def chunk_body(cb, carry):
        j0 = pl.multiple_of(cb * CMB_CHUNK, CMB_CHUNK)
        for jj in range(half_chunk):
            for k in range(TOP_K):
                row(slot_next_ref, nxt, cb * half_chunk + jj, k).start(priority=k % 2)
        gate = gate_ref[pl.ds(j0, CMB_CHUNK), :]
        acc = gate[:, 0:1] * _load_token_tiles(ybuf_ref.at[cur, 0], CMB_CHUNK, j0)
        for k in range(1, TOP_K):
            acc = acc + gate[:, k:k + 1] * _load_token_tiles(ybuf_ref.at[cur, k], CMB_CHUNK, j0)
        y_ref[pl.ds(j0, CMB_CHUNK), :] = acc
        return carry

    lax.fori_loop(0, tm // CMB_CHUNK, chunk_body, 0)
    for j in range(tm // 2, tm):
        for k in range(TOP_K):
            row(slot_next_ref, nxt, j, k).start(priority=k % 2)

    x2 = xmid_ref[...] + y_ref[...]
    gl = jnp.dot(_rms(x2, gple_ref[...]).astype(bf16), wgate_ref[...], preferred_element_type=f32)
    out_ref[...] = x2 + jax.nn.sigmoid(gl) * pe

    @pl.when(i + 1 == pl.num_programs(0))
    def _():
        drain(nxt)


def _combine(slot_flat, xmid, gate, p2, w_proj, g_post, g_ple, w_gate, yb):
    n = xmid.shape[0]
    tm = TM_CMB
    steps = n // tm
    slot3 = slot_flat.reshape(steps, 1, tm * TOP_K)
    row = lambda i: (i, 0)
    const = lambda i: (0, 0)
    return pl.pallas_call(
        functools.partial(_combine_kernel, tm=tm),
        grid=(steps,),
        in_specs=[
            pl.BlockSpec((1, 1, tm * TOP_K), lambda i: (i, 0, 0), memory_space=pltpu.SMEM),
            pl.BlockSpec((1, 1, tm * TOP_K), lambda i: (jnp.minimum(i + 1, steps - 1), 0, 0),
                         memory_space=pltpu.SMEM),
            pl.BlockSpec((tm, D_MODEL), row),
            pl.BlockSpec((tm, TOP_K), row),
            pl.BlockSpec((tm, PLE_DIM), row),
            pl.BlockSpec((PLE_DIM, D_MODEL), const),
            pl.BlockSpec((1, D_MODEL), const),
            pl.BlockSpec((1, D_MODEL), const),
            pl.BlockSpec((D_MODEL, D_MODEL), const),
            pl.BlockSpec(memory_space=pl.ANY),
        ],
        out_specs=pl.BlockSpec((tm, D_MODEL), row),
        out_shape=jax.ShapeDtypeStruct((n, D_MODEL), f32),
        scratch_shapes=[pltpu.VMEM((2, TOP_K, tm * TOKEN_TILE_ROWS, LANES), f32), pltpu.VMEM((tm, D_MODEL), f32),
                        pltpu.SemaphoreType.DMA((2,))],
        compiler_params=_cparams(("arbitrary",)),
        name="combine_ple",
    )(slot3, slot3, xmid, gate, p2, w_proj, g_post, g_ple, w_gate, yb)


def _wprep_kernel(w_ref, out_ref):
    grp = 2 * LANES
    src = lax.broadcasted_iota(i32, (grp, grp), 0)
    dst = lax.broadcasted_iota(i32, (grp, grp), 1)
    want = jnp.where(dst < LANES, 2 * dst, 2 * (dst - LANES) + 1)
    perm = (src == want).astype(bf16)
    for g in range(2 * D_FF // grp):
        t = jnp.dot(w_ref[:, g * grp:(g + 1) * grp].astype(bf16), perm, preferred_element_type=f32)
        out_ref[:, g * LANES:(g + 1) * LANES] = t[:, :LANES].astype(bf16)
        out_ref[:, D_FF + g * LANES:D_FF + (g + 1) * LANES] = t[:, LANES:].astype(bf16)


def _prep_weights(g_mix, w_in, b_gates, conv_w, g_m_head, g_q, g_k, rel_bias, g_a_out, w_out, g_ffn,
                  w_router, b_router, w_gate_up, b_gate_up, w_down, b_down, g_ple, w_ple_gate, w_ple_proj,
                  g_ple_post):
    g0 = 4 * M_WIDTH
    w = w_in[0]
    pw = {}
    pw["g_mix"] = g_mix[0][None, :]
    pw["w_main"] = jnp.concatenate([w[:, :g0], w[:, g0 + N_GATES:]], axis=1).astype(bf16)
    pw["w_gate"] = jnp.pad(w[:, g0:g0 + N_GATES], ((0, 0), (0, LANES - N_GATES))).astype(bf16)
    pw["b_gate"] = jnp.pad(b_gates[0], (0, LANES - N_GATES))[None, :]
    pw["conv_w"] = conv_w[0]
    pw["g_m"] = g_m_head[0].reshape(1, M_WIDTH)
    pw["g_q2"] = jnp.tile(g_q[0], 2)[None, :]
    pw["g_k2"] = jnp.tile(g_k[0], 2)[None, :]
    pw["bias_tab"] = _natten_bias_table(rel_bias[0])
    pw["g_a"] = g_a_out[0][None, :]
    pw["w_out"] = w_out[0].astype(bf16)
    pw["g_ffn"] = g_ffn[0][None, :]
    wr = jnp.pad(w_router[0], ((0, 0), (0, LANES - N_EXPERTS)))
    wr1 = wr.astype(bf16)
    pw["wr1"] = wr1
    pw["wr2"] = (wr - wr1.astype(f32)).astype(bf16)
    pw["b_r"] = jnp.pad(b_router[0], (0, LANES - N_EXPERTS), constant_values=NEG_INF)[None, :]
    pw["w_gate_up"] = w_gate_up[0]
    bgu = b_gate_up[0]
    pw["b_gu"] = jnp.concatenate([bgu[:, 0::2], bgu[:, 1::2]], axis=-1)[:, None, :]
    pw["w_d"] = w_down[0].astype(bf16)
    pw["b_d"] = b_down[0][:, None, :]
    pw["g_ple"] = g_ple[0][None, :]
    pw["w_ple_gate"] = w_ple_gate[0].astype(bf16)
    pw["w_ple_proj"] = w_ple_proj[0].astype(bf16)
    pw["g_ple_post"] = g_ple_post[0][None, :]
    return pw


def _mix_and_route(x, pw):
    batch, seq_len, _ = x.shape
    n = batch * seq_len
    x2 = x.reshape(n, D_MODEL)
    qk, v_m, o_m, gates, qa, ka, va = _inproj(x2, pw["g_mix"], pw["w_main"], pw["w_gate"], pw["b_gate"],
                                              pw["conv_w"], seq_len)
    hfb = _mlstm(qk, v_m, gates, batch, seq_len)
    ha = _natten(qa, ka, va, pw["bias_tab"], pw["g_q2"], pw["g_k2"], batch, seq_len)
    xmid, hn, ids, pos, gate, cnt = _outproj(x2, hfb, o_m, ha, pw["g_m"], pw["g_a"], pw["w_out"], pw["g_ffn"],
                                             pw["wr1"], pw["wr2"], pw["b_r"], batch, seq_len)

    tm_e = TM_EXP
    counts = cnt[0, :N_EXPERTS].astype(i32)
    padded = (counts + tm_e - 1) // tm_e * tm_e
    pend = jnp.cumsum(padded).astype(i32)
    pstart = pend - padded
    ids_t = ids[:TOP_K, :]
    first = jnp.zeros_like(ids_t)
    for e in range(N_EXPERTS):
        first = jnp.where(ids_t == e, pstart[e], first)
    slot = (first + pos[:TOP_K, :]).T.reshape(-1)
    nk = n * TOP_K
    n_blocks = (nk + N_EXPERTS * (tm_e - 1) + tm_e - 1) // tm_e
    cap = n_blocks * tm_e
    block_start = jnp.arange(n_blocks, dtype=i32) * tm_e
    block_e = jnp.minimum(jnp.sum((pend[None, :] <= block_start[:, None]).astype(i32), axis=1), N_EXPERTS - 1)
    nvalid = (pend[-1:] // tm_e).astype(i32)

    return dict(xmid=xmid, hn=hn, gate=gate, slot=slot, pend=pend, padded=padded, block_e=block_e, nvalid=nvalid,
                cap=cap)


def _finish(r, yb, p, pw, shape):
    n = r["xmid"].shape[0]
    out = _combine(r["slot"], r["xmid"], r["gate"], p.reshape(n, PLE_DIM), pw["w_ple_proj"], pw["g_ple_post"],
                   pw["g_ple"], pw["w_ple_gate"], yb)
    return out.reshape(shape)


def _layer(x_a, p_a, x_b, p_b, pw):
    ra = _mix_and_route(x_a, pw)
    xb_a, w_gu = _dispatch(ra["hn"], ra["slot"], ra["pend"], ra["padded"], ra["cap"], pw["w_gate_up"])
    rb = _mix_and_route(x_b, pw)
    assert rb["cap"] == ra["cap"], "the fused scatter writes a buffer of trunk a's size"
    yb_a, xb_b = _experts(xb_a, ra["block_e"], ra["nvalid"], w_gu, pw["b_gu"], pw["w_d"], pw["b_d"],
                          scatter=(rb["hn"], rb["slot"], rb["pend"], rb["padded"]))
    yb_b = _experts(xb_b, rb["block_e"], rb["nvalid"], w_gu, pw["b_gu"], pw["w_d"], pw["b_d"])
    return _finish(ra, yb_a, p_a, pw, x_a.shape), _finish(rb, yb_b, p_b, pw, x_b.shape)


def kernel(x_prompt, x_sample, p_prompt, p_sample, g_mix, w_in, b_gates, conv_w, g_m_head, g_q, g_k, rel_bias,
           g_a_out, w_out, g_ffn, w_router, b_router, w_gate_up, b_gate_up, w_down, b_down, g_ple, w_ple_gate,
           w_ple_proj, g_ple_post):
    assert w_in.shape[0] == 1, "single-layer trunk"
    pw = _prep_weights(g_mix, w_in, b_gates, conv_w, g_m_head, g_q, g_k, rel_bias, g_a_out, w_out, g_ffn,
                       w_router, b_router, w_gate_up, b_gate_up, w_down, b_down, g_ple, w_ple_gate, w_ple_proj,
                       g_ple_post)
    return _layer(x_prompt, p_prompt[0], x_sample, p_sample[0], pw)
```

```python
import functools

import jax
import jax.numpy as jnp
from jax import lax
from jax.experimental import pallas as pl
from jax.experimental.pallas import tpu as pltpu

f32 = jnp.float32
bf16 = jnp.bfloat16
i32 = jnp.int32

D_MODEL = 1024
HEAD_DIM = 64
M_HEADS = 8
A_HEADS = 8
M_WIDTH = M_HEADS * HEAD_DIM
A_WIDTH = A_HEADS * HEAD_DIM
N_GATES = 4 * M_HEADS
GRID_W = 64
WIN_H = 8
WIN_W = 16
N_EXPERTS = 32
TOP_K = 4
D_FF = 1024
SWIGLU_LIMIT = 7.0
SWIGLU_ALPHA = 1.702
PLE_DIM = 256
EPS = 1e-6

LANES = 128
SUBLANES = 8
MXU_TILE = 256
VMEM_LIMIT = 56 * 1024 * 1024

TM_IN = 512
MLSTM_CHUNK = 256
NAT_ROWS = 16
NAT_BATCHES = 2
TM_OUT = 512
TM_EXP = 512
TM_DISPATCH = 256
TM_CMB = 256
CMB_CHUNK = 32

NEG_INF = float("-inf")


def _cparams(sem):
    return pltpu.CompilerParams(dimension_semantics=sem, vmem_limit_bytes=VMEM_LIMIT)


def _rms(xv, g):
    return xv * lax.rsqrt(jnp.mean(xv * xv, axis=-1, keepdims=True) + EPS) * g


def _split2(a):
    hi = a.astype(bf16)
    lo = (a - hi.astype(f32)).astype(bf16)
    return hi, lo


def _split3(a):
    hi = a.astype(bf16)
    r = a - hi.astype(f32)
    mid = r.astype(bf16)
    lo = (r - mid.astype(f32)).astype(bf16)
    return hi, mid, lo


def _head_mean_sq(xv, width):
    grp = min(width, MXU_TILE)
    a = lax.broadcasted_iota(i32, (grp, grp), 0) // HEAD_DIM
    b = lax.broadcasted_iota(i32, (grp, grp), 1) // HEAD_DIM
    bd = jnp.where(a == b, 1.0 / HEAD_DIM, 0.0).astype(bf16)
    hi, lo = _split2(xv * xv)
    parts = []
    for g in range(width // grp):
        sl = slice(g * grp, (g + 1) * grp)
        parts.append(jnp.dot(hi[:, sl], bd, preferred_element_type=f32)
                     + jnp.dot(lo[:, sl], bd, preferred_element_type=f32))
    return parts[0] if len(parts) == 1 else jnp.concatenate(parts, axis=1)


TOKEN_TILE_ROWS = D_MODEL // LANES


def _store_token_tiles(ref, val):
    n = val.shape[0]
    for c in range(TOKEN_TILE_ROWS):
        ref[pl.ds(c, n, stride=TOKEN_TILE_ROWS), :] = val[:, c * LANES:(c + 1) * LANES]


def _token_tile(ref, t):
    return ref.at[pl.ds(pl.multiple_of(t * TOKEN_TILE_ROWS, TOKEN_TILE_ROWS), TOKEN_TILE_ROWS)]


def _load_token_tiles(ref, n, first=0):
    base = first * TOKEN_TILE_ROWS
    return jnp.concatenate([ref[pl.ds(base + c, n, stride=TOKEN_TILE_ROWS), :] for c in range(TOKEN_TILE_ROWS)],
                           axis=1)


def _log_sigmoid(x):
    return jnp.minimum(x, 0.0) - jnp.log1p(jnp.exp(-jnp.abs(x)))


def _inproj_kernel(x_ref, xp_ref, xn_ref, g_ref, wm_ref, wg_ref, bg_ref, cw_ref,
                   qk_ref, v_ref, o_ref, gates_ref, qa_ref, ka_ref, va_ref, *, tm, seq_len):
    i = pl.program_id(0)
    hf = _rms(jnp.concatenate([xp_ref[...], x_ref[...], xn_ref[...]], axis=0), g_ref[...])
    ext = tm + 2 * SUBLANES
    h = hf[SUBLANES:SUBLANES + tm, :].astype(bf16)
    zq = jnp.dot(hf.astype(bf16), wm_ref[:, :2 * M_WIDTH], preferred_element_type=f32)
    z = jnp.dot(h, wm_ref[:, 2 * M_WIDTH:], preferred_element_type=f32)
    zg = jnp.dot(h, wg_ref[...], preferred_element_type=f32) + bg_ref[...]
    gates_ref[...] = zg.T[:N_GATES, :]
    start = lax.rem(i * tm, seq_len)
    rid = lax.broadcasted_iota(i32, (tm, 1), 0)
    u = zq[SUBLANES:SUBLANES + tm, :]
    u_prev = pltpu.roll(zq, 1, 0)[SUBLANES:SUBLANES + tm, :]
    u_next = pltpu.roll(zq, ext - 1, 0)[SUBLANES:SUBLANES + tm, :]
    u_prev = jnp.where(jnp.logical_and(rid == 0, start == 0), 0.0, u_prev)
    u_next = jnp.where(jnp.logical_and(rid == tm - 1, start + tm == seq_len), 0.0, u_next)
    cw = cw_ref[...]
    c = u_prev * cw[0:1, :] + u * cw[1:2, :] + u_next * cw[2:3, :]
    qk_ref[...] = (c * jax.nn.sigmoid(c)).astype(bf16)
    v_ref[...] = z[:, 0:M_WIDTH].T.astype(bf16)
    o_ref[...] = z[:, M_WIDTH:2 * M_WIDTH]
    base = 2 * M_WIDTH
    qa_ref[...] = z[:, base:base + A_WIDTH]
    ka_ref[...] = z[:, base + A_WIDTH:base + 2 * A_WIDTH]
    va_ref[...] = z[:, base + 2 * A_WIDTH:base + 3 * A_WIDTH].astype(bf16)


def _inproj(x2, g_mix, w_main, w_gate, b_gate, conv_w, seq_len):
    n = x2.shape[0]
    tm = TM_IN
    nb8 = n // SUBLANES
    r8 = tm // SUBLANES
    wcols = w_main.shape[1]
    row = lambda i: (i, 0)
    const = lambda i: (0, 0)
    out_shapes = (
        jax.ShapeDtypeStruct((n, 2 * M_WIDTH), bf16),
        jax.ShapeDtypeStruct((M_WIDTH, n), bf16),
        jax.ShapeDtypeStruct((n, M_WIDTH), f32),
        jax.ShapeDtypeStruct((N_GATES, n), f32),
        jax.ShapeDtypeStruct((n, A_WIDTH), f32),
        jax.ShapeDtypeStruct((n, A_WIDTH), f32),
        jax.ShapeDtypeStruct((n, A_WIDTH), bf16),
    )
    return pl.pallas_call(
        functools.partial(_inproj_kernel, tm=tm, seq_len=seq_len),
        grid=(n // tm,),
        in_specs=[
            pl.BlockSpec((tm, D_MODEL), row),
            pl.BlockSpec((SUBLANES, D_MODEL), lambda i: (jnp.maximum(i * r8 - 1, 0), 0)),
            pl.BlockSpec((SUBLANES, D_MODEL), lambda i: (jnp.minimum((i + 1) * r8, nb8 - 1), 0)),
            pl.BlockSpec((1, D_MODEL), const),
            pl.BlockSpec((D_MODEL, wcols), const),
            pl.BlockSpec((D_MODEL, LANES), const),
            pl.BlockSpec((1, LANES), const),
            pl.BlockSpec((3, 2 * M_WIDTH), const),
        ],
        out_specs=[
            pl.BlockSpec((tm, 2 * M_WIDTH), row),
            pl.BlockSpec((M_WIDTH, tm), lambda i: (0, i)),
            pl.BlockSpec((tm, M_WIDTH), row),
            pl.BlockSpec((N_GATES, tm), lambda i: (0, i)),
            pl.BlockSpec((tm, A_WIDTH), row),
            pl.BlockSpec((tm, A_WIDTH), row),
            pl.BlockSpec((tm, A_WIDTH), row),
        ],
        out_shape=out_shapes,
        compiler_params=_cparams(("parallel",)),
        name="inproj",
    )(x2, x2, x2, g_mix, w_main, w_gate, b_gate, conv_w)


def _mlstm_kernel(q_ref, k_ref, vt_ref, g_ref, out_ref, c_ref, m_ref, *, L):
    d = pl.program_id(1)
    c = pl.program_id(2)

    @pl.when(c == 0)
    def _():
        c_ref[...] = jnp.zeros_like(c_ref)
        m_ref[...] = jnp.zeros_like(m_ref)

    fwd = d == 0
    sgn = jnp.where(fwd, 1, -1)
    row = lax.broadcasted_iota(i32, (L, L), 0)
    col = lax.broadcasted_iota(i32, (L, L), 1)
    vis = (col - row) * sgn >= 0
    t_row = vis.astype(bf16)

    g = g_ref[...]
    h8 = M_HEADS
    li = jnp.where(fwd, g[0:h8, :], g[2 * h8:3 * h8, :])
    lf = _log_sigmoid(jnp.where(fwd, g[h8:2 * h8, :], g[3 * h8:4 * h8, :]))
    b = sum(jnp.dot(part, t_row, preferred_element_type=f32) for part in _split3(lf))
    bl = jnp.sum(lf, axis=1, keepdims=True)
    r_row = li - b
    pos = lax.broadcasted_iota(i32, (1, L), 1)
    cm = r_row
    sh = 1
    while sh < L:
        from_before = jnp.where(pos >= sh, pltpu.roll(cm, sh, 1), NEG_INF)
        from_after = jnp.where(pos < L - sh, pltpu.roll(cm, L - sh, 1), NEG_INF)
        cm = jnp.maximum(cm, jnp.where(fwd, from_before, from_after))
        sh *= 2
    m_prev = m_ref[:, 0:1]
    mx_r = jnp.maximum(m_prev, cm)
    m_new = bl + jnp.maximum(m_prev, jnp.max(r_row, axis=1, keepdims=True))
    decay = jnp.exp(bl + m_prev - m_new)
    m_ref[...] = jnp.broadcast_to(m_new, (h8, LANES))
    w_int = jnp.exp(m_prev - mx_r)
    floor = jnp.exp(-(b + mx_r))
    wk = jnp.exp(bl + r_row - m_new)
    r_col = jnp.concatenate([r_row, jnp.zeros((LANES - h8, L), f32)], axis=0).T

    lane = lax.broadcasted_iota(i32, (1, LANES), 1)
    sub = lax.broadcasted_iota(i32, (LANES, 1), 0)
    lane_half = (lane < HEAD_DIM, lane >= HEAD_DIM)
    sub_half = (sub < HEAD_DIM, sub >= HEAD_DIM)
    nt = (((1,), (1,)), ((), ()))

    heads = range(M_HEADS)
    pair = lambda h: slice((h // 2) * LANES, (h // 2 + 1) * LANES)
    qz = [jnp.where(lane_half[h % 2], q_ref[:, pair(h)] * (HEAD_DIM ** -0.5), 0).astype(bf16) for h in heads]
    kz = [jnp.where(lane_half[h % 2], k_ref[:, pair(h)], 0).astype(bf16) for h in heads]
    va_t = [jnp.where(sub_half[h % 2], vt_ref[pair(h), :], 1).astype(bf16) for h in heads]
    kq = [lax.dot_general(kz[h], qz[h], nt, preferred_element_type=f32) for h in heads]
    s_t = [jnp.where(vis, kq[h] * jnp.exp(r_col[:, h:h + 1] - mx_r[h:h + 1, :]), 0.0).astype(bf16) for h in heads]
    cst = [c_ref[h] for h in heads]
    nums = [w_int[h:h + 1, :] * lax.dot_general(cst[h].astype(bf16), qz[h], nt, preferred_element_type=f32)
            + jnp.dot(va_t[h], s_t[h], preferred_element_type=f32) for h in heads]
    for h in heads:
        vw = (va_t[h].astype(f32) * wk[h:h + 1, :]).astype(bf16)
        c_ref[h] = decay[h:h + 1, :] * cst[h] + jnp.dot(vw, kz[h], preferred_element_type=f32)
    for p in range(M_HEADS // 2):
        ev, od = 2 * p, 2 * p + 1
        numer = jnp.where(sub_half[0], nums[ev], nums[od])
        den = jnp.where(sub_half[0], nums[ev][HEAD_DIM:HEAD_DIM + 1, :], nums[od][0:1, :])
        lim = jnp.where(sub_half[0], floor[ev:ev + 1, :], floor[od:od + 1, :])
        out_ref[:, pair(ev)] = (numer / jnp.maximum(jnp.abs(den), lim)).T


def _mlstm(qk, v_t, gates_t, batch, seq_len):
    L = MLSTM_CHUNK
    nc = seq_len // L
    qk3 = qk.reshape(batch, seq_len, 2 * M_WIDTH)
    cidx = lambda d, c: jnp.where(d == 0, c, nc - 1 - c)
    return pl.pallas_call(
        functools.partial(_mlstm_kernel, L=L),
        grid=(batch, 2, nc),
        in_specs=[
            pl.BlockSpec((None, L, M_WIDTH), lambda b, d, c: (b, cidx(d, c), 0)),
            pl.BlockSpec((None, L, M_WIDTH), lambda b, d, c: (b, cidx(d, c), 1)),
            pl.BlockSpec((M_WIDTH, L), lambda b, d, c: (0, b * nc + cidx(d, c))),
            pl.BlockSpec((N_GATES, L), lambda b, d, c: (0, b * nc + cidx(d, c))),
        ],
        out_specs=pl.BlockSpec((None, None, L, M_WIDTH), lambda b, d, c: (b, d, cidx(d, c), 0)),
        out_shape=jax.ShapeDtypeStruct((batch, 2, seq_len, M_WIDTH), f32),
        scratch_shapes=[pltpu.VMEM((M_HEADS, LANES, LANES), f32), pltpu.VMEM((M_HEADS, LANES), f32)],
        compiler_params=_cparams(("parallel", "parallel", "arbitrary")),
        name="mlstm",
    )(qk3, qk3, v_t, gates_t)


def _natten_kernel(q_ref, k_ref, v_ref, bias_ref, gq_ref, gk_ref, out_ref, kn_ref, *, seq_len, rb_rows):
    rb = pl.program_id(2)
    rows = seq_len // GRID_W
    norm_rows = 512

    @pl.when(rb == 0)
    def _():
        def body(i, carry):
            sl = pl.ds(pl.multiple_of(i * norm_rows, norm_rows), norm_rows)
            kv = k_ref[sl, :]
            kn_ref[sl, :] = (kv * lax.rsqrt(_head_mean_sq(kv, LANES) + EPS) * gk_ref[...]).astype(bf16)
            return carry
        lax.fori_loop(0, seq_len // norm_rows, body, 0)

    qv = q_ref[...]
    qn = (qv * lax.rsqrt(_head_mean_sq(qv, LANES) + EPS) * gq_ref[...] * (HEAD_DIM ** -0.5)).astype(bf16)
    lane = lax.broadcasted_iota(i32, (1, LANES), 1)
    lo = lane < HEAD_DIM
    nt = (((1,), (1,)), ((), ()))
    pair = 2 * GRID_W

    def window(j):
        r = rb * rb_rows + j
        r0 = jnp.clip(r - WIN_H // 2, 0, rows - WIN_H)
        return r - r0, pl.ds(pl.multiple_of(r0 * GRID_W, GRID_W), WIN_H * GRID_W)

    batch_rows = rb_rows // NAT_BATCHES
    for b0 in range(0, rb_rows, batch_rows):
        tiles = []
        for j in range(b0, b0 + batch_rows):
            delta, ks = window(j)
            qj = qn[j * GRID_W:(j + 1) * GRID_W, :]
            q2 = jnp.concatenate([jnp.where(lo, qj, jnp.zeros_like(qj)), jnp.where(lo, jnp.zeros_like(qj), qj)],
                                 axis=0)
            tiles.append(lax.dot_general(q2, kn_ref[ks, :], nt, preferred_element_type=f32) + bias_ref[delta])
        s = jnp.concatenate(tiles, axis=0)
        p = jnp.exp(s - jnp.max(s, axis=-1, keepdims=True))
        inv = 1.0 / jnp.sum(p, axis=-1, keepdims=True)
        pb = p.astype(bf16)
        for jj in range(batch_rows):
            j = b0 + jj
            _, ks = window(j)
            o = jnp.dot(pb[jj * pair:(jj + 1) * pair, :], v_ref[ks, :], preferred_element_type=f32)
            o = o * inv[jj * pair:(jj + 1) * pair, :]
            out_ref[j * GRID_W:(j + 1) * GRID_W, :] = jnp.where(lo, o[:GRID_W, :], o[GRID_W:, :])


def _natten_bias_table(rel_bias):
    cq = jnp.arange(GRID_W)[:, None]
    ck = jnp.arange(GRID_W)[None, :]
    c0 = jnp.clip(cq - WIN_W // 2, 0, GRID_W - WIN_W)
    col_in = (ck >= c0) & (ck < c0 + WIN_W)
    idx_c = jnp.clip(ck - cq, -(WIN_W - 1), WIN_W - 1) + (WIN_W - 1)
    pick = idx_c[:, :, None] == jnp.arange(2 * WIN_W - 1)
    tz = jnp.sum(jnp.where(pick[None, None], rel_bias.astype(f32)[:, :, None, None, :], 0.0), axis=-1)
    tz = jnp.where(col_in[None, None], tz, NEG_INF)
    tab = jnp.stack([tz[:, WIN_H - 1 - dl:2 * WIN_H - 1 - dl] for dl in range(WIN_H)], axis=1)
    tab = tab.transpose(0, 1, 3, 2, 4).reshape(A_HEADS // 2, 2, WIN_H, GRID_W, WIN_H * GRID_W)
    return tab.transpose(0, 2, 1, 3, 4).reshape(A_HEADS // 2, WIN_H, 2 * GRID_W, WIN_H * GRID_W)


def _natten(qa, ka, va, bias_tab, g_q2, g_k2, batch, seq_len):
    rows = seq_len // GRID_W
    rbr = NAT_ROWS
    tq = rbr * GRID_W
    q3 = qa.reshape(batch, seq_len, A_WIDTH)
    k3 = ka.reshape(batch, seq_len, A_WIDTH)
    v3 = va.reshape(batch, seq_len, A_WIDTH)
    return pl.pallas_call(
        functools.partial(_natten_kernel, seq_len=seq_len, rb_rows=rbr),
        grid=(batch, A_HEADS // 2, rows // rbr),
        in_specs=[
            pl.BlockSpec((None, tq, LANES), lambda b, hp, rb: (b, rb, hp)),
            pl.BlockSpec((None, seq_len, LANES), lambda b, hp, rb: (b, 0, hp)),
            pl.BlockSpec((None, seq_len, LANES), lambda b, hp, rb: (b, 0, hp)),
            pl.BlockSpec((None, WIN_H, 2 * GRID_W, WIN_H * GRID_W), lambda b, hp, rb: (hp, 0, 0, 0)),
            pl.BlockSpec((1, LANES), lambda b, hp, rb: (0, 0)),
            pl.BlockSpec((1, LANES), lambda b, hp, rb: (0, 0)),
        ],
        out_specs=pl.BlockSpec((None, tq, LANES), lambda b, hp, rb: (b, rb, hp)),
        out_shape=jax.ShapeDtypeStruct((batch, seq_len, A_WIDTH), f32),
        scratch_shapes=[pltpu.VMEM((seq_len, LANES), bf16)],
        compiler_params=_cparams(("parallel", "parallel", "arbitrary")),
        name="natten",
    )(q3, k3, v3, bias_tab, g_q2, g_k2)


def _outproj_kernel(x_ref, hf_ref, hb_ref, o_ref, ha_ref, gm_ref, ga_ref, wo_ref, gf_ref,
                    wr1_ref, wr2_ref, br_ref,
                    xmid_ref, hn_ref, ids_ref, pos_ref, gate_ref, cnt_ref, carry_ref, *, tm):
    i = pl.program_id(0)

    @pl.when(i == 0)
    def _():
        carry_ref[...] = jnp.zeros_like(carry_ref)

    hm = hf_ref[...] + hb_ref[...]
    hm = hm * lax.rsqrt(_head_mean_sq(hm, M_WIDTH) + EPS) * gm_ref[...] * jax.nn.sigmoid(o_ref[...])
    ha = _rms(ha_ref[...], ga_ref[...])
    mix = (jnp.dot(hm.astype(bf16), wo_ref[0:M_WIDTH, :], preferred_element_type=f32)
           + jnp.dot(ha.astype(bf16), wo_ref[M_WIDTH:M_WIDTH + A_WIDTH, :], preferred_element_type=f32))
    xm = x_ref[...] + mix
    xmid_ref[...] = xm
    hn = _rms(xm, gf_ref[...])
    _store_token_tiles(hn_ref, hn)

    h1, h2 = _split2(hn)
    logits = (jnp.dot(h1, wr1_ref[...], preferred_element_type=f32)
              + (jnp.dot(h1, wr2_ref[...], preferred_element_type=f32)
                 + jnp.dot(h2, wr1_ref[...], preferred_element_type=f32))) + br_ref[...]
    lane = lax.broadcasted_iota(i32, (tm, LANES), 1)
    work = logits
    vals, idxs, sels = [], [], []
    for _ in range(TOP_K):
        mx = jnp.max(work, axis=-1, keepdims=True)
        idx = jnp.min(jnp.where(work == mx, lane, LANES), axis=-1, keepdims=True)
        sel = lane == idx
        vals.append(mx)
        idxs.append(idx)
        sels.append(sel)
        work = jnp.where(sel, NEG_INF, work)
    es = [jnp.exp(v - vals[0]) for v in vals]
    tot = es[0] + es[1] + es[2] + es[3]

    onehot = jnp.where(sels[0] | sels[1] | sels[2] | sels[3], 1.0, 0.0)
    tri = (lax.broadcasted_iota(i32, (tm, tm), 0) > lax.broadcasted_iota(i32, (tm, tm), 1)).astype(bf16)
    base = jnp.dot(tri, onehot.astype(bf16), preferred_element_type=f32) + carry_ref[...]
    ids_out = jnp.zeros((tm, LANES), i32)
    pos_out = jnp.zeros((tm, LANES), i32)
    gate_out = jnp.zeros((tm, LANES), f32)
    for k in range(TOP_K):
        pk = jnp.sum(jnp.where(sels[k], base, 0.0), axis=-1, keepdims=True).astype(i32)
        ids_out = jnp.where(lane == k, idxs[k], ids_out)
        pos_out = jnp.where(lane == k, pk, pos_out)
        gate_out = jnp.where(lane == k, es[k] / tot, gate_out)
    ids_ref[...] = ids_out.T[:SUBLANES, :]
    pos_ref[...] = pos_out.T[:SUBLANES, :]
    gate_ref[...] = gate_out[:, :TOP_K]
    carry_ref[...] += jnp.sum(onehot, axis=0, keepdims=True)
    cnt_ref[...] = carry_ref[...]


def _outproj(x2, hfb, o_m, ha, g_m, g_a, w_out, g_ffn, wr1, wr2, b_r, batch, seq_len):
    n = x2.shape[0]
    tm = TM_OUT
    tpb = seq_len // tm
    row = lambda i: (i, 0)
    const = lambda i: (0, 0)
    ha2 = ha.reshape(n, A_WIDTH)
    return pl.pallas_call(
        functools.partial(_outproj_kernel, tm=tm),
        grid=(n // tm,),
        in_specs=[
            pl.BlockSpec((tm, D_MODEL), row),
            pl.BlockSpec((None, None, tm, M_WIDTH), lambda i: (i // tpb, 0, i % tpb, 0)),
            pl.BlockSpec((None, None, tm, M_WIDTH), lambda i: (i // tpb, 1, i % tpb, 0)),
            pl.BlockSpec((tm, M_WIDTH), row),
            pl.BlockSpec((tm, A_WIDTH), row),
            pl.BlockSpec((1, M_WIDTH), const),
            pl.BlockSpec((1, A_WIDTH), const),
            pl.BlockSpec((M_WIDTH + A_WIDTH, D_MODEL), const),
            pl.BlockSpec((1, D_MODEL), const),
            pl.BlockSpec((D_MODEL, LANES), const),
            pl.BlockSpec((D_MODEL, LANES), const),
            pl.BlockSpec((1, LANES), const),
        ],
        out_specs=[
            pl.BlockSpec((tm, D_MODEL), row),
            pl.BlockSpec((tm * TOKEN_TILE_ROWS, LANES), row),
            pl.BlockSpec((SUBLANES, tm), lambda i: (0, i)),
            pl.BlockSpec((SUBLANES, tm), lambda i: (0, i)),
            pl.BlockSpec((tm, TOP_K), row),
            pl.BlockSpec((1, LANES), const),
        ],
        out_shape=(
            jax.ShapeDtypeStruct((n, D_MODEL), f32),
            jax.ShapeDtypeStruct((n * TOKEN_TILE_ROWS, LANES), f32),
            jax.ShapeDtypeStruct((SUBLANES, n), i32),
            jax.ShapeDtypeStruct((SUBLANES, n), i32),
            jax.ShapeDtypeStruct((n, TOP_K), f32),
            jax.ShapeDtypeStruct((1, LANES), f32),
        ),
        scratch_shapes=[pltpu.VMEM((1, LANES), f32)],
        compiler_params=_cparams(("arbitrary",)),
        name="outproj_router",
    )(x2, hfb, hfb, o_m, ha2, g_m, g_a, w_out, g_ffn, wr1, wr2, b_r)


def _scatter_clear(pend_ref, padded_ref, xb_hbm, zero_ref, sem, tm_e, n_blocks):
    blk = tm_e * TOKEN_TILE_ROWS

    def zero_copy(e):
        start = (pend_ref[e] - tm_e) * TOKEN_TILE_ROWS
        return pltpu.make_async_copy(zero_ref, xb_hbm.at[pl.ds(pl.multiple_of(start, blk), blk)], sem)

    def tail_copy(b):
        return pltpu.make_async_copy(zero_ref, xb_hbm.at[pl.ds(pl.multiple_of(b * blk, blk), blk)], sem)

    zero_ref[...] = jnp.zeros_like(zero_ref)
    for e in range(N_EXPERTS):
        @pl.when(padded_ref[e] > 0)
        def _():
            zero_copy(e).start()
    used = pend_ref[N_EXPERTS - 1] // tm_e

    def tail_start(b, carry):
        tail_copy(b).start()
        return carry

    def tail_wait(b, carry):
        tail_copy(b).wait()
        return carry

    lax.fori_loop(used, n_blocks, tail_start, 0)
    for e in range(N_EXPERTS):
        @pl.when(padded_ref[e] > 0)
        def _():
            zero_copy(e).wait()
    lax.fori_loop(used, n_blocks, tail_wait, 0)


def _scatter_rows(slot_ref, hn_ref, xb_hbm, sem, tm):
    def row(j, k):
        return pltpu.make_async_copy(_token_tile(hn_ref, j), _token_tile(xb_hbm, slot_ref[0, 0, j * TOP_K + k]), sem)

    def issue(j, carry):
        for k in range(TOP_K):
            row(j, k).start(priority=k % 2)
        return carry

    def drain(j, carry):
        for k in range(TOP_K):
            row(0, k).wait()
        return carry

    start = lambda: lax.fori_loop(0, tm, issue, 0, unroll=4)
    wait = lambda: lax.fori_loop(0, tm, drain, 0, unroll=4)
    return start, wait


def _dispatch_kernel(pend_ref, padded_ref, slot_ref, hn_ref, w_ref, xb_hbm, wout_ref, zero_ref, sem, *,
                     tm, tm_e, n_blocks):
    @pl.when(pl.program_id(0) == 0)
    def _():
        _scatter_clear(pend_ref, padded_ref, xb_hbm, zero_ref, sem, tm_e, n_blocks)

    start, wait = _scatter_rows(slot_ref, hn_ref, xb_hbm, sem, tm)
    start()
    _wprep_kernel(w_ref, wout_ref)
    wait()


def _dispatch(hn8, slot_flat, pend, padded, cap, w_gate_up):
    n = hn8.shape[0] // TOKEN_TILE_ROWS
    tm = TM_DISPATCH
    steps = n // tm
    ne = w_gate_up.shape[0]
    halves = steps // ne
    assert halves * ne == steps and D_MODEL % halves == 0, "one weight slab per grid step"
    wspec = pl.BlockSpec((None, D_MODEL // halves, 2 * D_FF), lambda i, pe, pa: (i // halves, i % halves, 0))
    grid_spec = pltpu.PrefetchScalarGridSpec(
        num_scalar_prefetch=2,
        grid=(steps,),
        in_specs=[
            pl.BlockSpec((1, 1, tm * TOP_K), lambda i, pe, pa: (i, 0, 0), memory_space=pltpu.SMEM),
            pl.BlockSpec((tm * TOKEN_TILE_ROWS, LANES), lambda i, pe, pa: (i, 0)),
            wspec,
        ],
        out_specs=[pl.BlockSpec(memory_space=pl.ANY), wspec],
        scratch_shapes=[pltpu.VMEM((TM_EXP * TOKEN_TILE_ROWS, LANES), f32), pltpu.SemaphoreType.DMA(())],
    )
    return pl.pallas_call(
        functools.partial(_dispatch_kernel, tm=tm, tm_e=TM_EXP, n_blocks=cap // TM_EXP),
        grid_spec=grid_spec,
        out_shape=[jax.ShapeDtypeStruct((cap * TOKEN_TILE_ROWS, LANES), f32),
                   jax.ShapeDtypeStruct((ne, D_MODEL, 2 * D_FF), bf16)],
        compiler_params=_cparams(("arbitrary",)),
        name="dispatch_wprep",
    )(pend, padded, slot_flat.reshape(steps, 1, tm * TOP_K), hn8, w_gate_up)


def _experts_kernel(be_ref, nv_ref, *rest, tm, n_blocks, scatter_steps):
    if scatter_steps:
        (pend2_ref, padded2_ref, xb_ref, wgu_ref, bgu_ref, wd_ref, bd_ref, slot2_ref, hn2_ref,
         yb_ref, xb2_hbm, zero_ref, sem) = rest
        start, wait = _scatter_rows(slot2_ref, hn2_ref, xb2_hbm, sem, tm // TOP_K)
    else:
        xb_ref, wgu_ref, bgu_ref, wd_ref, bd_ref, yb_ref = rest
    j = pl.program_id(0)

    if scatter_steps:
        @pl.when(j == 0)
        def _():
            _scatter_clear(pend2_ref, padded2_ref, xb2_hbm, zero_ref, sem, tm, n_blocks)

        @pl.when(j < scatter_steps)
        def _():
            start()

    @pl.when(j < nv_ref[0])
    def _():
        xv = _load_token_tiles(xb_ref, tm).astype(bf16)
        h = jnp.dot(xv, wgu_ref[...], preferred_element_type=f32) + bgu_ref[...]
        gt = jnp.minimum(h[:, :D_FF], SWIGLU_LIMIT)
        up = jnp.clip(h[:, D_FF:], -SWIGLU_LIMIT, SWIGLU_LIMIT)
        act = (up + 1.0) * (gt * jax.nn.sigmoid(SWIGLU_ALPHA * gt))
        _store_token_tiles(yb_ref, jnp.dot(act.astype(bf16), wd_ref[...], preferred_element_type=f32) + bd_ref[...])

    @pl.when(j >= nv_ref[0])
    def _():
        yb_ref[...] = jnp.zeros_like(yb_ref)

    if scatter_steps:
        @pl.when(j < scatter_steps)
        def _():
            wait()


def _experts(xb, block_e, nvalid, w_gu, b_gu, w_d, b_d, scatter=None):
    cap = xb.shape[0] // TOKEN_TILE_ROWS
    tm = TM_EXP
    n_blocks = cap // tm
    nsp = 4 if scatter is not None else 2
    blk = lambda f: (lambda j, *pref: f(j, *pref[:2]))
    in_specs = [
        pl.BlockSpec((tm * TOKEN_TILE_ROWS, LANES), blk(lambda j, be, nv: (jnp.minimum(j, nv[0] - 1), 0))),
        pl.BlockSpec((None, D_MODEL, 2 * D_FF), blk(lambda j, be, nv: (be[j], 0, 0))),
        pl.BlockSpec((None, 1, 2 * D_FF), blk(lambda j, be, nv: (be[j], 0, 0))),
        pl.BlockSpec((None, D_FF, D_MODEL), blk(lambda j, be, nv: (be[j], 0, 0))),
        pl.BlockSpec((None, 1, D_MODEL), blk(lambda j, be, nv: (be[j], 0, 0))),
    ]
    out_specs = [pl.BlockSpec((tm * TOKEN_TILE_ROWS, LANES), blk(lambda j, be, nv: (j, 0)))]
    out_shape = [jax.ShapeDtypeStruct((cap * TOKEN_TILE_ROWS, LANES), f32)]
    prefetch = [block_e, nvalid]
    operands = [xb, w_gu, b_gu, w_d, b_d]
    scratch = []
    steps = 0
    if scatter is not None:
        hn8, slot_flat, pend2, padded2 = scatter
        tms = tm // TOP_K
        steps = hn8.shape[0] // TOKEN_TILE_ROWS // tms
        assert steps * tms * TOKEN_TILE_ROWS == hn8.shape[0] and steps <= n_blocks
        last = steps - 1
        prefetch += [pend2, padded2]
        in_specs += [
            pl.BlockSpec((1, 1, tm), blk(lambda j, be, nv: (jnp.minimum(j, last), 0, 0)), memory_space=pltpu.SMEM),
            pl.BlockSpec((tms * TOKEN_TILE_ROWS, LANES), blk(lambda j, be, nv: (jnp.minimum(j, last), 0))),
        ]
        operands += [slot_flat.reshape(steps, 1, tm), hn8]
        out_specs.append(pl.BlockSpec(memory_space=pl.ANY))
        out_shape.append(jax.ShapeDtypeStruct((cap * TOKEN_TILE_ROWS, LANES), f32))
        scratch = [pltpu.VMEM((tm * TOKEN_TILE_ROWS, LANES), f32), pltpu.SemaphoreType.DMA(())]
    grid_spec = pltpu.PrefetchScalarGridSpec(
        num_scalar_prefetch=nsp, grid=(n_blocks,), in_specs=in_specs, out_specs=out_specs, scratch_shapes=scratch)
    outs = pl.pallas_call(
        functools.partial(_experts_kernel, tm=tm, n_blocks=n_blocks, scatter_steps=steps),
        grid_spec=grid_spec,
        out_shape=out_shape,
        compiler_params=_cparams(("arbitrary",)),
        name="experts_scatter" if scatter is not None else "experts",
    )(*prefetch, *operands)
    return outs if scatter is not None else outs[0]


def _combine_kernel(slot_ref, slot_next_ref, xmid_ref, gate_ref, p_ref, wproj_ref, gpost_ref, gple_ref, wgate_ref,
                    yb_hbm, out_ref, ybuf_ref, y_ref, sem, *, tm):
    i = pl.program_id(0)
    cur = lax.rem(i, 2)
    nxt = 1 - cur

    def row(slots, buf, j, k):
        return pltpu.make_async_copy(_token_tile(yb_hbm, slots[0, 0, j * TOP_K + k]),
                                     _token_tile(ybuf_ref.at[buf, k], j), sem.at[buf])

    def drain(buf):
        def wait(j, carry):
            for k in range(TOP_K):
                row(slot_ref, buf, 0, k).wait()
            return carry
        lax.fori_loop(0, tm, wait, 0, unroll=4)

    @pl.when(i == 0)
    def _():
        def issue(j, carry):
            for k in range(TOP_K):
                row(slot_ref, 0, j, k).start(priority=k % 2)
            return carry
        lax.fori_loop(0, tm, issue, 0, unroll=4)

    pe = _rms(jnp.dot(p_ref[...].astype(bf16), wproj_ref[...], preferred_element_type=f32), gpost_ref[...])
    drain(cur)

    half_chunk = CMB_CHUNK // 2

    def chunk_body(cb, carry):
        j0 = pl.multiple_of(cb * CMB_CHUNK, CMB_CHUNK)
        for jj in range(half_chunk):
            for k in range(TOP_K):
                row(slot_next_ref, nxt, cb * half_chunk + jj, k).start(priority=k % 2)
        gate = gate_ref[pl.ds(j0, CMB_CHUNK), :]
        acc = gate[:, 0:1] * _load_token_tiles(ybuf_ref.at[cur, 0], CMB_CHUNK, j0)
        for k in range(1, TOP_K):
            acc = acc + gate[:, k:k + 1] * _load_token_tiles(ybuf_ref.at[cur, k], CMB_CHUNK, j0)
        y_ref[pl.ds(j0, CMB_CHUNK), :] = acc
        return carry

    lax.fori_loop(0, tm // CMB_CHUNK, chunk_body, 0)
    for j in range(tm // 2, tm):
        for k in range(TOP_K):
            row(slot_next_ref, nxt, j, k).start(priority=k % 2)

    x2 = xmid_ref[...] + y_ref[...]
    gl = jnp.dot(_rms(x2, gple_ref[...]).astype(bf16), wgate_ref[...], preferred_element_type=f32)
    out_ref[...] = x2 + jax.nn.sigmoid(gl) * pe

    @pl.when(i + 1 == pl.num_programs(0))
    def _():
        drain(nxt)


def _combine(slot_flat, xmid, gate, p2, w_proj, g_post, g_ple, w_gate, yb):
    n = xmid.shape[0]
    tm = TM_CMB
    steps = n // tm
    slot3 = slot_flat.reshape(steps, 1, tm * TOP_K)
    row = lambda i: (i, 0)
    const = lambda i: (0, 0)
    return pl.pallas_call(
        functools.partial(_combine_kernel, tm=tm),
        grid=(steps,),
        in_specs=[
            pl.BlockSpec((1, 1, tm * TOP_K), lambda i: (i, 0, 0), memory_space=pltpu.SMEM),
            pl.BlockSpec((1, 1, tm * TOP_K), lambda i: (jnp.minimum(i + 1, steps - 1), 0, 0),
                         memory_space=pltpu.SMEM),
            pl.BlockSpec((tm, D_MODEL), row),
            pl.BlockSpec((tm, TOP_K), row),
            pl.BlockSpec((tm, PLE_DIM), row),
            pl.BlockSpec((PLE_DIM, D_MODEL), const),
            pl.BlockSpec((1, D_MODEL), const),
            pl.BlockSpec((1, D_MODEL), const),
            pl.BlockSpec((D_MODEL, D_MODEL), const),
            pl.BlockSpec(memory_space=pl.ANY),
        ],
        out_specs=pl.BlockSpec((tm, D_MODEL), row),
        out_shape=jax.ShapeDtypeStruct((n, D_MODEL), f32),
        scratch_shapes=[pltpu.VMEM((2, TOP_K, tm * TOKEN_TILE_ROWS, LANES), f32), pltpu.VMEM((tm, D_MODEL), f32),
                        pltpu.SemaphoreType.DMA((2,))],
        compiler_params=_cparams(("arbitrary",)),
        name="combine_ple",
    )(slot3, slot3, xmid, gate, p2, w_proj, g_post, g_ple, w_gate, yb)


def _wprep_kernel(w_ref, out_ref):
    grp = 2 * LANES
    src = lax.broadcasted_iota(i32, (grp, grp), 0)
    dst = lax.broadcasted_iota(i32, (grp, grp), 1)
    want = jnp.where(dst < LANES, 2 * dst, 2 * (dst - LANES) + 1)
    perm = (src == want).astype(bf16)
    for g in range(2 * D_FF // grp):
        t = jnp.dot(w_ref[:, g * grp:(g + 1) * grp].astype(bf16), perm, preferred_element_type=f32)
        out_ref[:, g * LANES:(g + 1) * LANES] = t[:, :LANES].astype(bf16)
        out_ref[:, D_FF + g * LANES:D_FF + (g + 1) * LANES] = t[:, LANES:].astype(bf16)


def _prep_weights(g_mix, w_in, b_gates, conv_w, g_m_head, g_q, g_k, rel_bias, g_a_out, w_out, g_ffn,
                  w_router, b_router, w_gate_up, b_gate_up, w_down, b_down, g_ple, w_ple_gate, w_ple_proj,
                  g_ple_post):
    g0 = 4 * M_WIDTH
    w = w_in[0]
    pw = {}
    pw["g_mix"] = g_mix[0][None, :]
    pw["w_main"] = jnp.concatenate([w[:, :g0], w[:, g0 + N_GATES:]], axis=1).astype(bf16)
    pw["w_gate"] = jnp.pad(w[:, g0:g0 + N_GATES], ((0, 0), (0, LANES - N_GATES))).astype(bf16)
    pw["b_gate"] = jnp.pad(b_gates[0], (0, LANES - N_GATES))[None, :]
    pw["conv_w"] = conv_w[0]
    pw["g_m"] = g_m_head[0].reshape(1, M_WIDTH)
    pw["g_q2"] = jnp.tile(g_q[0], 2)[None, :]
    pw["g_k2"] = jnp.tile(g_k[0], 2)[None, :]
    pw["bias_tab"] = _natten_bias_table(rel_bias[0])
    pw["g_a"] = g_a_out[0][None, :]
    pw["w_out"] = w_out[0].astype(bf16)
    pw["g_ffn"] = g_ffn[0][None, :]
    wr = jnp.pad(w_router[0], ((0, 0), (0, LANES - N_EXPERTS)))
    wr1 = wr.astype(bf16)
    pw["wr1"] = wr1
    pw["wr2"] = (wr - wr1.astype(f32)).astype(bf16)
    pw["b_r"] = jnp.pad(b_router[0], (0, LANES - N_EXPERTS), constant_values=NEG_INF)[None, :]
    pw["w_gate_up"] = w_gate_up[0]
    bgu = b_gate_up[0]
    pw["b_gu"] = jnp.concatenate([bgu[:, 0::2], bgu[:, 1::2]], axis=-1)[:, None, :]
    pw["w_d"] = w_down[0].astype(bf16)
    pw["b_d"] = b_down[0][:, None, :]
    pw["g_ple"] = g_ple[0][None, :]
    pw["w_ple_gate"] = w_ple_gate[0].astype(bf16)
    pw["w_ple_proj"] = w_ple_proj[0].astype(bf16)
    pw["g_ple_post"] = g_ple_post[0][None, :]
    return pw


def _mix_and_route(x, pw):
    batch, seq_len, _ = x.shape
    n = batch * seq_len
    x2 = x.reshape(n, D_MODEL)
    qk, v_m, o_m, gates, qa, ka, va = _inproj(x2, pw["g_mix"], pw["w_main"], pw["w_gate"], pw["b_gate"],
                                              pw["conv_w"], seq_len)
    hfb = _mlstm(qk, v_m, gates, batch, seq_len)
    ha = _natten(qa, ka, va, pw["bias_tab"], pw["g_q2"], pw["g_k2"], batch, seq_len)
    xmid, hn, ids, pos, gate, cnt = _outproj(x2, hfb, o_m, ha, pw["g_m"], pw["g_a"], pw["w_out"], pw["g_ffn"],
                                             pw["wr1"], pw["wr2"], pw["b_r"], batch, seq_len)

    tm_e = TM_EXP
    counts = cnt[0, :N_EXPERTS].astype(i32)
    padded = (counts + tm_e - 1) // tm_e * tm_e
    pend = jnp.cumsum(padded).astype(i32)
    pstart = pend - padded
    ids_t = ids[:TOP_K, :]
    first = jnp.zeros_like(ids_t)
    for e in range(N_EXPERTS):
        first = jnp.where(ids_t == e, pstart[e], first)
    slot = (first + pos[:TOP_K, :]).T.reshape(-1)
    nk = n * TOP_K
    n_blocks = (nk + N_EXPERTS * (tm_e - 1) + tm_e - 1) // tm_e
    cap = n_blocks * tm_e
    block_start = jnp.arange(n_blocks, dtype=i32) * tm_e
    block_e = jnp.minimum(jnp.sum((pend[None, :] <= block_start[:, None]).astype(i32), axis=1), N_EXPERTS - 1)
    nvalid = (pend[-1:] // tm_e).astype(i32)

    return dict(xmid=xmid, hn=hn, gate=gate, slot=slot, pend=pend, padded=padded, block_e=block_e, nvalid=nvalid,
                cap=cap)


def _finish(r, yb, p, pw, shape):
    n = r["xmid"].shape[0]
    out = _combine(r["slot"], r["xmid"], r["gate"], p.reshape(n, PLE_DIM), pw["w_ple_proj"], pw["g_ple_post"],
                   pw["g_ple"], pw["w_ple_gate"], yb)
    return out.reshape(shape)


def _layer(x_a, p_a, x_b, p_b, pw):
    ra = _mix_and_route(x_a, pw)
    xb_a, w_gu = _dispatch(ra["hn"], ra["slot"], ra["pend"], ra["padded"], ra["cap"], pw["w_gate_up"])
    rb = _mix_and_route(x_b, pw)
    assert rb["cap"] == ra["cap"], "the fused scatter writes a buffer of trunk a's size"
    yb_a, xb_b = _experts(xb_a, ra["block_e"], ra["nvalid"], w_gu, pw["b_gu"], pw["w_d"], pw["b_d"],
                          scatter=(rb["hn"], rb["slot"], rb["pend"], rb["padded"]))
    yb_b = _experts(xb_b, rb["block_e"], rb["nvalid"], w_gu, pw["b_gu"], pw["w_d"], pw["b_d"])
    return _finish(ra, yb_a, p_a, pw, x_a.shape), _finish(rb, yb_b, p_b, pw, x_b.shape)


def kernel(x_prompt, x_sample, p_prompt, p_sample, g_mix, w_in, b_gates, conv_w, g_m_head, g_q, g_k, rel_bias,
           g_a_out, w_out, g_ffn, w_router, b_router, w_gate_up, b_gate_up, w_down, b_down, g_ple, w_ple_gate,
           w_ple_proj, g_ple_post):
    assert w_in.shape[0] == 1, "single-layer trunk"
    pw = _prep_weights(g_mix, w_in, b_gates, conv_w, g_m_head, g_q, g_k, rel_bias, g_a_out, w_out, g_ffn,
                       w_router, b_router, w_gate_up, b_gate_up, w_down, b_down, g_ple, w_ple_gate, w_ple_proj,
                       g_ple_post)
    return _layer(x_prompt, p_prompt[0], x_sample, p_sample[0], pw)
```
